```python
import math
import jax, jax.numpy as jnp
from jax import lax
import numpy as np

D_MODEL = 1024
BATCH = 8
SEQ = 4096
DEPTH = 2

MEM_LEN = 256
Q_BLOCK = 128
EPS = 1e-6
NEG = -1e30
FFN_RESIDUAL = 0.5
D_FF = 2816

SB_HEADS = 8
SB_HEAD_DIM = 64
SB_WIDTH = SB_HEADS * SB_HEAD_DIM

MLA_HEADS = 8
MLA_Q_RANK = 384
MLA_KV_RANK = 128
MLA_NOPE_DIM = 64
MLA_ROPE_DIM = 32
MLA_V_DIM = 64
MLA_WIDTH = MLA_HEADS * MLA_V_DIM
ROPE_THETA = 10000.0

DIL_GROUPS = ((128, 1), (512, 4), (2048, 16))
DIL_HEADS = 8
DIL_HEAD_DIM = 64
DIL_WIDTH = DIL_HEADS * DIL_HEAD_DIM

REL_BUCKETS = 32
REL_MAX_DIST = 2048

X_HEADS = 4
X_HEAD_DIM = 128
X_WIDTH = X_HEADS * X_HEAD_DIM

N_BRANCH = 3
IN_SPLITS = ([SB_WIDTH] * 3
             + [MLA_Q_RANK, MLA_KV_RANK, MLA_ROPE_DIM]
             + [DIL_WIDTH] * (3 * len(DIL_GROUPS))
             + [D_MODEL] * N_BRANCH)
N_IN = sum(IN_SPLITS)
IN_OFFSETS = [int(o) for o in np.cumsum(IN_SPLITS)[:-1]]

kernel_name = "hybrid_gated_sb_mla_dilated_macaron"


def rms_norm(x, g):
    xf = x.astype(jnp.float32)
    y = xf * lax.rsqrt(jnp.mean(xf * xf, axis=-1, keepdims=True) + EPS)
    return (y * g.astype(jnp.float32)).astype(x.dtype)


def swiglu(x, w_gate, w_up, w_down):
    return (jax.nn.silu(x @ w_gate) * (x @ w_up)) @ w_down


def rope(x, pos):
    half = x.shape[-1] // 2
    freqs = ROPE_THETA ** (-jnp.arange(half, dtype=jnp.float32) / half)
    ang = pos.astype(jnp.float32)[:, None] * freqs[None, :]
    cos = jnp.cos(ang)[:, None, :]
    sin = jnp.sin(ang)[:, None, :]
    x1, x2 = x[..., :half], x[..., half:]
    return jnp.concatenate([x1 * cos - x2 * sin, x1 * sin + x2 * cos], axis=-1).astype(x.dtype)


def to_blocks(t):
    b, s = t.shape[:2]
    return t.reshape(b, s // Q_BLOCK, Q_BLOCK, *t.shape[2:]).swapaxes(0, 1)


def from_blocks(t):
    t = t.swapaxes(0, 1)
    return t.reshape(t.shape[0], t.shape[1] * t.shape[2], *t.shape[3:])


def stick_breaking_attention(q, k, v):
    s = q.shape[1]
    scale = q.shape[-1] ** -0.5
    key_pos = jnp.arange(s)

    def block(args):
        qb, q0 = args
        z = jnp.einsum('bqhd,bkhd->bhqk', qb, k).astype(jnp.float32) * scale
        strict = key_pos[None, :] < (q0 + jnp.arange(Q_BLOCK))[:, None]
        log_beta = jax.nn.log_sigmoid(z)
        log_keep = jnp.where(strict, jax.nn.log_sigmoid(-z), 0.0)
        after = lax.cumsum(log_keep, axis=3, reverse=True) - log_keep
        w = jnp.where(strict, jnp.exp(log_beta + after), 0.0)
        return jnp.einsum('bhqk,bkhd->bqhd', w.astype(v.dtype), v)

    starts = jnp.arange(s // Q_BLOCK, dtype=jnp.int32) * Q_BLOCK
    return from_blocks(lax.map(block, (to_blocks(q), starts)))


def causal_softmax_attention(q, k, v, scale):
    s = q.shape[1]
    key_pos = jnp.arange(s)

    def block(args):
        qb, q0 = args
        logits = jnp.einsum('bqhd,bkhd->bhqk', qb, k).astype(jnp.float32) * scale
        causal = key_pos[None, :] <= (q0 + jnp.arange(Q_BLOCK))[:, None]
        p = jax.nn.softmax(jnp.where(causal, logits, NEG), axis=-1)
        return jnp.einsum('bhqk,bkhd->bqhd', p.astype(v.dtype), v)

    starts = jnp.arange(s // Q_BLOCK, dtype=jnp.int32) * Q_BLOCK
    return from_blocks(lax.map(block, (to_blocks(q), starts)))


def rel_bucket(dist):
    exact = REL_BUCKETS // 2
    d = jnp.maximum(dist, exact).astype(jnp.float32)
    large = exact + (jnp.log(d / exact) / math.log(REL_MAX_DIST / exact)
                     * (REL_BUCKETS - exact)).astype(jnp.int32)
    return jnp.where(dist < exact, dist, jnp.minimum(large, REL_BUCKETS - 1))


def dilated_group_attention(q, k, v, dilation, taps, bias_table):
    b, s, h, d = q.shape
    r = dilation
    L = s // r
    nb = -(-L // Q_BLOCK)
    Lp = nb * Q_BLOCK

    def residue_major(t):
        return t.reshape(b, L, r, h, d).swapaxes(1, 2)

    def pad(t, front):
        return jnp.pad(t, ((0, 0), (0, 0), (front, Lp - L), (0, 0), (0, 0)))

    qr = pad(residue_major(q), 0).reshape(b, r, nb, Q_BLOCK, h, d)
    kr = pad(residue_major(k), Q_BLOCK).reshape(b, r, nb + 1, Q_BLOCK, h, d)
    vr = pad(residue_major(v), Q_BLOCK).reshape(b, r, nb + 1, Q_BLOCK, h, d)
    kb = jnp.concatenate([kr[:, :, :-1], kr[:, :, 1:]], axis=3)
    vb = jnp.concatenate([vr[:, :, :-1], vr[:, :, 1:]], axis=3)

    logits = jnp.einsum('brnqhd,brnkhd->brnhqk', qr, kb).astype(jnp.float32) * d ** -0.5
    qi = jnp.arange(Q_BLOCK)
    kj = jnp.arange(2 * Q_BLOCK)
    steps = (qi[:, None] + Q_BLOCK) - kj[None, :]
    in_window = (steps >= 0) & (steps <= taps)
    key_idx = jnp.arange(nb)[:, None, None] * Q_BLOCK + kj[None, None, :] - Q_BLOCK
    valid = in_window[None] & (key_idx >= 0)
    bias = bias_table[rel_bucket(jnp.clip(steps, 0, taps) * r)]
    logits = logits + jnp.transpose(bias, (2, 0, 1)).astype(jnp.float32)
    logits = jnp.where(valid[None, None, :, None], logits, NEG)

    lse = jax.nn.logsumexp(logits, axis=-1)
    p = jnp.exp(logits - lse[..., None])
    out = jnp.einsum('brnhqk,brnkhd->brnqhd', p.astype(v.dtype), vb)
    out = out.reshape(b, r, Lp, h, d)[:, :, :L].swapaxes(1, 2).reshape(b, s, h, d)
    lse = jnp.transpose(lse, (0, 1, 2, 4, 3)).reshape(b, r, Lp, h)[:, :, :L]
    lse = lse.swapaxes(1, 2).reshape(b, s, h)
    return out, lse


def parallel_mixer(u, w_in, gate_bias, mla_q_norm, mla_w_uq, mla_kv_norm, mla_w_ukv,
                   w_branch_a, w_branch_b, w_branch_c, w_mix_out, rel_bias):
    b, s, _ = u.shape
    parts = jnp.split(u @ w_in, IN_OFFSETS, axis=-1)
    sb_q, sb_k, sb_v, c_q, c_kv, k_rope = parts[:6]
    dil = parts[6:6 + 3 * len(DIL_GROUPS)]
    g_a, g_b, g_c = parts[6 + 3 * len(DIL_GROUPS):]

    def heads(t, n):
        return t.reshape(b, s, n, -1)

    o_a = stick_breaking_attention(heads(sb_q, SB_HEADS), heads(sb_k, SB_HEADS),
                                   heads(sb_v, SB_HEADS)).reshape(b, s, SB_WIDTH)

    pos = jnp.arange(s)
    q = (rms_norm(c_q, mla_q_norm) @ mla_w_uq).reshape(b, s, MLA_HEADS, MLA_NOPE_DIM + MLA_ROPE_DIM)
    q = jnp.concatenate([q[..., :MLA_NOPE_DIM], rope(q[..., MLA_NOPE_DIM:], pos)], axis=-1)
    kv = (rms_norm(c_kv, mla_kv_norm) @ mla_w_ukv).reshape(b, s, MLA_HEADS, MLA_NOPE_DIM + MLA_V_DIM)
    k_pe = rope(k_rope[:, :, None, :], pos)
    k = jnp.concatenate([kv[..., :MLA_NOPE_DIM],
                         jnp.broadcast_to(k_pe, (b, s, MLA_HEADS, MLA_ROPE_DIM))], axis=-1)
    v = kv[..., MLA_NOPE_DIM:]
    o_b = causal_softmax_attention(q, k, v, (MLA_NOPE_DIM + MLA_ROPE_DIM) ** -0.5)
    o_b = o_b.reshape(b, s, MLA_WIDTH)

    outs, lses = [], []
    for g, (window, dilation) in enumerate(DIL_GROUPS):
        qg, kg, vg = (heads(t, DIL_HEADS) for t in dil[3 * g:3 * g + 3])
        o, lse = dilated_group_attention(qg, kg, vg, dilation, window // dilation,
                                         rel_bias[:, g * DIL_HEADS:(g + 1) * DIL_HEADS])
        outs.append(o)
        lses.append(lse)
    outs = jnp.stack(outs)
    alpha = jax.nn.softmax(jnp.stack(lses), axis=0)
    o_c = jnp.einsum('gbsh,gbshd->bshd', alpha.astype(outs.dtype), outs).reshape(b, s, DIL_WIDTH)

    merged = (jax.nn.sigmoid(g_a + gate_bias[0]) * (o_a @ w_branch_a)
              + jax.nn.sigmoid(g_b + gate_bias[1]) * (o_b @ w_branch_b)
              + jax.nn.sigmoid(g_c + gate_bias[2]) * (o_c @ w_branch_c))
    return merged @ w_mix_out


def memory_cross_attention(h, mem, w_q, w_kv, w_o):
    b, s, _ = h.shape
    m = mem.shape[1]
    q = (h @ w_q).reshape(b, s, X_HEADS, X_HEAD_DIM)
    kv = (mem @ w_kv).reshape(b, m, 2, X_HEADS, X_HEAD_DIM)
    k, v = kv[:, :, 0], kv[:, :, 1]
    logits = jnp.einsum('bshd,bmhd->bhsm', q, k).astype(jnp.float32) * X_HEAD_DIM ** -0.5
    p = jax.nn.softmax(logits, axis=-1)
    o = jnp.einsum('bhsm,bmhd->bshd', p.astype(v.dtype), v)
    return o.reshape(b, s, X_WIDTH) @ w_o


def setup_inputs(seed: int = 0) -> dict:
    key = jax.random.key(seed)
    ks = iter(jax.random.split(key, 32))

    def dense(shape, fan_in):
        return jax.random.normal(next(ks), shape, jnp.float32) * fan_in ** -0.5

    def gain(shape):
        return 1.0 + 0.02 * jax.random.normal(next(ks), shape, jnp.float32)

    Ld = DEPTH
    return {
        "x": jax.random.normal(next(ks), (BATCH, SEQ, D_MODEL), jnp.float32),
        "mem": jax.random.normal(next(ks), (BATCH, MEM_LEN, D_MODEL), jnp.float32),
        "ffn1_norm": gain((Ld, D_MODEL)),
        "ffn1_w_gate": dense((Ld, D_MODEL, D_FF), D_MODEL),
        "ffn1_w_up": dense((Ld, D_MODEL, D_FF), D_MODEL),
        "ffn1_w_down": dense((Ld, D_FF, D_MODEL), D_FF),
        "mix_norm": gain((Ld, D_MODEL)),
        "w_in": dense((Ld, D_MODEL, N_IN), D_MODEL),
        "gate_bias": 0.1 * jax.random.normal(next(ks), (Ld, N_BRANCH, D_MODEL), jnp.float32),
        "mla_q_norm": gain((Ld, MLA_Q_RANK)),
        "mla_w_uq": dense((Ld, MLA_Q_RANK, MLA_HEADS * (MLA_NOPE_DIM + MLA_ROPE_DIM)), MLA_Q_RANK),
        "mla_kv_norm": gain((Ld, MLA_KV_RANK)),
        "mla_w_ukv": dense((Ld, MLA_KV_RANK, MLA_HEADS * (MLA_NOPE_DIM + MLA_V_DIM)), MLA_KV_RANK),
        "w_branch_a": dense((Ld, SB_WIDTH, D_MODEL), SB_WIDTH),
        "w_branch_b": dense((Ld, MLA_WIDTH, D_MODEL), MLA_WIDTH),
        "w_branch_c": dense((Ld, DIL_WIDTH, D_MODEL), DIL_WIDTH),
        "w_mix_out": dense((Ld, D_MODEL, D_MODEL), D_MODEL),
        "rel_bias": 0.5 * jax.random.normal(next(ks), (REL_BUCKETS, DIL_HEADS * len(DIL_GROUPS)), jnp.float32),
        "xattn_norm": gain((Ld, D_MODEL)),
        "mem_norm": gain((Ld, D_MODEL)),
        "xattn_w_q": dense((Ld, D_MODEL, X_WIDTH), D_MODEL),
        "xattn_w_kv": dense((Ld, D_MODEL, 2 * X_WIDTH), D_MODEL),
        "xattn_w_o": dense((Ld, X_WIDTH, D_MODEL), X_WIDTH),
        "ffn2_norm": gain((Ld, D_MODEL)),
        "ffn2_w_gate": dense((Ld, D_MODEL, D_FF), D_MODEL),
        "ffn2_w_up": dense((Ld, D_MODEL, D_FF), D_MODEL),
        "ffn2_w_down": dense((Ld, D_FF, D_MODEL), D_FF),
        "final_norm": gain((D_MODEL,)),
    }


def reference(x, mem, ffn1_norm, ffn1_w_gate, ffn1_w_up, ffn1_w_down, mix_norm, w_in, gate_bias,
              mla_q_norm, mla_w_uq, mla_kv_norm, mla_w_ukv, w_branch_a, w_branch_b, w_branch_c,
              w_mix_out, rel_bias, xattn_norm, mem_norm, xattn_w_q, xattn_w_kv, xattn_w_o,
              ffn2_norm, ffn2_w_gate, ffn2_w_up, ffn2_w_down, final_norm):
    h = x
    for l in range(DEPTH):
        h = h + FFN_RESIDUAL * swiglu(rms_norm(h, ffn1_norm[l]),
                                      ffn1_w_gate[l], ffn1_w_up[l], ffn1_w_down[l])
        h = h + parallel_mixer(rms_norm(h, mix_norm[l]), w_in[l], gate_bias[l],
                               mla_q_norm[l], mla_w_uq[l], mla_kv_norm[l], mla_w_ukv[l],
                               w_branch_a[l], w_branch_b[l], w_branch_c[l], w_mix_out[l], rel_bias)
        h = h + memory_cross_attention(rms_norm(h, xattn_norm[l]), rms_norm(mem, mem_norm[l]),
                                       xattn_w_q[l], xattn_w_kv[l], xattn_w_o[l])
        h = h + FFN_RESIDUAL * swiglu(rms_norm(h, ffn2_norm[l]),
                                      ffn2_w_gate[l], ffn2_w_up[l], ffn2_w_down[l])
    return rms_norm(h, final_norm)
```

```python
import functools
import math

import jax
import jax.numpy as jnp
import numpy as np
from jax import lax
from jax.experimental import pallas as pl
from jax.experimental.pallas import tpu as pltpu

BF = jnp.bfloat16
F32 = jnp.float32

EPS = 1e-6
NEG = -1e30
FFN_RESIDUAL = 0.5

SB_WIDTH = 512
MLA_HEADS = 8
MLA_Q_RANK = 384
MLA_KV_RANK = 128
MLA_NOPE = 64
MLA_ROPE = 32
MLA_V = 64
ROPE_THETA = 10000.0
DIL_GROUPS = ((128, 1), (512, 4), (2048, 16))
DIL_WIDTH = 512
REL_BUCKETS = 32
REL_MAX_DIST = 2048
X_HEADS = 4
X_HEAD_DIM = 128
HEAD_DIM = 64
BAND = 128

V7X_VMEM_LIMIT = 48 * 1024 * 1024
FF_CHUNK = 256


def _cparams(sem):
    return pltpu.CompilerParams(dimension_semantics=sem, vmem_limit_bytes=V7X_VMEM_LIMIT)


def _rms(x, g):
    ms = jnp.mean(x * x, axis=-1, keepdims=True)
    return x * lax.rsqrt(ms + EPS) * g


def _dot(a, b):
    return jnp.dot(a, b, preferred_element_type=F32)


def _dot_t(a, b):
    return lax.dot_general(a, b, (((1,), (1,)), ((), ())), preferred_element_type=F32)


def _resident(shape):
    nd = len(shape)
    return pl.BlockSpec(shape, lambda *_: (0,) * nd, pipeline_mode=pl.Buffered(1))


def _ffn_body(h_ref, g_ref, wg_ref, wu_ref, wd_ref, fg_ref, o_ref, act_ref, *, final):
    x = h_ref[...]
    u = _rms(x, g_ref[...]).astype(BF)
    d_ff = wg_ref.shape[1]
    for c in range(d_ff // FF_CHUNK):
        sl = slice(c * FF_CHUNK, (c + 1) * FF_CHUNK)
        a = _dot(u, wg_ref[:, sl])
        b = _dot(u, wu_ref[:, sl])
        act_ref[:, sl] = (a * jax.nn.sigmoid(a) * b).astype(BF)
    y = x + FFN_RESIDUAL * _dot(act_ref[...], wd_ref[...])
    if final:
        y = _rms(y, fg_ref[...])
    o_ref[...] = y


def _ffn(h, g, wg, wu, wd, fg, final, tm=512):
    t, d = h.shape
    d_ff = wg.shape[1]
    tm = min(tm, t)
    return pl.pallas_call(
        functools.partial(_ffn_body, final=final),
        grid=(t // tm,),
        in_specs=[
            pl.BlockSpec((tm, d), lambda i: (i, 0)),
            _resident((1, d)),
            _resident((d, d_ff)),
            _resident((d, d_ff)),
            _resident((d_ff, d)),
            _resident((1, d)),
        ],
        out_specs=pl.BlockSpec((tm, d), lambda i: (i, 0)),
        out_shape=jax.ShapeDtypeStruct((t, d), F32),
        scratch_shapes=[pltpu.VMEM((tm, d_ff), BF)],
        compiler_params=_cparams(("parallel",)),
        name="ffn",
    )(h, g, wg, wu, wd, fg)


def _proj_body(h_ref, g_ref, w_ref, o_ref, u_ref):
    @pl.when(pl.program_id(1) == 0)
    def _():
        u_ref[...] = _rms(h_ref[...], g_ref[...]).astype(BF)

    o_ref[...] = _dot(u_ref[...], w_ref[...]).astype(o_ref.dtype)


def _norm_proj(h, g, w, tn, tm=512):
    t, d = h.shape
    n = w.shape[1]
    tm = min(tm, t)
    return pl.pallas_call(
        _proj_body,
        grid=(t // tm, n // tn),
        in_specs=[
            pl.BlockSpec((tm, d), lambda i, j: (i, 0)),
            pl.BlockSpec((1, d), lambda i, j: (0, 0)),
            pl.BlockSpec((d, tn), lambda i, j: (0, j)),
        ],
        out_specs=pl.BlockSpec((tm, tn), lambda i, j: (i, j)),
        out_shape=jax.ShapeDtypeStruct((t, n), BF),
        scratch_shapes=[pltpu.VMEM((tm, d), BF)],
        compiler_params=_cparams(("parallel", "arbitrary")),
        name="norm_proj",
    )(h, g, w)


def _sb_body(q_ref, k_ref, v_ref, o_ref, *, tq):
    i = pl.program_id(2)
    q = q_ref[0] * jnp.asarray(HEAD_DIM ** -0.5, BF)
    lane = lax.broadcasted_iota(jnp.int32, (1, 2 * HEAD_DIM), 1)
    first = lane < HEAD_DIM
    zero = jnp.zeros_like(q)
    q_heads = (jnp.where(first, q, zero), jnp.where(first, zero, q))
    row = lax.broadcasted_iota(jnp.int32, (tq, tq), 0)
    col = lax.broadcasted_iota(jnp.int32, (tq, tq), 1)
    strict = col < row
    later = jnp.where(row > col, 1.0, 0.0).astype(BF)

    def block(j, carry, diag):
        acc, r0, r1 = carry
        start = pl.multiple_of(j * tq, tq)
        kj = k_ref[0, pl.ds(start, tq), :]
        vj = v_ref[0, pl.ds(start, tq), :]
        outs, new_r = [], []
        for qh, r in zip(q_heads, (r0, r1)):
            z = _dot_t(qh, kj)
            sp = jnp.log(1.0 + jnp.exp(-jnp.abs(z)))
            log_beta = jnp.minimum(z, 0.0) - sp
            log_keep = log_beta - z
            if diag:
                log_keep = jnp.where(strict, log_keep, 0.0)
            hi = log_keep.astype(BF)
            lo = (log_keep - hi.astype(F32)).astype(BF)
            after = _dot(hi, later) + _dot(lo, later)
            w = jnp.exp(log_beta + after + r)
            if diag:
                w = jnp.where(strict, w, 0.0)
            outs.append(_dot(w.astype(BF), vj))
            new_r.append(r + jnp.sum(log_keep, axis=-1, keepdims=True))
        return acc + jnp.where(first, outs[0], outs[1]), new_r[0], new_r[1]

    init = (jnp.zeros((tq, 2 * HEAD_DIM), F32), jnp.zeros((tq, 1), F32), jnp.zeros((tq, 1), F32))
    carry = block(i, init, True)
    carry = lax.fori_loop(0, i, lambda s, c: block(i - 1 - s, c, False), carry)
    o_ref[0] = carry[0].astype(o_ref.dtype)


def _sb_attention(a3, tq=256):
    b, s, _ = a3.shape
    tq = min(tq, s)
    pairs = SB_WIDTH // (2 * HEAD_DIM)
    w = 2 * HEAD_DIM
    return pl.pallas_call(
        functools.partial(_sb_body, tq=tq),
        grid=(b, pairs, s // tq),
        in_specs=[
            pl.BlockSpec((1, tq, w), lambda bi, p, i: (bi, i, p)),
            pl.BlockSpec((1, s, w), lambda bi, p, i: (bi, 0, pairs + p)),
            pl.BlockSpec((1, s, w), lambda bi, p, i: (bi, 0, 2 * pairs + p)),
        ],
        out_specs=pl.BlockSpec((1, tq, w), lambda bi, p, i: (bi, i, p)),
        out_shape=jax.ShapeDtypeStruct((b, s, SB_WIDTH), BF),
        compiler_params=_cparams(("parallel", "parallel", "arbitrary")),
        name="sb_attn",
    )(a3, a3, a3)


def _mla_prep_body(cq_ref, ckv_ref, kr_ref, cos_ref, sin_ref, qg_ref, kvg_ref, wq_ref, wkv_ref,
                   qp_ref, kp_ref, v_ref, *, scale):
    nq = _rms(cq_ref[0].astype(F32), qg_ref[...]).astype(BF)
    q = _dot(nq, wq_ref[...])
    nkv = _rms(ckv_ref[0].astype(F32), kvg_ref[...]).astype(BF)
    kv = _dot(nkv, wkv_ref[...])
    cos = cos_ref[...]
    sin = sin_ref[...]
    kr = kr_ref[0].astype(F32)
    k_pe = (kr[:, :128] * cos + kr[:, 128:] * sin).astype(BF)
    width = q.shape[1] // 3
    for p in range(width // 128):
        lo, hi = p * 128, (p + 1) * 128
        qp_ref[0, :, 2 * lo:2 * lo + 128] = (q[:, lo:hi] * scale).astype(BF)
        q_pe = q[:, width + lo:width + hi] * cos + q[:, 2 * width + lo:2 * width + hi] * sin
        qp_ref[0, :, 2 * lo + 128:2 * hi] = (q_pe * scale).astype(BF)
        kp_ref[0, :, 2 * lo:2 * lo + 128] = kv[:, lo:hi].astype(BF)
        kp_ref[0, :, 2 * lo + 128:2 * hi] = k_pe
    v_ref[0] = kv[:, width:].astype(BF)


def _mla_prep(a3, cos, sin, qg, kvg, wq, wkv, tm=512):
    b, s, _ = a3.shape
    tm = min(tm, s)
    width = MLA_HEADS * MLA_NOPE
    scale = (MLA_NOPE + MLA_ROPE) ** -0.5
    out_sds = lambda n: jax.ShapeDtypeStruct((b, s, n), BF)
    return pl.pallas_call(
        functools.partial(_mla_prep_body, scale=scale),
        grid=(b, s // tm),
        in_specs=[
            pl.BlockSpec((1, tm, MLA_Q_RANK), lambda bi, i: (bi, i, 1536 // MLA_Q_RANK)),
            pl.BlockSpec((1, tm, MLA_KV_RANK), lambda bi, i: (bi, i, 1920 // MLA_KV_RANK)),
            pl.BlockSpec((1, tm, 256), lambda bi, i: (bi, i, 2048 // 256)),
            pl.BlockSpec((tm, 128), lambda bi, i: (i, 0)),
            pl.BlockSpec((tm, 128), lambda bi, i: (i, 0)),
            pl.BlockSpec((1, MLA_Q_RANK), lambda bi, i: (0, 0)),
            pl.BlockSpec((1, MLA_KV_RANK), lambda bi, i: (0, 0)),
            pl.BlockSpec(wq.shape, lambda bi, i: (0, 0)),
            pl.BlockSpec(wkv.shape, lambda bi, i: (0, 0)),
        ],
        out_specs=[
            pl.BlockSpec((1, tm, 2 * width), lambda bi, i: (bi, i, 0)),
            pl.BlockSpec((1, tm, 2 * width), lambda bi, i: (bi, i, 0)),
            pl.BlockSpec((1, tm, width), lambda bi, i: (bi, i, 0)),
        ],
        out_shape=[out_sds(2 * width), out_sds(2 * width), out_sds(width)],
        compiler_params=_cparams(("parallel", "parallel")),
        name="mla_prep",
    )(a3, a3, a3, cos, sin, qg, kvg, wq, wkv)


def _mla_body(q_ref, k_ref, v_ref, o_ref, *, tq):
    i = pl.program_id(2)
    q = q_ref[0]
    lane = lax.broadcasted_iota(jnp.int32, (1, 256), 1)
    sel0 = (lane < 64) | ((lane >= 128) & (lane < 160))
    sel1 = ((lane >= 64) & (lane < 128)) | ((lane >= 160) & (lane < 192))
    zero = jnp.zeros_like(q)
    q_heads = (jnp.where(sel0, q, zero), jnp.where(sel1, q, zero))
    row = lax.broadcasted_iota(jnp.int32, (tq, tq), 0)
    col = lax.broadcasted_iota(jnp.int32, (tq, tq), 1)
    causal = col <= row

    def block(j, carry, diag):
        start = pl.multiple_of(j * tq, tq)
        kj = k_ref[0, pl.ds(start, tq), :]
        vj = v_ref[0, pl.ds(start, tq), :]
        new = []
        for qh, (m, l, acc) in zip(q_heads, carry):
            s = _dot_t(qh, kj)
            if diag:
                s = jnp.where(causal, s, NEG)
            m_new = jnp.maximum(m, jnp.max(s, axis=-1, keepdims=True))
            alpha = jnp.exp(m - m_new)
            p = jnp.exp(s - m_new)
            l = alpha * l + jnp.sum(p, axis=-1, keepdims=True)
            acc = alpha * acc + _dot(p.astype(BF), vj)
            new.append((m_new, l, acc))
        return tuple(new)

    one = (jnp.full((tq, 1), NEG, F32), jnp.zeros((tq, 1), F32), jnp.zeros((tq, 128), F32))
    carry = lax.fori_loop(0, i, lambda j, c: block(j, c, False), (one, one))
    (_, l0, a0), (_, l1, a1) = block(i, carry, True)
    lane_o = lax.broadcasted_iota(jnp.int32, (1, 128), 1)
    o_ref[0] = jnp.where(lane_o < 64, a0 / l0, a1 / l1).astype(o_ref.dtype)


def _mla_attention(qp, kp, v, tq=256):
    b, s, _ = qp.shape
    tq = min(tq, s)
    pairs = MLA_HEADS // 2
    return pl.pallas_call(
        functools.partial(_mla_body, tq=tq),
        grid=(b, pairs, s // tq),
        in_specs=[
            pl.BlockSpec((1, tq, 256), lambda bi, p, i: (bi, i, p)),
            pl.BlockSpec((1, s, 256), lambda bi, p, i: (bi, 0, p)),
            pl.BlockSpec((1, s, 128), lambda bi, p, i: (bi, 0, p)),
        ],
        out_specs=pl.BlockSpec((1, tq, 128), lambda bi, p, i: (bi, i, p)),
        out_shape=jax.ShapeDtypeStruct((b, s, MLA_HEADS * MLA_V), BF),
        compiler_params=_cparams(("parallel", "parallel", "arbitrary")),
        name="mla_attn",
    )(qp, kp, v)


def _dil_body(q_ref, kc_ref, kp_ref, vc_ref, vp_ref, bias_ref, o_ref, lse_ref, *, tl):
    j = pl.program_id(2)
    lane = lax.broadcasted_iota(jnp.int32, (1, 2 * HEAD_DIM), 1)
    first = lane < HEAD_DIM
    qi = lax.broadcasted_iota(jnp.int32, (BAND, 2 * BAND), 0)
    kj = lax.broadcasted_iota(jnp.int32, (BAND, 2 * BAND), 1)
    steps = qi + BAND - kj
    in_window = (steps >= 0) & (steps <= BAND)
    in_window_first = in_window & ((kj >= BAND) | (j > 0))
    qscale = jnp.asarray(HEAD_DIM ** -0.5, BF)
    for sb in range(tl // BAND):
        rows = slice(sb * BAND, (sb + 1) * BAND)
        prev = slice((sb - 1) * BAND, sb * BAND)
        k_prev = kp_ref[0] if sb == 0 else kc_ref[0, prev, :]
        v_prev = vp_ref[0] if sb == 0 else vc_ref[0, prev, :]
        kcat = jnp.concatenate([k_prev, kc_ref[0, rows, :]], axis=0)
        vcat = jnp.concatenate([v_prev, vc_ref[0, rows, :]], axis=0)
        valid = in_window_first if sb == 0 else in_window
        qs = q_ref[0, rows, :] * qscale
        for hp in range(DIL_WIDTH // (2 * HEAD_DIM)):
            cols = slice(hp * 2 * HEAD_DIM, (hp + 1) * 2 * HEAD_DIM)
            qb = qs[:, cols]
            kb = kcat[:, cols]
            vb = vcat[:, cols]
            zero = jnp.zeros_like(qb)
            outs, lses = [], []
            for hh in range(2):
                qh = jnp.where(first, qb, zero) if hh == 0 else jnp.where(first, zero, qb)
                s = _dot_t(qh, kb) + bias_ref[2 * hp + hh]
                s = jnp.where(valid, s, NEG)
                m = jnp.max(s, axis=-1, keepdims=True)
                p = jnp.exp(s - m)
                l = jnp.sum(p, axis=-1, keepdims=True)
                outs.append(_dot(p.astype(BF), vb) / l)
                lses.append(m + jnp.log(l))
            o_ref[0, rows, cols] = jnp.where(first, outs[0], outs[1]).astype(o_ref.dtype)
            lse_ref[0, rows, cols] = jnp.where(first, lses[0], lses[1])


def _dil_attention(dl, bias, group, dilation, batch, seq, tl=512):
    r = dilation
    length = seq // r
    tl = min(tl, length)
    ncol = dl.shape[1] // DIL_WIDTH
    view = dl.reshape(batch, length, r * dl.shape[1])
    sub = tl // BAND

    def cur(c):
        return pl.BlockSpec((1, tl, DIL_WIDTH), lambda bi, res, j: (bi, j, res * ncol + 3 * group + c))

    def prev(c):
        return pl.BlockSpec((1, BAND, DIL_WIDTH),
                            lambda bi, res, j: (bi, jnp.maximum(j * sub - 1, 0), res * ncol + 3 * group + c))

    out_spec = pl.BlockSpec((1, tl, DIL_WIDTH), lambda bi, res, j: (bi, j, res))
    out, lse = pl.pallas_call(
        functools.partial(_dil_body, tl=tl),
        grid=(batch, r, length // tl),
        in_specs=[cur(0), cur(1), prev(1), cur(2), prev(2),
                  pl.BlockSpec(bias.shape, lambda bi, res, j: (0, 0, 0))],
        out_specs=[out_spec, out_spec],
        out_shape=[jax.ShapeDtypeStruct((batch, length, r * DIL_WIDTH), BF),
                   jax.ShapeDtypeStruct((batch, length, r * DIL_WIDTH), F32)],
        compiler_params=_cparams(("parallel", "parallel", "arbitrary")),
        name=f"dil_attn_r{r}",
    )(view, view, view, view, view, bias)
    t = batch * seq
    return out.reshape(t, DIL_WIDTH), lse.reshape(t, DIL_WIDTH)


def _merge_body(h_ref, oa_ref, ob_ref, oc0_ref, oc1_ref, oc2_ref, l0_ref, l1_ref, l2_ref,
                ga_ref, gb_ref, gc_ref, gbias_ref, wa_ref, wb_ref, wc_ref, wo_ref, o_ref):
    l0, l1, l2 = l0_ref[...], l1_ref[...], l2_ref[...]
    mx = jnp.maximum(jnp.maximum(l0, l1), l2)
    e0, e1, e2 = jnp.exp(l0 - mx), jnp.exp(l1 - mx), jnp.exp(l2 - mx)
    oc = (e0 * oc0_ref[...].astype(F32) + e1 * oc1_ref[...].astype(F32)
          + e2 * oc2_ref[...].astype(F32)) / (e0 + e1 + e2)
    gbias = gbias_ref[...]
    merged = (jax.nn.sigmoid(ga_ref[...].astype(F32) + gbias[0:1]) * _dot(oa_ref[...], wa_ref[...])
              + jax.nn.sigmoid(gb_ref[...].astype(F32) + gbias[1:2]) * _dot(ob_ref[...], wb_ref[...])
              + jax.nn.sigmoid(gc_ref[...].astype(F32) + gbias[2:3]) * _dot(oc.astype(BF), wc_ref[...]))
    o_ref[...] = h_ref[...] + _dot(merged.astype(BF), wo_ref[...])


def _merge(h, oa, ob, ocs, lses, gates, gbias, wa, wb, wc, wo, tm=512):
    t, d = h.shape
    tm = min(tm, t)
    row = lambda n: pl.BlockSpec((tm, n), lambda i: (i, 0))
    gate = lambda c: pl.BlockSpec((tm, d), lambda i: (i, c))
    return pl.pallas_call(
        _merge_body,
        grid=(t // tm,),
        in_specs=[row(d), row(SB_WIDTH), row(SB_WIDTH)] + [row(DIL_WIDTH)] * 6
        + [gate(0), gate(1), gate(2), _resident(gbias.shape),
           _resident(wa.shape), _resident(wb.shape), _resident(wc.shape), _resident(wo.shape)],
        out_specs=row(d),
        out_shape=jax.ShapeDtypeStruct((t, d), F32),
        compiler_params=_cparams(("parallel",)),
        name="merge",
    )(h, oa, ob, *ocs, *lses, gates, gates, gates, gbias, wa, wb, wc, wo)


def _memkv_body(m_ref, g_ref, w_ref, o_ref):
    u = _rms(m_ref[0], g_ref[...]).astype(BF)
    o_ref[0] = _dot(u, w_ref[...]).astype(o_ref.dtype)


def _mem_kv(mem, g, w):
    b, m, d = mem.shape
    n = w.shape[1]
    return pl.pallas_call(
        _memkv_body,
        grid=(b,),
        in_specs=[pl.BlockSpec((1, m, d), lambda i: (i, 0, 0)),
                  pl.BlockSpec((1, d), lambda i: (0, 0)),
                  pl.BlockSpec((d, n), lambda i: (0, 0))],
        out_specs=pl.BlockSpec((1, m, n), lambda i: (i, 0, 0)),
        out_shape=jax.ShapeDtypeStruct((b, m, n), BF),
        compiler_params=_cparams(("parallel",)),
        name="mem_kv",
    )(mem, g, w)


def _xattn_body(h_ref, g_ref, kv_ref, wq_ref, wo_ref, o_ref):
    x = h_ref[0]
    u = _rms(x, g_ref[...]).astype(BF)
    q = _dot(u, wq_ref[...]).astype(BF)
    kv = kv_ref[0]
    width = X_HEADS * X_HEAD_DIM
    scale = X_HEAD_DIM ** -0.5
    outs = []
    for h in range(X_HEADS):
        cols = slice(h * X_HEAD_DIM, (h + 1) * X_HEAD_DIM)
        s = _dot_t(q[:, cols], kv[:, cols]) * scale
        m = jnp.max(s, axis=-1, keepdims=True)
        e = jnp.exp(s - m)
        p = e / jnp.sum(e, axis=-1, keepdims=True)
        outs.append(_dot(p.astype(BF), kv[:, width + h * X_HEAD_DIM:width + (h + 1) * X_HEAD_DIM]))
    o = jnp.concatenate(outs, axis=-1).astype(BF)
    o_ref[0] = x + _dot(o, wo_ref[...])


def _xattn(h3, g, kv, wq, wo, tm=512):
    b, s, d = h3.shape
    tm = min(tm, s)
    m = kv.shape[1]
    return pl.pallas_call(
        _xattn_body,
        grid=(b, s // tm),
        in_specs=[pl.BlockSpec((1, tm, d), lambda bi, i: (bi, i, 0)),
                  pl.BlockSpec((1, d), lambda bi, i: (0, 0)),
                  pl.BlockSpec((1, m, kv.shape[2]), lambda bi, i: (bi, 0, 0)),
                  pl.BlockSpec(wq.shape, lambda bi, i: (0, 0)),
                  pl.BlockSpec(wo.shape, lambda bi, i: (0, 0))],
        out_specs=pl.BlockSpec((1, tm, d), lambda bi, i: (bi, i, 0)),
        out_shape=jax.ShapeDtypeStruct((b, s, d), F32),
        compiler_params=_cparams(("parallel", "parallel")),
        name="xattn",
    )(h3, g, kv, wq, wo)


def _rel_bucket(dist):
    exact = REL_BUCKETS // 2
    d = jnp.maximum(dist, exact).astype(F32)
    large = exact + (jnp.log(d / exact) / math.log(REL_MAX_DIST / exact)
                     * (REL_BUCKETS - exact)).astype(jnp.int32)
    return jnp.where(dist < exact, dist, jnp.minimum(large, REL_BUCKETS - 1))


def _band_bias(rel_bias, group, dilation):
    heads = DIL_WIDTH // HEAD_DIM
    qi = jnp.arange(BAND)
    kj = jnp.arange(2 * BAND)
    steps = (qi[:, None] + BAND) - kj[None, :]
    table = rel_bias[:, group * heads:(group + 1) * heads]
    bias = table[_rel_bucket(jnp.clip(steps, 0, BAND) * dilation)]
    return jnp.transpose(bias, (2, 0, 1)).astype(F32)


def _swap_halves(w, width):
    k, n = w.shape
    w = w.reshape(k, n // width, 2, width // 2)
    return w[:, :, ::-1, :].reshape(k, n)


def _pack_w_in(w_in):
    d = w_in.shape[0]
    sb = w_in[:, :1536]
    cq = w_in[:, 1536:1920]
    ckv = w_in[:, 1920:2048]
    kr = w_in[:, 2048:2080]
    zeros = jnp.zeros((d, 64), w_in.dtype)
    kr_sw = _swap_halves(kr, MLA_ROPE)
    seg_a = jnp.concatenate([sb, cq, ckv, kr, kr, zeros, kr_sw, kr_sw, zeros], axis=1)
    seg_d = w_in[:, 2080:2080 + 9 * DIL_WIDTH]
    seg_g = w_in[:, 2080 + 9 * DIL_WIDTH:]
    return seg_a.astype(BF), seg_d.astype(BF), seg_g.astype(BF)


def _pack_w_uq(w_uq):
    k = w_uq.shape[0]
    w = w_uq.reshape(k, MLA_HEADS, MLA_NOPE + MLA_ROPE)
    nope = w[:, :, :MLA_NOPE].reshape(k, MLA_HEADS * MLA_NOPE)
    rope = w[:, :, MLA_NOPE:]

    def pair_layout(rp):
        rp = rp.reshape(k, MLA_HEADS // 2, 2 * MLA_ROPE)
        pad = jnp.zeros((k, MLA_HEADS // 2, 128 - 2 * MLA_ROPE), rp.dtype)
        return jnp.concatenate([rp, pad], axis=-1).reshape(k, (MLA_HEADS // 2) * 128)

    rope_sw = rope.reshape(k, MLA_HEADS, 2, MLA_ROPE // 2)[:, :, ::-1, :].reshape(k, MLA_HEADS, MLA_ROPE)
    return jnp.concatenate([nope, pair_layout(rope), pair_layout(rope_sw)], axis=1).astype(BF)


def _pack_w_ukv(w_ukv):
    k = w_ukv.shape[0]
    w = w_ukv.reshape(k, MLA_HEADS, MLA_NOPE + MLA_V)
    return jnp.concatenate([w[:, :, :MLA_NOPE].reshape(k, -1), w[:, :, MLA_NOPE:].reshape(k, -1)],
                           axis=1).astype(BF)


def _rope_tables(seq):
    half = MLA_ROPE // 2
    freqs = ROPE_THETA ** (-jnp.arange(half, dtype=F32) / half)
    ang = jnp.arange(seq).astype(F32)[:, None] * freqs[None, :]
    cos, sin = jnp.cos(ang), jnp.sin(ang)
    pad = jnp.zeros((seq, 128 - 2 * MLA_ROPE), F32)
    cos_t = jnp.concatenate([cos, cos, cos, cos, pad], axis=1)
    sin_t = jnp.concatenate([-sin, sin, -sin, sin, pad], axis=1)
    return cos_t, sin_t


def kernel(x, mem, ffn1_norm, ffn1_w_gate, ffn1_w_up, ffn1_w_down, mix_norm, w_in, gate_bias, mla_q_norm, mla_w_uq, mla_kv_norm, mla_w_ukv, w_branch_a, w_branch_b, w_branch_c, w_mix_out, rel_bias, xattn_norm, mem_norm, xattn_w_q, xattn_w_kv, xattn_w_o, ffn2_norm, ffn2_w_gate, ffn2_w_up, ffn2_w_down, final_norm):
    b, s, d = x.shape
    t = b * s
    depth = w_in.shape[0]
    cos_t, sin_t = _rope_tables(s)
    biases = [_band_bias(rel_bias, g, dil) for g, (_, dil) in enumerate(DIL_GROUPS)]
    row = lambda v: v.reshape(1, -1)
    bf = lambda w: w.astype(BF)
    fg = row(final_norm)

    h = x.reshape(t, d)
    for l in range(depth):
        h = _ffn(h, row(ffn1_norm[l]), bf(ffn1_w_gate[l]), bf(ffn1_w_up[l]), bf(ffn1_w_down[l]), fg, False)

        w_a, w_d, w_g = _pack_w_in(w_in[l])
        g_mix = row(mix_norm[l])
        seg_a = _norm_proj(h, g_mix, w_a, tn=w_a.shape[1])
        seg_d = _norm_proj(h, g_mix, w_d, tn=w_d.shape[1] // 2)
        seg_g = _norm_proj(h, g_mix, w_g, tn=w_g.shape[1] // 2)

        a3 = seg_a.reshape(b, s, seg_a.shape[1])
        o_a = _sb_attention(a3).reshape(t, SB_WIDTH)
        qp, kp, v = _mla_prep(a3, cos_t, sin_t, row(mla_q_norm[l]), row(mla_kv_norm[l]),
                              _pack_w_uq(mla_w_uq[l]), _pack_w_ukv(mla_w_ukv[l]))
        o_b = _mla_attention(qp, kp, v).reshape(t, MLA_HEADS * MLA_V)
        ocs, lses = [], []
        for g, (_, dil) in enumerate(DIL_GROUPS):
            o, lse = _dil_attention(seg_d, biases[g], g, dil, b, s)
            ocs.append(o)
            lses.append(lse)
        h = _merge(h, o_a, o_b, ocs, lses, seg_g, gate_bias[l], bf(w_branch_a[l]), bf(w_branch_b[l]),
                   bf(w_branch_c[l]), bf(w_mix_out[l]))

        kv = _mem_kv(mem, row(mem_norm[l]), bf(xattn_w_kv[l]))
        h = _xattn(h.reshape(b, s, d), row(xattn_norm[l]), kv, bf(xattn_w_q[l]), bf(xattn_w_o[l])).reshape(t, d)

        h = _ffn(h, row(ffn2_norm[l]), bf(ffn2_w_gate[l]), bf(ffn2_w_up[l]), bf(ffn2_w_down[l]), fg,
                 l == depth - 1)
    return h.reshape(b, s, d)
```

```python
import functools
import math

import jax
import jax.numpy as jnp
import numpy as np
from jax import lax
from jax.experimental import pallas as pl
from jax.experimental.pallas import tpu as pltpu

BF = jnp.bfloat16
F32 = jnp.float32

EPS = 1e-6
NEG = -1e30
FFN_RESIDUAL = 0.5

SB_WIDTH = 512
MLA_HEADS = 8
MLA_Q_RANK = 384
MLA_KV_RANK = 128
MLA_NOPE = 64
MLA_ROPE = 32
MLA_V = 64
ROPE_THETA = 10000.0
DIL_GROUPS = ((128, 1), (512, 4), (2048, 16))
DIL_WIDTH = 512
REL_BUCKETS = 32
REL_MAX_DIST = 2048
X_HEADS = 4
X_HEAD_DIM = 128
HEAD_DIM = 64
BAND = 128
SB_DEAD_LOG = -100.0
LOG2E = math.log2(math.e)

V7X_VMEM_LIMIT = 48 * 1024 * 1024
FF_CHUNK = 256


def _cparams(sem):
    return pltpu.CompilerParams(dimension_semantics=sem, vmem_limit_bytes=V7X_VMEM_LIMIT)


def _rms(x, g):
    ms = jnp.mean(x * x, axis=-1, keepdims=True)
    return x * lax.rsqrt(ms + EPS) * g


def _dot(a, b):
    return jnp.dot(a, b, preferred_element_type=F32)


def _dot_t(a, b):
    return lax.dot_general(a, b, (((1,), (1,)), ((), ())), preferred_element_type=F32)


def _resident(shape):
    nd = len(shape)
    return pl.BlockSpec(shape, lambda *_: (0,) * nd, pipeline_mode=pl.Buffered(1))


def _ffn_body(h_ref, g_ref, wg_ref, wu_ref, wd_ref, fg_ref, o_ref, act_ref, *, final):
    x = h_ref[...]
    u = _rms(x, g_ref[...]).astype(BF)
    d_ff = wg_ref.shape[1]
    for c in range(d_ff // FF_CHUNK):
        sl = slice(c * FF_CHUNK, (c + 1) * FF_CHUNK)
        a = _dot(u, wg_ref[:, sl])
        b = _dot(u, wu_ref[:, sl])
        act_ref[:, sl] = (a * jax.nn.sigmoid(a) * b).astype(BF)
    y = x + FFN_RESIDUAL * _dot(act_ref[...], wd_ref[...])
    if final:
        y = _rms(y, fg_ref[...])
    o_ref[...] = y


def _ffn(h, g, wg, wu, wd, fg, final, tm=512):
    t, d = h.shape
    d_ff = wg.shape[1]
    tm = min(tm, t)
    return pl.pallas_call(
        functools.partial(_ffn_body, final=final),
        grid=(t // tm,),
        in_specs=[
            pl.BlockSpec((tm, d), lambda i: (i, 0)),
            _resident((1, d)),
            _resident((d, d_ff)),
            _resident((d, d_ff)),
            _resident((d_ff, d)),
            _resident((1, d)),
        ],
        out_specs=pl.BlockSpec((tm, d), lambda i: (i, 0)),
        out_shape=jax.ShapeDtypeStruct((t, d), F32),
        scratch_shapes=[pltpu.VMEM((tm, d_ff), BF)],
        compiler_params=_cparams(("parallel",)),
        name="ffn",
    )(h, g, wg, wu, wd, fg)


def _proj_body(h_ref, g_ref, w_ref, o_ref, u_ref):
    @pl.when(pl.program_id(1) == 0)
    def _():
        u_ref[...] = _rms(h_ref[...], g_ref[...]).astype(BF)

    o_ref[...] = _dot(u_ref[...], w_ref[...]).astype(o_ref.dtype)


def _norm_proj(h, g, w, tn, tm=512):
    t, d = h.shape
    n = w.shape[1]
    tm = min(tm, t)
    return pl.pallas_call(
        _proj_body,
        grid=(t // tm, n // tn),
        in_specs=[
            pl.BlockSpec((tm, d), lambda i, j: (i, 0)),
            pl.BlockSpec((1, d), lambda i, j: (0, 0)),
            pl.BlockSpec((d, tn), lambda i, j: (0, j)),
        ],
        out_specs=pl.BlockSpec((tm, tn), lambda i, j: (i, j)),
        out_shape=jax.ShapeDtypeStruct((t, n), BF),
        scratch_shapes=[pltpu.VMEM((tm, d), BF)],
        compiler_params=_cparams(("parallel", "arbitrary")),
        name="norm_proj",
    )(h, g, w)


def _sb_body(q_ref, k_ref, v_ref, o_ref, *, tq):
    i = pl.program_id(2)
    q = q_ref[0] * jnp.asarray(HEAD_DIM ** -0.5, BF)
    lane = lax.broadcasted_iota(jnp.int32, (1, 2 * HEAD_DIM), 1)
    first = lane < HEAD_DIM
    zero = jnp.zeros_like(q)
    q_heads = (jnp.where(first, q, zero), jnp.where(first, zero, q))
    row = lax.broadcasted_iota(jnp.int32, (tq, tq), 0)
    col = lax.broadcasted_iota(jnp.int32, (tq, tq), 1)
    strict = col < row
    later = jnp.where(row > col, 1.0, 0.0).astype(BF)

    def block(j, carry, diag):
        acc, r0, r1 = carry
        start = pl.multiple_of(j * tq, tq)
        kj = k_ref[0, pl.ds(start, tq), :]
        vj = v_ref[0, pl.ds(start, tq), :]
        outs, new_r = [], []
        for qh, r in zip(q_heads, (r0, r1)):
            z = _dot_t(qh, kj)
            sp = jnp.log(1.0 + jnp.exp(-jnp.abs(z)))
            log_beta = jnp.minimum(z, 0.0) - sp
            log_keep = log_beta - z
            if diag:
                log_keep = jnp.where(strict, log_keep, 0.0)
            hi = log_keep.astype(BF)
            lo = (log_keep - hi.astype(F32)).astype(BF)
            after = _dot(hi, later) + _dot(lo, later)
            w = jnp.exp(log_beta + after + r)
            if diag:
                w = jnp.where(strict, w, 0.0)
            outs.append(_dot(w.astype(BF), vj))
            new_r.append(r + jnp.sum(log_keep, axis=-1, keepdims=True))
        return acc + jnp.where(first, outs[0], outs[1]), new_r[0], new_r[1]

    def remaining(r0, r1):
        return jnp.max(jnp.maximum(r0, r1))

    def cond(c):
        return (c[0] < i) & (c[1] > SB_DEAD_LOG)

    def body(c):
        acc, r0, r1 = block(i - 1 - c[0], c[2:], False)
        return c[0] + 1, remaining(r0, r1), acc, r0, r1

    init = (jnp.zeros((tq, 2 * HEAD_DIM), F32), jnp.zeros((tq, 1), F32), jnp.zeros((tq, 1), F32))
    acc, r0, r1 = block(i, init, True)
    out = lax.while_loop(cond, body, (jnp.int32(0), remaining(r0, r1), acc, r0, r1))
    o_ref[0] = out[2].astype(o_ref.dtype)


def _sb_attention(a3, tq=256):
    b, s, _ = a3.shape
    tq = min(tq, s)
    pairs = SB_WIDTH // (2 * HEAD_DIM)
    w = 2 * HEAD_DIM
    return pl.pallas_call(
        functools.partial(_sb_body, tq=tq),
        grid=(b, pairs, s // tq),
        in_specs=[
            pl.BlockSpec((1, tq, w), lambda bi, p, i: (bi, i, p)),
            pl.BlockSpec((1, s, w), lambda bi, p, i: (bi, 0, pairs + p)),
            pl.BlockSpec((1, s, w), lambda bi, p, i: (bi, 0, 2 * pairs + p)),
        ],
        out_specs=pl.BlockSpec((1, tq, w), lambda bi, p, i: (bi, i, p)),
        out_shape=jax.ShapeDtypeStruct((b, s, SB_WIDTH), BF),
        compiler_params=_cparams(("parallel", "parallel", "arbitrary")),
        name="sb_attn",
    )(a3, a3, a3)


def _mla_prep_body(cq_ref, ckv_ref, kr_ref, cos_ref, sin_ref, qg_ref, kvg_ref, wq_ref, wkv_ref,
                   qp_ref, kp_ref, v_ref, *, scale):
    nq = _rms(cq_ref[0].astype(F32), qg_ref[...]).astype(BF)
    q = _dot(nq, wq_ref[...])
    nkv = _rms(ckv_ref[0].astype(F32), kvg_ref[...]).astype(BF)
    kv = _dot(nkv, wkv_ref[...])
    cos = cos_ref[...]
    sin = sin_ref[...]
    kr = kr_ref[0].astype(F32)
    k_pe = (kr[:, :128] * cos + kr[:, 128:] * sin).astype(BF)
    width = q.shape[1] // 3
    for p in range(width // 128):
        lo, hi = p * 128, (p + 1) * 128
        qp_ref[0, :, 2 * lo:2 * lo + 128] = (q[:, lo:hi] * scale).astype(BF)
        q_pe = q[:, width + lo:width + hi] * cos + q[:, 2 * width + lo:2 * width + hi] * sin
        qp_ref[0, :, 2 * lo + 128:2 * hi] = (q_pe * scale).astype(BF)
        kp_ref[0, :, 2 * lo:2 * lo + 128] = kv[:, lo:hi].astype(BF)
        kp_ref[0, :, 2 * lo + 128:2 * hi] = k_pe
    v_ref[0] = kv[:, width:].astype(BF)


def _mla_prep(a3, cos, sin, qg, kvg, wq, wkv, tm=512):
    b, s, _ = a3.shape
    tm = min(tm, s)
    width = MLA_HEADS * MLA_NOPE
    scale = (MLA_NOPE + MLA_ROPE) ** -0.5 * LOG2E
    out_sds = lambda n: jax.ShapeDtypeStruct((b, s, n), BF)
    return pl.pallas_call(
        functools.partial(_mla_prep_body, scale=scale),
        grid=(b, s // tm),
        in_specs=[
            pl.BlockSpec((1, tm, MLA_Q_RANK), lambda bi, i: (bi, i, 1536 // MLA_Q_RANK)),
            pl.BlockSpec((1, tm, MLA_KV_RANK), lambda bi, i: (bi, i, 1920 // MLA_KV_RANK)),
            pl.BlockSpec((1, tm, 256), lambda bi, i: (bi, i, 2048 // 256)),
            pl.BlockSpec((tm, 128), lambda bi, i: (i, 0)),
            pl.BlockSpec((tm, 128), lambda bi, i: (i, 0)),
            pl.BlockSpec((1, MLA_Q_RANK), lambda bi, i: (0, 0)),
            pl.BlockSpec((1, MLA_KV_RANK), lambda bi, i: (0, 0)),
            pl.BlockSpec(wq.shape, lambda bi, i: (0, 0)),
            pl.BlockSpec(wkv.shape, lambda bi, i: (0, 0)),
        ],
        out_specs=[
            pl.BlockSpec((1, tm, 2 * width), lambda bi, i: (bi, i, 0)),
            pl.BlockSpec((1, tm, 2 * width), lambda bi, i: (bi, i, 0)),
            pl.BlockSpec((1, tm, width), lambda bi, i: (bi, i, 0)),
        ],
        out_shape=[out_sds(2 * width), out_sds(2 * width), out_sds(width)],
        compiler_params=_cparams(("parallel", "parallel")),
        name="mla_prep",
    )(a3, a3, a3, cos, sin, qg, kvg, wq, wkv)


def _mla_body(q_ref, k_ref, v_ref, o_ref, *, tq):
    i = pl.program_id(2)
    q = q_ref[0]
    lane = lax.broadcasted_iota(jnp.int32, (1, 256), 1)
    sel0 = (lane < 64) | ((lane >= 128) & (lane < 160))
    sel1 = ((lane >= 64) & (lane < 128)) | ((lane >= 160) & (lane < 192))
    zero = jnp.zeros_like(q)
    q_heads = (jnp.where(sel0, q, zero), jnp.where(sel1, q, zero))
    row = lax.broadcasted_iota(jnp.int32, (tq, tq), 0)
    col = lax.broadcasted_iota(jnp.int32, (tq, tq), 1)
    causal = col <= row

    first = lax.broadcasted_iota(jnp.int32, (1, 128), 1) < 64

    def block(j, carry, diag):
        start = pl.multiple_of(j * tq, tq)
        kj = k_ref[0, pl.ds(start, tq), :]
        vj = v_ref[0, pl.ds(start, tq), :]
        ones = jnp.ones_like(vj)
        v_heads = (jnp.where(first, vj, ones), jnp.where(first, ones, vj))
        new = []
        for qh, vh, (m, acc) in zip(q_heads, v_heads, carry):
            s = _dot_t(qh, kj)
            if diag:
                s = jnp.where(causal, s, NEG)
            m_new = jnp.maximum(m, jnp.max(s, axis=-1, keepdims=True))
            p = jnp.exp2(s - m_new)
            acc = jnp.exp2(m - m_new) * acc + _dot(p.astype(BF), vh)
            new.append((m_new, acc))
        return tuple(new)

    one = (jnp.full((tq, 1), NEG, F32), jnp.zeros((tq, 128), F32))
    carry = lax.fori_loop(0, i, lambda j, c: block(j, c, False), (one, one))
    (_, a0), (_, a1) = block(i, carry, True)
    o_ref[0] = jnp.where(first, a0 * pltpu.roll(1.0 / a0, 64, 1),
                         a1 * pltpu.roll(1.0 / a1, 64, 1)).astype(o_ref.dtype)


def _mla_attention(qp, kp, v, tq=512):
    b, s, _ = qp.shape
    tq = min(tq, s)
    pairs = MLA_HEADS // 2
    return pl.pallas_call(
        functools.partial(_mla_body, tq=tq),
        grid=(b, pairs, s // tq),
        in_specs=[
            pl.BlockSpec((1, tq, 256), lambda bi, p, i: (bi, i, p)),
            pl.BlockSpec((1, s, 256), lambda bi, p, i: (bi, 0, p)),
            pl.BlockSpec((1, s, 128), lambda bi, p, i: (bi, 0, p)),
        ],
        out_specs=pl.BlockSpec((1, tq, 128), lambda bi, p, i: (bi, i, p)),
        out_shape=jax.ShapeDtypeStruct((b, s, MLA_HEADS * MLA_V), BF),
        compiler_params=_cparams(("parallel", "parallel", "arbitrary")),
        name="mla_attn",
    )(qp, kp, v)


def _dil_body(q_ref, kc_ref, kp_ref, vc_ref, vp_ref, bias_ref, o_ref, lse_ref, *, tl):
    j = pl.program_id(2)
    lane = lax.broadcasted_iota(jnp.int32, (1, 2 * HEAD_DIM), 1)
    first = lane < HEAD_DIM
    qi = lax.broadcasted_iota(jnp.int32, (BAND, 2 * BAND), 0)
    kj = lax.broadcasted_iota(jnp.int32, (BAND, 2 * BAND), 1)
    steps = qi + BAND - kj
    in_window = (steps >= 0) & (steps <= BAND)
    in_window_first = in_window & ((kj >= BAND) | (j > 0))
    qscale = jnp.asarray(HEAD_DIM ** -0.5, BF)
    for sb in range(tl // BAND):
        rows = slice(sb * BAND, (sb + 1) * BAND)
        prev = slice((sb - 1) * BAND, sb * BAND)
        k_prev = kp_ref[0] if sb == 0 else kc_ref[0, prev, :]
        v_prev = vp_ref[0] if sb == 0 else vc_ref[0, prev, :]
        kcat = jnp.concatenate([k_prev, kc_ref[0, rows, :]], axis=0)
        vcat = jnp.concatenate([v_prev, vc_ref[0, rows, :]], axis=0)
        valid = in_window_first if sb == 0 else in_window
        qs = q_ref[0, rows, :] * qscale
        for hp in range(DIL_WIDTH // (2 * HEAD_DIM)):
            cols = slice(hp * 2 * HEAD_DIM, (hp + 1) * 2 * HEAD_DIM)
            qb = qs[:, cols]
            kb = kcat[:, cols]
            vb = vcat[:, cols]
            zero = jnp.zeros_like(qb)
            outs, lses = [], []
            for hh in range(2):
                qh = jnp.where(first, qb, zero) if hh == 0 else jnp.where(first, zero, qb)
                s = _dot_t(qh, kb) + bias_ref[2 * hp + hh]
                s = jnp.where(valid, s, NEG)
                m = jnp.max(s, axis=-1, keepdims=True)
                p = jnp.exp(s - m)
                l = jnp.sum(p, axis=-1, keepdims=True)
                outs.append(_dot(p.astype(BF), vb) / l)
                lses.append(m + jnp.log(l))
            o_ref[0, rows, cols] = jnp.where(first, outs[0], outs[1]).astype(o_ref.dtype)
            lse_ref[0, rows, cols] = jnp.where(first, lses[0], lses[1])


def _dil_attention(dl, bias, group, dilation, batch, seq, tl=512):
    r = dilation
    length = seq // r
    tl = min(tl, length)
    ncol = dl.shape[1] // DIL_WIDTH
    view = dl.reshape(batch, length, r * dl.shape[1])
    sub = tl // BAND

    def cur(c):
        return pl.BlockSpec((1, tl, DIL_WIDTH), lambda bi, res, j: (bi, j, res * ncol + 3 * group + c))

    def prev(c):
        return pl.BlockSpec((1, BAND, DIL_WIDTH),
                            lambda bi, res, j: (bi, jnp.maximum(j * sub - 1, 0), res * ncol + 3 * group + c))

    out_spec = pl.BlockSpec((1, tl, DIL_WIDTH), lambda bi, res, j: (bi, j, res))
    out, lse = pl.pallas_call(
        functools.partial(_dil_body, tl=tl),
        grid=(batch, r, length // tl),
        in_specs=[cur(0), cur(1), prev(1), cur(2), prev(2),
                  pl.BlockSpec(bias.shape, lambda bi, res, j: (0, 0, 0))],
        out_specs=[out_spec, out_spec],
        out_shape=[jax.ShapeDtypeStruct((batch, length, r * DIL_WIDTH), BF),
                   jax.ShapeDtypeStruct((batch, length, r * DIL_WIDTH), F32)],
        compiler_params=_cparams(("parallel", "parallel", "arbitrary")),
        name=f"dil_attn_r{r}",
    )(view, view, view, view, view, bias)
    t = batch * seq
    return out.reshape(t, DIL_WIDTH), lse.reshape(t, DIL_WIDTH)


def _merge_body(h_ref, oa_ref, ob_ref, oc0_ref, oc1_ref, oc2_ref, l0_ref, l1_ref, l2_ref,
                ga_ref, gb_ref, gc_ref, gbias_ref, wa_ref, wb_ref, wc_ref, wo_ref, o_ref):
    l0, l1, l2 = l0_ref[...], l1_ref[...], l2_ref[...]
    mx = jnp.maximum(jnp.maximum(l0, l1), l2)
    e0, e1, e2 = jnp.exp(l0 - mx), jnp.exp(l1 - mx), jnp.exp(l2 - mx)
    oc = (e0 * oc0_ref[...].astype(F32) + e1 * oc1_ref[...].astype(F32)
          + e2 * oc2_ref[...].astype(F32)) / (e0 + e1 + e2)
    gbias = gbias_ref[...]
    merged = (jax.nn.sigmoid(ga_ref[...].astype(F32) + gbias[0:1]) * _dot(oa_ref[...], wa_ref[...])
              + jax.nn.sigmoid(gb_ref[...].astype(F32) + gbias[1:2]) * _dot(ob_ref[...], wb_ref[...])
              + jax.nn.sigmoid(gc_ref[...].astype(F32) + gbias[2:3]) * _dot(oc.astype(BF), wc_ref[...]))
    o_ref[...] = h_ref[...] + _dot(merged.astype(BF), wo_ref[...])


def _merge(h, oa, ob, ocs, lses, gates, gbias, wa, wb, wc, wo, tm=512):
    t, d = h.shape
    tm = min(tm, t)
    row = lambda n: pl.BlockSpec((tm, n), lambda i: (i, 0))
    gate = lambda c: pl.BlockSpec((tm, d), lambda i: (i, c))
    return pl.pallas_call(
        _merge_body,
        grid=(t // tm,),
        in_specs=[row(d), row(SB_WIDTH), row(SB_WIDTH)] + [row(DIL_WIDTH)] * 6
        + [gate(0), gate(1), gate(2), _resident(gbias.shape),
           _resident(wa.shape), _resident(wb.shape), _resident(wc.shape), _resident(wo.shape)],
        out_specs=row(d),
        out_shape=jax.ShapeDtypeStruct((t, d), F32),
        compiler_params=_cparams(("parallel",)),
        name="merge",
    )(h, oa, ob, *ocs, *lses, gates, gates, gates, gbias, wa, wb, wc, wo)


def _memkv_body(m_ref, g_ref, w_ref, o_ref):
    u = _rms(m_ref[0], g_ref[...]).astype(BF)
    o_ref[0] = _dot(u, w_ref[...]).astype(o_ref.dtype)


def _mem_kv(mem, g, w):
    b, m, d = mem.shape
    n = w.shape[1]
    return pl.pallas_call(
        _memkv_body,
        grid=(b,),
        in_specs=[pl.BlockSpec((1, m, d), lambda i: (i, 0, 0)),
                  pl.BlockSpec((1, d), lambda i: (0, 0)),
                  pl.BlockSpec((d, n), lambda i: (0, 0))],
        out_specs=pl.BlockSpec((1, m, n), lambda i: (i, 0, 0)),
        out_shape=jax.ShapeDtypeStruct((b, m, n), BF),
        compiler_params=_cparams(("parallel",)),
        name="mem_kv",
    )(mem, g, w)


def _xattn_body(h_ref, g_ref, kv_ref, wq_ref, wo_ref, o_ref):
    x = h_ref[0]
    u = _rms(x, g_ref[...]).astype(BF)
    q = _dot(u, wq_ref[...]).astype(BF)
    kv = kv_ref[0]
    width = X_HEADS * X_HEAD_DIM
    scale = X_HEAD_DIM ** -0.5
    outs = []
    for h in range(X_HEADS):
        cols = slice(h * X_HEAD_DIM, (h + 1) * X_HEAD_DIM)
        s = _dot_t(q[:, cols], kv[:, cols]) * scale
        m = jnp.max(s, axis=-1, keepdims=True)
        e = jnp.exp(s - m)
        p = e / jnp.sum(e, axis=-1, keepdims=True)
        outs.append(_dot(p.astype(BF), kv[:, width + h * X_HEAD_DIM:width + (h + 1) * X_HEAD_DIM]))
    o = jnp.concatenate(outs, axis=-1).astype(BF)
    o_ref[0] = x + _dot(o, wo_ref[...])


def _xattn(h3, g, kv, wq, wo, tm=512):
    b, s, d = h3.shape
    tm = min(tm, s)
    m = kv.shape[1]
    return pl.pallas_call(
        _xattn_body,
        grid=(b, s // tm),
        in_specs=[pl.BlockSpec((1, tm, d), lambda bi, i: (bi, i, 0)),
                  pl.BlockSpec((1, d), lambda bi, i: (0, 0)),
                  pl.BlockSpec((1, m, kv.shape[2]), lambda bi, i: (bi, 0, 0)),
                  pl.BlockSpec(wq.shape, lambda bi, i: (0, 0)),
                  pl.BlockSpec(wo.shape, lambda bi, i: (0, 0))],
        out_specs=pl.BlockSpec((1, tm, d), lambda bi, i: (bi, i, 0)),
        out_shape=jax.ShapeDtypeStruct((b, s, d), F32),
        compiler_params=_cparams(("parallel", "parallel")),
        name="xattn",
    )(h3, g, kv, wq, wo)


def _rel_bucket(dist):
    exact = REL_BUCKETS // 2
    d = jnp.maximum(dist, exact).astype(F32)
    large = exact + (jnp.log(d / exact) / math.log(REL_MAX_DIST / exact)
                     * (REL_BUCKETS - exact)).astype(jnp.int32)
    return jnp.where(dist < exact, dist, jnp.minimum(large, REL_BUCKETS - 1))


def _band_bias(rel_bias, group, dilation):
    heads = DIL_WIDTH // HEAD_DIM
    qi = jnp.arange(BAND)
    kj = jnp.arange(2 * BAND)
    steps = (qi[:, None] + BAND) - kj[None, :]
    table = rel_bias[:, group * heads:(group + 1) * heads].astype(F32)
    bucket = _rel_bucket(jnp.clip(steps, 0, BAND) * dilation)
    onehot = (bucket[None] == jnp.arange(REL_BUCKETS)[:, None, None]).astype(F32)
    return jnp.einsum('bh,bqk->hqk', table, onehot, precision=lax.Precision.HIGHEST)


def _swap_halves(w, width):
    k, n = w.shape
    w = w.reshape(k, n // width, 2, width // 2)
    return w[:, :, ::-1, :].reshape(k, n)


def _pack_w_in(w_in):
    d = w_in.shape[0]
    sb = w_in[:, :1536]
    cq = w_in[:, 1536:1920]
    ckv = w_in[:, 1920:2048]
    kr = w_in[:, 2048:2080]
    zeros = jnp.zeros((d, 64), w_in.dtype)
    kr_sw = _swap_halves(kr, MLA_ROPE)
    seg_a = jnp.concatenate([sb, cq, ckv, kr, kr, zeros, kr_sw, kr_sw, zeros], axis=1)
    seg_d = w_in[:, 2080:2080 + 9 * DIL_WIDTH]
    seg_g = w_in[:, 2080 + 9 * DIL_WIDTH:]
    return seg_a.astype(BF), seg_d.astype(BF), seg_g.astype(BF)


def _pack_w_uq(w_uq):
    k = w_uq.shape[0]
    w = w_uq.reshape(k, MLA_HEADS, MLA_NOPE + MLA_ROPE)
    nope = w[:, :, :MLA_NOPE].reshape(k, MLA_HEADS * MLA_NOPE)
    rope = w[:, :, MLA_NOPE:]

    def pair_layout(rp):
        rp = rp.reshape(k, MLA_HEADS // 2, 2 * MLA_ROPE)
        pad = jnp.zeros((k, MLA_HEADS // 2, 128 - 2 * MLA_ROPE), rp.dtype)
        return jnp.concatenate([rp, pad], axis=-1).reshape(k, (MLA_HEADS // 2) * 128)

    rope_sw = rope.reshape(k, MLA_HEADS, 2, MLA_ROPE // 2)[:, :, ::-1, :].reshape(k, MLA_HEADS, MLA_ROPE)
    return jnp.concatenate([nope, pair_layout(rope), pair_layout(rope_sw)], axis=1).astype(BF)


def _pack_w_ukv(w_ukv):
    k = w_ukv.shape[0]
    w = w_ukv.reshape(k, MLA_HEADS, MLA_NOPE + MLA_V)
    return jnp.concatenate([w[:, :, :MLA_NOPE].reshape(k, -1), w[:, :, MLA_NOPE:].reshape(k, -1)],
                           axis=1).astype(BF)


def _rope_tables(seq):
    half = MLA_ROPE // 2
    freqs = ROPE_THETA ** (-jnp.arange(half, dtype=F32) / half)
    ang = jnp.arange(seq).astype(F32)[:, None] * freqs[None, :]
    cos, sin = jnp.cos(ang), jnp.sin(ang)
    pad = jnp.zeros((seq, 128 - 2 * MLA_ROPE), F32)
    cos_t = jnp.concatenate([cos, cos, cos, cos, pad], axis=1)
    sin_t = jnp.concatenate([-sin, sin, -sin, sin, pad], axis=1)
    return cos_t, sin_t


def kernel(x, mem, ffn1_norm, ffn1_w_gate, ffn1_w_up, ffn1_w_down, mix_norm, w_in, gate_bias, mla_q_norm, mla_w_uq, mla_kv_norm, mla_w_ukv, w_branch_a, w_branch_b, w_branch_c, w_mix_out, rel_bias, xattn_norm, mem_norm, xattn_w_q, xattn_w_kv, xattn_w_o, ffn2_norm, ffn2_w_gate, ffn2_w_up, ffn2_w_down, final_norm):
    b, s, d = x.shape
    t = b * s
    depth = w_in.shape[0]
    cos_t, sin_t = _rope_tables(s)
    biases = [_band_bias(rel_bias, g, dil) for g, (_, dil) in enumerate(DIL_GROUPS)]
    row = lambda v: v.reshape(1, -1)
    bf = lambda w: w.astype(BF)
    fg = row(final_norm)

    h = x.reshape(t, d)
    for l in range(depth):
        h = _ffn(h, row(ffn1_norm[l]), bf(ffn1_w_gate[l]), bf(ffn1_w_up[l]), bf(ffn1_w_down[l]), fg, False)

        w_a, w_d, w_g = _pack_w_in(w_in[l])
        g_mix = row(mix_norm[l])
        seg_a = _norm_proj(h, g_mix, w_a, tn=w_a.shape[1])
        seg_d = _norm_proj(h, g_mix, w_d, tn=w_d.shape[1] // 2)
        seg_g = _norm_proj(h, g_mix, w_g, tn=w_g.shape[1] // 2)

        a3 = seg_a.reshape(b, s, seg_a.shape[1])
        o_a = _sb_attention(a3).reshape(t, SB_WIDTH)
        qp, kp, v = _mla_prep(a3, cos_t, sin_t, row(mla_q_norm[l]), row(mla_kv_norm[l]),
                              _pack_w_uq(mla_w_uq[l]), _pack_w_ukv(mla_w_ukv[l]))
        o_b = _mla_attention(qp, kp, v).reshape(t, MLA_HEADS * MLA_V)
        ocs, lses = [], []
        for g, (_, dil) in enumerate(DIL_GROUPS):
            o, lse = _dil_attention(seg_d, biases[g], g, dil, b, s)
            ocs.append(o)
            lses.append(lse)
        h = _merge(h, o_a, o_b, ocs, lses, seg_g, gate_bias[l], bf(w_branch_a[l]), bf(w_branch_b[l]),
                   bf(w_branch_c[l]), bf(w_mix_out[l]))

        kv = _mem_kv(mem, row(mem_norm[l]), bf(xattn_w_kv[l]))
        h = _xattn(h.reshape(b, s, d), row(xattn_norm[l]), kv, bf(xattn_w_q[l]), bf(xattn_w_o[l])).reshape(t, d)

        h = _ffn(h, row(ffn2_norm[l]), bf(ffn2_w_gate[l]), bf(ffn2_w_up[l]), bf(ffn2_w_down[l]), fg,
                 l == depth - 1)
    return h.reshape(b, s, d)
```

```python
import functools
import math

import jax
import jax.numpy as jnp
import numpy as np
from jax import lax
from jax.experimental import pallas as pl
from jax.experimental.pallas import tpu as pltpu

BF = jnp.bfloat16
F32 = jnp.float32

EPS = 1e-6
NEG = -1e30
FFN_RESIDUAL = 0.5

SB_WIDTH = 512
MLA_HEADS = 8
MLA_Q_RANK = 384
MLA_KV_RANK = 128
MLA_NOPE = 64
MLA_ROPE = 32
MLA_V = 64
ROPE_THETA = 10000.0
DIL_GROUPS = ((128, 1), (512, 4), (2048, 16))
DIL_WIDTH = 512
REL_BUCKETS = 32
REL_MAX_DIST = 2048
X_HEADS = 4
X_HEAD_DIM = 128
HEAD_DIM = 64
BAND = 128
LANES = 128
SB_DEAD_LOG = -100.0
LOG2E = math.log2(math.e)

V7X_VMEM_LIMIT = 48 * 1024 * 1024
FF_CHUNK = 256


def _cparams(sem):
    return pltpu.CompilerParams(dimension_semantics=sem, vmem_limit_bytes=V7X_VMEM_LIMIT)


def _rms(x, g):
    ms = jnp.mean(x * x, axis=-1, keepdims=True)
    return x * lax.rsqrt(ms + EPS) * g


def _dot(a, b):
    return jnp.dot(a, b, preferred_element_type=F32)


def _dot_t(a, b):
    return lax.dot_general(a, b, (((1,), (1,)), ((), ())), preferred_element_type=F32)


def _resident(shape):
    nd = len(shape)
    return pl.BlockSpec(shape, lambda *_: (0,) * nd, pipeline_mode=pl.Buffered(1))


def _ffn_body(h_ref, g_ref, wg_ref, wu_ref, wd_ref, fg_ref, o_ref, act_ref, *, final):
    x = h_ref[...]
    u = _rms(x, g_ref[...]).astype(BF)
    d_ff = wg_ref.shape[1]
    for c in range(d_ff // FF_CHUNK):
        sl = slice(c * FF_CHUNK, (c + 1) * FF_CHUNK)
        a = _dot(u, wg_ref[:, sl])
        b = _dot(u, wu_ref[:, sl])
        act_ref[:, sl] = (a * jax.nn.sigmoid(a) * b).astype(BF)
    y = x + FFN_RESIDUAL * _dot(act_ref[...], wd_ref[...])
    if final:
        y = _rms(y, fg_ref[...])
    o_ref[...] = y


def _ffn(h, g, wg, wu, wd, fg, final, tm=512):
    t, d = h.shape
    d_ff = wg.shape[1]
    tm = min(tm, t)
    return pl.pallas_call(
        functools.partial(_ffn_body, final=final),
        grid=(t // tm,),
        in_specs=[
            pl.BlockSpec((tm, d), lambda i: (i, 0)),
            _resident((1, d)),
            _resident((d, d_ff)),
            _resident((d, d_ff)),
            _resident((d_ff, d)),
            _resident((1, d)),
        ],
        out_specs=pl.BlockSpec((tm, d), lambda i: (i, 0)),
        out_shape=jax.ShapeDtypeStruct((t, d), F32),
        scratch_shapes=[pltpu.VMEM((tm, d_ff), BF)],
        compiler_params=_cparams(("parallel",)),
        name="ffn",
    )(h, g, wg, wu, wd, fg)


def _proj_body(h_ref, g_ref, w_ref, o_ref, u_ref):
    @pl.when(pl.program_id(1) == 0)
    def _():
        u_ref[...] = _rms(h_ref[...], g_ref[...]).astype(BF)

    o_ref[...] = _dot(u_ref[...], w_ref[...]).astype(o_ref.dtype)


def _norm_proj(h, g, w, tn, tm=512):
    t, d = h.shape
    n = w.shape[1]
    tm = min(tm, t)
    return pl.pallas_call(
        _proj_body,
        grid=(t // tm, n // tn),
        in_specs=[
            pl.BlockSpec((tm, d), lambda i, j: (i, 0)),
            pl.BlockSpec((1, d), lambda i, j: (0, 0)),
            pl.BlockSpec((d, tn), lambda i, j: (0, j)),
        ],
        out_specs=pl.BlockSpec((tm, tn), lambda i, j: (i, j)),
        out_shape=jax.ShapeDtypeStruct((t, n), BF),
        scratch_shapes=[pltpu.VMEM((tm, d), BF)],
        compiler_params=_cparams(("parallel", "arbitrary")),
        name="norm_proj",
    )(h, g, w)


def _dil_proj_body(h_ref, g_ref, w_ref, o_ref, *stage, r):
    rows = h_ref.shape[1] // r
    n = w_ref.shape[1]
    u = _rms(h_ref[0], g_ref[...])
    if r == 1:
        o_ref[0] = _dot(u.astype(BF), w_ref[...]).astype(o_ref.dtype)
        return
    (stage,) = stage
    chunks = u.shape[1] // LANES
    for c in range(chunks):
        stage[c] = u[:, c * LANES:(c + 1) * LANES]
    for res in range(r):
        ur = jnp.concatenate([stage[c, pl.ds(res, rows, stride=r), :] for c in range(chunks)], axis=1)
        o_ref[0, :, res * n:(res + 1) * n] = _dot(ur.astype(BF), w_ref[...]).astype(o_ref.dtype)


def _dil_proj(h3, g, w, r):
    b, s, d = h3.shape
    n = w.shape[1]
    tm = min(s, max(1024, BAND * r) if r > 1 else 512)
    return pl.pallas_call(
        functools.partial(_dil_proj_body, r=r),
        grid=(b, s // tm),
        in_specs=[
            pl.BlockSpec((1, tm, d), lambda bi, i: (bi, i, 0)),
            pl.BlockSpec((1, d), lambda bi, i: (0, 0)),
            pl.BlockSpec((d, n), lambda bi, i: (0, 0)),
        ],
        out_specs=pl.BlockSpec((1, tm // r, r * n), lambda bi, i: (bi, i, 0)),
        out_shape=jax.ShapeDtypeStruct((b, s // r, r * n), BF),
        scratch_shapes=[pltpu.VMEM((d // LANES, tm, LANES), F32)] if r > 1 else [],
        compiler_params=_cparams(("parallel", "parallel")),
        name=f"dil_proj_r{r}",
    )(h3, g, w)


def _sb_body(q_ref, k_ref, v_ref, o_ref, *, tq):
    i = pl.program_id(2)
    q = q_ref[0] * jnp.asarray(HEAD_DIM ** -0.5, BF)
    lane = lax.broadcasted_iota(jnp.int32, (1, 2 * HEAD_DIM), 1)
    first = lane < HEAD_DIM
    zero = jnp.zeros_like(q)
    q_heads = (jnp.where(first, q, zero), jnp.where(first, zero, q))
    row = lax.broadcasted_iota(jnp.int32, (tq, tq), 0)
    col = lax.broadcasted_iota(jnp.int32, (tq, tq), 1)
    strict = col < row
    later = jnp.where(row > col, 1.0, 0.0).astype(BF)

    def block(j, carry, diag):
        acc, r0, r1 = carry
        start = pl.multiple_of(j * tq, tq)
        kj = k_ref[0, pl.ds(start, tq), :]
        vj = v_ref[0, pl.ds(start, tq), :]
        outs, new_r = [], []
        for qh, r in zip(q_heads, (r0, r1)):
            z = _dot_t(qh, kj)
            sp = jnp.log(1.0 + jnp.exp(-jnp.abs(z)))
            log_beta = jnp.minimum(z, 0.0) - sp
            log_keep = log_beta - z
            if diag:
                log_keep = jnp.where(strict, log_keep, 0.0)
            hi = log_keep.astype(BF)
            lo = (log_keep - hi.astype(F32)).astype(BF)
            after = _dot(hi, later) + _dot(lo, later)
            w = jnp.exp(log_beta + after + r)
            if diag:
                w = jnp.where(strict, w, 0.0)
            outs.append(_dot(w.astype(BF), vj))
            new_r.append(r + jnp.sum(log_keep, axis=-1, keepdims=True))
        return acc + jnp.where(first, outs[0], outs[1]), new_r[0], new_r[1]

    def remaining(r0, r1):
        return jnp.max(jnp.maximum(r0, r1))

    def cond(c):
        return (c[0] < i) & (c[1] > SB_DEAD_LOG)

    def body(c):
        acc, r0, r1 = block(i - 1 - c[0], c[2:], False)
        return c[0] + 1, remaining(r0, r1), acc, r0, r1

    init = (jnp.zeros((tq, 2 * HEAD_DIM), F32), jnp.zeros((tq, 1), F32), jnp.zeros((tq, 1), F32))
    acc, r0, r1 = block(i, init, True)
    out = lax.while_loop(cond, body, (jnp.int32(0), remaining(r0, r1), acc, r0, r1))
    o_ref[0] = out[2].astype(o_ref.dtype)


def _sb_attention(a3, tq=256):
    b, s, _ = a3.shape
    tq = min(tq, s)
    pairs = SB_WIDTH // (2 * HEAD_DIM)
    w = 2 * HEAD_DIM
    return pl.pallas_call(
        functools.partial(_sb_body, tq=tq),
        grid=(b, pairs, s // tq),
        in_specs=[
            pl.BlockSpec((1, tq, w), lambda bi, p, i: (bi, i, p)),
            pl.BlockSpec((1, s, w), lambda bi, p, i: (bi, 0, pairs + p)),
            pl.BlockSpec((1, s, w), lambda bi, p, i: (bi, 0, 2 * pairs + p)),
        ],
        out_specs=pl.BlockSpec((1, tq, w), lambda bi, p, i: (bi, i, p)),
        out_shape=jax.ShapeDtypeStruct((b, s, SB_WIDTH), BF),
        compiler_params=_cparams(("parallel", "parallel", "arbitrary")),
        name="sb_attn",
    )(a3, a3, a3)


def _mla_prep_body(cq_ref, ckv_ref, kr_ref, cos_ref, sin_ref, qg_ref, kvg_ref, wq_ref, wkv_ref,
                   qp_ref, kp_ref, v_ref, *, scale):
    nq = _rms(cq_ref[0].astype(F32), qg_ref[...]).astype(BF)
    q = _dot(nq, wq_ref[...])
    nkv = _rms(ckv_ref[0].astype(F32), kvg_ref[...]).astype(BF)
    kv = _dot(nkv, wkv_ref[...])
    cos = cos_ref[...]
    sin = sin_ref[...]
    kr = kr_ref[0].astype(F32)
    k_pe = (kr[:, :128] * cos + kr[:, 128:] * sin).astype(BF)
    width = q.shape[1] // 3
    for p in range(width // 128):
        lo, hi = p * 128, (p + 1) * 128
        qp_ref[0, :, 2 * lo:2 * lo + 128] = (q[:, lo:hi] * scale).astype(BF)
        q_pe = q[:, width + lo:width + hi] * cos + q[:, 2 * width + lo:2 * width + hi] * sin
        qp_ref[0, :, 2 * lo + 128:2 * hi] = (q_pe * scale).astype(BF)
        kp_ref[0, :, 2 * lo:2 * lo + 128] = kv[:, lo:hi].astype(BF)
        kp_ref[0, :, 2 * lo + 128:2 * hi] = k_pe
    v_ref[0] = kv[:, width:].astype(BF)


def _mla_prep(a3, cos, sin, qg, kvg, wq, wkv, tm=512):
    b, s, _ = a3.shape
    tm = min(tm, s)
    width = MLA_HEADS * MLA_NOPE
    scale = (MLA_NOPE + MLA_ROPE) ** -0.5 * LOG2E
    out_sds = lambda n: jax.ShapeDtypeStruct((b, s, n), BF)
    return pl.pallas_call(
        functools.partial(_mla_prep_body, scale=scale),
        grid=(b, s // tm),
        in_specs=[
            pl.BlockSpec((1, tm, MLA_Q_RANK), lambda bi, i: (bi, i, 1536 // MLA_Q_RANK)),
            pl.BlockSpec((1, tm, MLA_KV_RANK), lambda bi, i: (bi, i, 1920 // MLA_KV_RANK)),
            pl.BlockSpec((1, tm, 256), lambda bi, i: (bi, i, 2048 // 256)),
            pl.BlockSpec((tm, 128), lambda bi, i: (i, 0)),
            pl.BlockSpec((tm, 128), lambda bi, i: (i, 0)),
            pl.BlockSpec((1, MLA_Q_RANK), lambda bi, i: (0, 0)),
            pl.BlockSpec((1, MLA_KV_RANK), lambda bi, i: (0, 0)),
            pl.BlockSpec(wq.shape, lambda bi, i: (0, 0)),
            pl.BlockSpec(wkv.shape, lambda bi, i: (0, 0)),
        ],
        out_specs=[
            pl.BlockSpec((1, tm, 2 * width), lambda bi, i: (bi, i, 0)),
            pl.BlockSpec((1, tm, 2 * width), lambda bi, i: (bi, i, 0)),
            pl.BlockSpec((1, tm, width), lambda bi, i: (bi, i, 0)),
        ],
        out_shape=[out_sds(2 * width), out_sds(2 * width), out_sds(width)],
        compiler_params=_cparams(("parallel", "parallel")),
        name="mla_prep",
    )(a3, a3, a3, cos, sin, qg, kvg, wq, wkv)


def _mla_body(q_ref, k_ref, v_ref, o_ref, *, tq):
    i = pl.program_id(2)
    q = q_ref[0]
    lane = lax.broadcasted_iota(jnp.int32, (1, 256), 1)
    sel0 = (lane < 64) | ((lane >= 128) & (lane < 160))
    sel1 = ((lane >= 64) & (lane < 128)) | ((lane >= 160) & (lane < 192))
    zero = jnp.zeros_like(q)
    q_heads = (jnp.where(sel0, q, zero), jnp.where(sel1, q, zero))
    row = lax.broadcasted_iota(jnp.int32, (tq, tq), 0)
    col = lax.broadcasted_iota(jnp.int32, (tq, tq), 1)
    causal = col <= row

    first = lax.broadcasted_iota(jnp.int32, (1, 128), 1) < 64

    def block(j, carry, diag):
        start = pl.multiple_of(j * tq, tq)
        kj = k_ref[0, pl.ds(start, tq), :]
        vj = v_ref[0, pl.ds(start, tq), :]
        ones = jnp.ones_like(vj)
        v_heads = (jnp.where(first, vj, ones), jnp.where(first, ones, vj))
        new = []
        for qh, vh, (m, acc) in zip(q_heads, v_heads, carry):
            s = _dot_t(qh, kj)
            if diag:
                s = jnp.where(causal, s, NEG)
            m_new = jnp.maximum(m, jnp.max(s, axis=-1, keepdims=True))
            p = jnp.exp2(s - m_new)
            acc = jnp.exp2(m - m_new) * acc + _dot(p.astype(BF), vh)
            new.append((m_new, acc))
        return tuple(new)

    one = (jnp.full((tq, 1), NEG, F32), jnp.zeros((tq, 128), F32))
    carry = lax.fori_loop(0, i, lambda j, c: block(j, c, False), (one, one))
    (_, a0), (_, a1) = block(i, carry, True)
    o_ref[0] = jnp.where(first, a0 * pltpu.roll(1.0 / a0, 64, 1),
                         a1 * pltpu.roll(1.0 / a1, 64, 1)).astype(o_ref.dtype)


def _mla_attention(qp, kp, v, tq=512):
    b, s, _ = qp.shape
    tq = min(tq, s)
    pairs = MLA_HEADS // 2
    return pl.pallas_call(
        functools.partial(_mla_body, tq=tq),
        grid=(b, pairs, s // tq),
        in_specs=[
            pl.BlockSpec((1, tq, 256), lambda bi, p, i: (bi, i, p)),
            pl.BlockSpec((1, s, 256), lambda bi, p, i: (bi, 0, p)),
            pl.BlockSpec((1, s, 128), lambda bi, p, i: (bi, 0, p)),
        ],
        out_specs=pl.BlockSpec((1, tq, 128), lambda bi, p, i: (bi, i, p)),
        out_shape=jax.ShapeDtypeStruct((b, s, MLA_HEADS * MLA_V), BF),
        compiler_params=_cparams(("parallel", "parallel", "arbitrary")),
        name="mla_attn",
    )(qp, kp, v)


def _dil_body(q_ref, kc_ref, kp_ref, vc_ref, vp_ref, bias_ref, o_ref, lse_ref, *, tl):
    j = pl.program_id(2)
    lane = lax.broadcasted_iota(jnp.int32, (1, 2 * HEAD_DIM), 1)
    first = lane < HEAD_DIM
    qi = lax.broadcasted_iota(jnp.int32, (BAND, 2 * BAND), 0)
    kj = lax.broadcasted_iota(jnp.int32, (BAND, 2 * BAND), 1)
    steps = qi + BAND - kj
    in_window = (steps >= 0) & (steps <= BAND)
    in_window_first = in_window & ((kj >= BAND) | (j > 0))
    qscale = jnp.asarray(HEAD_DIM ** -0.5, BF)
    for sb in range(tl // BAND):
        rows = slice(sb * BAND, (sb + 1) * BAND)
        prev = slice((sb - 1) * BAND, sb * BAND)
        k_prev = kp_ref[0] if sb == 0 else kc_ref[0, prev, :]
        v_prev = vp_ref[0] if sb == 0 else vc_ref[0, prev, :]
        kcat = jnp.concatenate([k_prev, kc_ref[0, rows, :]], axis=0)
        vcat = jnp.concatenate([v_prev, vc_ref[0, rows, :]], axis=0)
        valid = in_window_first if sb == 0 else in_window
        qs = q_ref[0, rows, :] * qscale
        for hp in range(DIL_WIDTH // (2 * HEAD_DIM)):
            cols = slice(hp * 2 * HEAD_DIM, (hp + 1) * 2 * HEAD_DIM)
            qb = qs[:, cols]
            kb = kcat[:, cols]
            vb = vcat[:, cols]
            zero = jnp.zeros_like(qb)
            outs, lses = [], []
            for hh in range(2):
                qh = jnp.where(first, qb, zero) if hh == 0 else jnp.where(first, zero, qb)
                s = _dot_t(qh, kb) + bias_ref[2 * hp + hh]
                s = jnp.where(valid, s, NEG)
                m = jnp.max(s, axis=-1, keepdims=True)
                p = jnp.exp(s - m)
                l = jnp.sum(p, axis=-1, keepdims=True)
                outs.append(_dot(p.astype(BF), vb) / l)
                lses.append(m + jnp.log(l))
            o_ref[0, rows, cols] = jnp.where(first, outs[0], outs[1]).astype(o_ref.dtype)
            lse_ref[0, rows, cols] = jnp.where(first, lses[0], lses[1])


def _dil_attention(view, bias, dilation, tl=512):
    r = dilation
    batch, length, _ = view.shape
    tl = min(tl, length)
    sub = tl // BAND

    def cur(c):
        return pl.BlockSpec((1, tl, DIL_WIDTH), lambda bi, res, j: (bi, j, res * 3 + c))

    def prev(c):
        return pl.BlockSpec((1, BAND, DIL_WIDTH),
                            lambda bi, res, j: (bi, jnp.maximum(j * sub - 1, 0), res * 3 + c))

    out_spec = pl.BlockSpec((1, tl, DIL_WIDTH), lambda bi, res, j: (bi, j, res))
    return pl.pallas_call(
        functools.partial(_dil_body, tl=tl),
        grid=(batch, r, length // tl),
        in_specs=[cur(0), cur(1), prev(1), cur(2), prev(2),
                  pl.BlockSpec(bias.shape, lambda bi, res, j: (0, 0, 0))],
        out_specs=[out_spec, out_spec],
        out_shape=[jax.ShapeDtypeStruct((batch, length, r * DIL_WIDTH), BF),
                   jax.ShapeDtypeStruct((batch, length, r * DIL_WIDTH), F32)],
        compiler_params=_cparams(("parallel", "parallel", "arbitrary")),
        name=f"dil_attn_r{r}",
    )(view, view, view, view, view, bias)


def _merge_body(h_ref, oa_ref, ob_ref, oc0_ref, oc1_ref, oc2_ref, l0_ref, l1_ref, l2_ref,
                ga_ref, gb_ref, gc_ref, gbias_ref, wa_ref, wb_ref, wc_ref, wo_ref, o_ref, *scratch):
    tm = h_ref.shape[1]
    scratch = list(scratch)

    def token_major(ref, r):
        if r == 1:
            return ref[0].astype(F32)
        buf = scratch.pop(0)
        chunks = DIL_WIDTH // LANES
        for res in range(r):
            for c in range(chunks):
                lo = res * DIL_WIDTH + c * LANES
                buf[c, pl.ds(res, tm // r, stride=r), :] = ref[0, :, lo:lo + LANES].astype(F32)
        return jnp.concatenate([buf[c] for c in range(chunks)], axis=1)

    dils = [dil for _, dil in DIL_GROUPS]
    l0, l1, l2 = (token_major(ref, r) for ref, r in zip((l0_ref, l1_ref, l2_ref), dils))
    mx = jnp.maximum(jnp.maximum(l0, l1), l2)
    e0, e1, e2 = jnp.exp(l0 - mx), jnp.exp(l1 - mx), jnp.exp(l2 - mx)
    o0, o1, o2 = (token_major(ref, r) for ref, r in zip((oc0_ref, oc1_ref, oc2_ref), dils))
    oc = (e0 * o0 + e1 * o1 + e2 * o2) / (e0 + e1 + e2)
    gbias = gbias_ref[...]
    merged = (jax.nn.sigmoid(ga_ref[0].astype(F32) + gbias[0:1]) * _dot(oa_ref[0], wa_ref[...])
              + jax.nn.sigmoid(gb_ref[0].astype(F32) + gbias[1:2]) * _dot(ob_ref[0], wb_ref[...])
              + jax.nn.sigmoid(gc_ref[0].astype(F32) + gbias[2:3]) * _dot(oc.astype(BF), wc_ref[...]))
    o_ref[0] = h_ref[0] + _dot(merged.astype(BF), wo_ref[...])


def _merge(h3, oa, ob, ocs, lses, gates, gbias, wa, wb, wc, wo, tm=512):
    b, s, d = h3.shape
    tm = min(tm, s)
    row = lambda n: pl.BlockSpec((1, tm, n), lambda bi, i: (bi, i, 0))
    gate = lambda c: pl.BlockSpec((1, tm, d), lambda bi, i: (bi, i, c))
    views = [pl.BlockSpec((1, tm // r, r * DIL_WIDTH), lambda bi, i: (bi, i, 0)) for _, r in DIL_GROUPS]
    n_buf = 2 * sum(1 for _, r in DIL_GROUPS if r > 1)
    return pl.pallas_call(
        _merge_body,
        grid=(b, s // tm),
        in_specs=[row(d), row(SB_WIDTH), row(SB_WIDTH)] + views + views
        + [gate(0), gate(1), gate(2), _resident(gbias.shape),
           _resident(wa.shape), _resident(wb.shape), _resident(wc.shape), _resident(wo.shape)],
        out_specs=row(d),
        out_shape=jax.ShapeDtypeStruct((b, s, d), F32),
        scratch_shapes=[pltpu.VMEM((DIL_WIDTH // LANES, tm, LANES), F32)] * n_buf,
        compiler_params=_cparams(("parallel", "parallel")),
        name="merge",
    )(h3, oa, ob, *ocs, *lses, gates, gates, gates, gbias, wa, wb, wc, wo)


def _memkv_body(m_ref, g_ref, w_ref, o_ref):
    u = _rms(m_ref[0], g_ref[...]).astype(BF)
    o_ref[0] = _dot(u, w_ref[...]).astype(o_ref.dtype)


def _mem_kv(mem, g, w):
    b, m, d = mem.shape
    n = w.shape[1]
    return pl.pallas_call(
        _memkv_body,
        grid=(b,),
        in_specs=[pl.BlockSpec((1, m, d), lambda i: (i, 0, 0)),
                  pl.BlockSpec((1, d), lambda i: (0, 0)),
                  pl.BlockSpec((d, n), lambda i: (0, 0))],
        out_specs=pl.BlockSpec((1, m, n), lambda i: (i, 0, 0)),
        out_shape=jax.ShapeDtypeStruct((b, m, n), BF),
        compiler_params=_cparams(("parallel",)),
        name="mem_kv",
    )(mem, g, w)


def _xattn_body(h_ref, g_ref, kv_ref, wq_ref, wo_ref, o_ref):
    x = h_ref[0]
    u = _rms(x, g_ref[...]).astype(BF)
    q = _dot(u, wq_ref[...]).astype(BF)
    kv = kv_ref[0]
    width = X_HEADS * X_HEAD_DIM
    scale = X_HEAD_DIM ** -0.5
    outs = []
    for h in range(X_HEADS):
        cols = slice(h * X_HEAD_DIM, (h + 1) * X_HEAD_DIM)
        s = _dot_t(q[:, cols], kv[:, cols]) * scale
        m = jnp.max(s, axis=-1, keepdims=True)
        e = jnp.exp(s - m)
        p = e / jnp.sum(e, axis=-1, keepdims=True)
        outs.append(_dot(p.astype(BF), kv[:, width + h * X_HEAD_DIM:width + (h + 1) * X_HEAD_DIM]))
    o = jnp.concatenate(outs, axis=-1).astype(BF)
    o_ref[0] = x + _dot(o, wo_ref[...])


def _xattn(h3, g, kv, wq, wo, tm=512):
    b, s, d = h3.shape
    tm = min(tm, s)
    m = kv.shape[1]
    return pl.pallas_call(
        _xattn_body,
        grid=(b, s // tm),
        in_specs=[pl.BlockSpec((1, tm, d), lambda bi, i: (bi, i, 0)),
                  pl.BlockSpec((1, d), lambda bi, i: (0, 0)),
                  pl.BlockSpec((1, m, kv.shape[2]), lambda bi, i: (bi, 0, 0)),
                  pl.BlockSpec(wq.shape, lambda bi, i: (0, 0)),
                  pl.BlockSpec(wo.shape, lambda bi, i: (0, 0))],
        out_specs=pl.BlockSpec((1, tm, d), lambda bi, i: (bi, i, 0)),
        out_shape=jax.ShapeDtypeStruct((b, s, d), F32),
        compiler_params=_cparams(("parallel", "parallel")),
        name="xattn",
    )(h3, g, kv, wq, wo)


def _rel_bucket(dist):
    exact = REL_BUCKETS // 2
    d = jnp.maximum(dist, exact).astype(F32)
    large = exact + (jnp.log(d / exact) / math.log(REL_MAX_DIST / exact)
                     * (REL_BUCKETS - exact)).astype(jnp.int32)
    return jnp.where(dist < exact, dist, jnp.minimum(large, REL_BUCKETS - 1))


def _band_bias(rel_bias, group, dilation):
    heads = DIL_WIDTH // HEAD_DIM
    qi = jnp.arange(BAND)
    kj = jnp.arange(2 * BAND)
    steps = (qi[:, None] + BAND) - kj[None, :]
    table = rel_bias[:, group * heads:(group + 1) * heads].astype(F32)
    bucket = _rel_bucket(jnp.clip(steps, 0, BAND) * dilation)
    onehot = (bucket[None] == jnp.arange(REL_BUCKETS)[:, None, None]).astype(F32)
    return jnp.einsum('bh,bqk->hqk', table, onehot, precision=lax.Precision.HIGHEST)


def _swap_halves(w, width):
    k, n = w.shape
    w = w.reshape(k, n // width, 2, width // 2)
    return w[:, :, ::-1, :].reshape(k, n)


def _pack_w_in(w_in):
    d = w_in.shape[0]
    sb = w_in[:, :1536]
    cq = w_in[:, 1536:1920]
    ckv = w_in[:, 1920:2048]
    kr = w_in[:, 2048:2080]
    zeros = jnp.zeros((d, 64), w_in.dtype)
    kr_sw = _swap_halves(kr, MLA_ROPE)
    seg_a = jnp.concatenate([sb, cq, ckv, kr, kr, zeros, kr_sw, kr_sw, zeros], axis=1)
    group = 3 * DIL_WIDTH
    seg_d = [w_in[:, 2080 + g * group:2080 + (g + 1) * group].astype(BF) for g in range(len(DIL_GROUPS))]
    seg_g = w_in[:, 2080 + len(DIL_GROUPS) * group:]
    return seg_a.astype(BF), seg_d, seg_g.astype(BF)


def _pack_w_uq(w_uq):
    k = w_uq.shape[0]
    w = w_uq.reshape(k, MLA_HEADS, MLA_NOPE + MLA_ROPE)
    nope = w[:, :, :MLA_NOPE].reshape(k, MLA_HEADS * MLA_NOPE)
    rope = w[:, :, MLA_NOPE:]

    def pair_layout(rp):
        rp = rp.reshape(k, MLA_HEADS // 2, 2 * MLA_ROPE)
        pad = jnp.zeros((k, MLA_HEADS // 2, 128 - 2 * MLA_ROPE), rp.dtype)
        return jnp.concatenate([rp, pad], axis=-1).reshape(k, (MLA_HEADS // 2) * 128)

    rope_sw = rope.reshape(k, MLA_HEADS, 2, MLA_ROPE // 2)[:, :, ::-1, :].reshape(k, MLA_HEADS, MLA_ROPE)
    return jnp.concatenate([nope, pair_layout(rope), pair_layout(rope_sw)], axis=1).astype(BF)


def _pack_w_ukv(w_ukv):
    k = w_ukv.shape[0]
    w = w_ukv.reshape(k, MLA_HEADS, MLA_NOPE + MLA_V)
    return jnp.concatenate([w[:, :, :MLA_NOPE].reshape(k, -1), w[:, :, MLA_NOPE:].reshape(k, -1)],
                           axis=1).astype(BF)


def _rope_tables(seq):
    half = MLA_ROPE // 2
    freqs = ROPE_THETA ** (-jnp.arange(half, dtype=F32) / half)
    ang = jnp.arange(seq).astype(F32)[:, None] * freqs[None, :]
    cos, sin = jnp.cos(ang), jnp.sin(ang)
    pad = jnp.zeros((seq, 128 - 2 * MLA_ROPE), F32)
    cos_t = jnp.concatenate([cos, cos, cos, cos, pad], axis=1)
    sin_t = jnp.concatenate([-sin, sin, -sin, sin, pad], axis=1)
    return cos_t, sin_t


def kernel(x, mem, ffn1_norm, ffn1_w_gate, ffn1_w_up, ffn1_w_down, mix_norm, w_in, gate_bias, mla_q_norm, mla_w_uq, mla_kv_norm, mla_w_ukv, w_branch_a, w_branch_b, w_branch_c, w_mix_out, rel_bias, xattn_norm, mem_norm, xattn_w_q, xattn_w_kv, xattn_w_o, ffn2_norm, ffn2_w_gate, ffn2_w_up, ffn2_w_down, final_norm):
    b, s, d = x.shape
    t = b * s
    depth = w_in.shape[0]
    cos_t, sin_t = _rope_tables(s)
    biases = [_band_bias(rel_bias, g, dil) for g, (_, dil) in enumerate(DIL_GROUPS)]
    row = lambda v: v.reshape(1, -1)
    bf = lambda w: w.astype(BF)
    fg = row(final_norm)

    h = x.reshape(t, d)
    for l in range(depth):
        h = _ffn(h, row(ffn1_norm[l]), bf(ffn1_w_gate[l]), bf(ffn1_w_up[l]), bf(ffn1_w_down[l]), fg, False)

        w_a, w_d, w_g = _pack_w_in(w_in[l])
        g_mix = row(mix_norm[l])
        h3 = h.reshape(b, s, d)
        seg_a = _norm_proj(h, g_mix, w_a, tn=w_a.shape[1])
        seg_g = _norm_proj(h, g_mix, w_g, tn=w_g.shape[1] // 2)

        a3 = seg_a.reshape(b, s, seg_a.shape[1])
        o_a = _sb_attention(a3)
        qp, kp, v = _mla_prep(a3, cos_t, sin_t, row(mla_q_norm[l]), row(mla_kv_norm[l]),
                              _pack_w_uq(mla_w_uq[l]), _pack_w_ukv(mla_w_ukv[l]))
        o_b = _mla_attention(qp, kp, v)
        ocs, lses = [], []
        for g, (_, dil) in enumerate(DIL_GROUPS):
            o, lse = _dil_attention(_dil_proj(h3, g_mix, w_d[g], dil), biases[g], dil)
            ocs.append(o)
            lses.append(lse)
        h3 = _merge(h3, o_a, o_b, ocs, lses, seg_g.reshape(b, s, seg_g.shape[1]), gate_bias[l],
                    bf(w_branch_a[l]), bf(w_branch_b[l]), bf(w_branch_c[l]), bf(w_mix_out[l]))

        kv = _mem_kv(mem, row(mem_norm[l]), bf(xattn_w_kv[l]))
        h = _xattn(h3, row(xattn_norm[l]), kv, bf(xattn_w_q[l]), bf(xattn_w_o[l])).reshape(t, d)

        h = _ffn(h, row(ffn2_norm[l]), bf(ffn2_w_gate[l]), bf(ffn2_w_up[l]), bf(ffn2_w_down[l]), fg,
                 l == depth - 1)
    return h.reshape(b, s, d)
```

```python
import functools
import math

import jax
import jax.numpy as jnp
import numpy as np
from jax import lax
from jax.experimental import pallas as pl
from jax.experimental.pallas import tpu as pltpu

BF = jnp.bfloat16
F32 = jnp.float32

EPS = 1e-6
NEG = -1e30
FFN_RESIDUAL = 0.5

SB_WIDTH = 512
MLA_HEADS = 8
MLA_Q_RANK = 384
MLA_KV_RANK = 128
MLA_NOPE = 64
MLA_ROPE = 32
MLA_V = 64
ROPE_THETA = 10000.0
DIL_GROUPS = ((128, 1), (512, 4), (2048, 16))
DIL_WIDTH = 512
REL_BUCKETS = 32
REL_MAX_DIST = 2048
X_HEADS = 4
X_HEAD_DIM = 128
HEAD_DIM = 64
BAND = 128
LANES = 128
SB_DEAD_LOG2 = -150.0
LOG2E = math.log2(math.e)
LN2 = math.log(2.0)

V7X_VMEM_LIMIT = 48 * 1024 * 1024
FF_CHUNK = 256


def _cparams(sem):
    return pltpu.CompilerParams(dimension_semantics=sem, vmem_limit_bytes=V7X_VMEM_LIMIT)


def _rms(x, g):
    ms = jnp.mean(x * x, axis=-1, keepdims=True)
    return x * lax.rsqrt(ms + EPS) * g


def _dot(a, b):
    return jnp.dot(a, b, preferred_element_type=F32)


def _dot_t(a, b):
    return lax.dot_general(a, b, (((1,), (1,)), ((), ())), preferred_element_type=F32)


def _resident(shape):
    nd = len(shape)
    return pl.BlockSpec(shape, lambda *_: (0,) * nd, pipeline_mode=pl.Buffered(1))


def _ffn_body(h_ref, g_ref, wg_ref, wu_ref, wd_ref, fg_ref, o_ref, act_ref, *, final):
    x = h_ref[...]
    u = _rms(x, g_ref[...]).astype(BF)
    d_ff = wg_ref.shape[1]
    for c in range(d_ff // FF_CHUNK):
        sl = slice(c * FF_CHUNK, (c + 1) * FF_CHUNK)
        a = _dot(u, wg_ref[:, sl])
        b = _dot(u, wu_ref[:, sl])
        act_ref[:, sl] = (a * jax.nn.sigmoid(a) * b).astype(BF)
    y = x + FFN_RESIDUAL * _dot(act_ref[...], wd_ref[...])
    if final:
        y = _rms(y, fg_ref[...])
    o_ref[...] = y


def _ffn(h, g, wg, wu, wd, fg, final, tm=512):
    t, d = h.shape
    d_ff = wg.shape[1]
    tm = min(tm, t)
    return pl.pallas_call(
        functools.partial(_ffn_body, final=final),
        grid=(t // tm,),
        in_specs=[
            pl.BlockSpec((tm, d), lambda i: (i, 0)),
            _resident((1, d)),
            _resident((d, d_ff)),
            _resident((d, d_ff)),
            _resident((d_ff, d)),
            _resident((1, d)),
        ],
        out_specs=pl.BlockSpec((tm, d), lambda i: (i, 0)),
        out_shape=jax.ShapeDtypeStruct((t, d), F32),
        scratch_shapes=[pltpu.VMEM((tm, d_ff), BF)],
        compiler_params=_cparams(("parallel",)),
        name="ffn",
    )(h, g, wg, wu, wd, fg)


def _proj_body(h_ref, g_ref, w_ref, o_ref, u_ref):
    @pl.when(pl.program_id(1) == 0)
    def _():
        u_ref[...] = _rms(h_ref[...], g_ref[...]).astype(BF)

    o_ref[...] = _dot(u_ref[...], w_ref[...]).astype(o_ref.dtype)


def _norm_proj(h, g, w, tn, tm=512):
    t, d = h.shape
    n = w.shape[1]
    tm = min(tm, t)
    return pl.pallas_call(
        _proj_body,
        grid=(t // tm, n // tn),
        in_specs=[
            pl.BlockSpec((tm, d), lambda i, j: (i, 0)),
            pl.BlockSpec((1, d), lambda i, j: (0, 0)),
            pl.BlockSpec((d, tn), lambda i, j: (0, j)),
        ],
        out_specs=pl.BlockSpec((tm, tn), lambda i, j: (i, j)),
        out_shape=jax.ShapeDtypeStruct((t, n), BF),
        scratch_shapes=[pltpu.VMEM((tm, d), BF)],
        compiler_params=_cparams(("parallel", "arbitrary")),
        name="norm_proj",
    )(h, g, w)


def _dil_proj_body(h_ref, g_ref, w_ref, o_ref, *stage, r):
    rows = h_ref.shape[1] // r
    n = w_ref.shape[1]
    u = _rms(h_ref[0], g_ref[...])
    if r == 1:
        o_ref[0] = _dot(u.astype(BF), w_ref[...]).astype(o_ref.dtype)
        return
    (stage,) = stage
    chunks = u.shape[1] // LANES
    for c in range(chunks):
        stage[c] = u[:, c * LANES:(c + 1) * LANES]
    for res in range(r):
        ur = jnp.concatenate([stage[c, pl.ds(res, rows, stride=r), :] for c in range(chunks)], axis=1)
        o_ref[0, :, res * n:(res + 1) * n] = _dot(ur.astype(BF), w_ref[...]).astype(o_ref.dtype)


def _dil_proj(h3, g, w, r):
    b, s, d = h3.shape
    n = w.shape[1]
    tm = min(s, max(1024, BAND * r) if r > 1 else 512)
    return pl.pallas_call(
        functools.partial(_dil_proj_body, r=r),
        grid=(b, s // tm),
        in_specs=[
            pl.BlockSpec((1, tm, d), lambda bi, i: (bi, i, 0)),
            pl.BlockSpec((1, d), lambda bi, i: (0, 0)),
            pl.BlockSpec((d, n), lambda bi, i: (0, 0)),
        ],
        out_specs=pl.BlockSpec((1, tm // r, r * n), lambda bi, i: (bi, i, 0)),
        out_shape=jax.ShapeDtypeStruct((b, s // r, r * n), BF),
        scratch_shapes=[pltpu.VMEM((d // LANES, tm, LANES), F32)] if r > 1 else [],
        compiler_params=_cparams(("parallel", "parallel")),
        name=f"dil_proj_r{r}",
    )(h3, g, w)


def _sb_body(q_ref, k_ref, v_ref, o_ref, *, tq, pairs):
    i = pl.program_id(2)
    pw = 2 * HEAD_DIM
    lane = lax.broadcasted_iota(jnp.int32, (1, pw), 1)
    first = lane < HEAD_DIM
    q_heads = []
    for p in range(pairs):
        q = q_ref[0, :, p * pw:(p + 1) * pw]
        zero = jnp.zeros_like(q)
        q_heads += [jnp.where(first, q, zero), jnp.where(first, zero, q)]
    row = lax.broadcasted_iota(jnp.int32, (tq, tq), 0)
    col = lax.broadcasted_iota(jnp.int32, (tq, tq), 1)
    strict = col < row
    later = jnp.where(row > col, 1.0, 0.0).astype(BF)

    def block(j, accs, rs, diag):
        start = pl.multiple_of(j * tq, tq)
        heads = range(2 * pairs)
        ks = [k_ref[0, pl.ds(start, tq), p * pw:(p + 1) * pw] for p in range(pairs)]
        vs = [v_ref[0, pl.ds(start, tq), p * pw:(p + 1) * pw] for p in range(pairs)]
        zs = [_dot_t(q_heads[h], ks[h // 2]) for h in heads]
        log_betas, log_keeps, his, los = [], [], [], []
        for z in zs:
            sp = jnp.log(1.0 + jnp.exp2(-jnp.abs(z))) * LOG2E
            log_beta = jnp.minimum(z, 0.0) - sp
            log_keep = log_beta - z
            if diag:
                log_keep = jnp.where(strict, log_keep, 0.0)
            hi = log_keep.astype(BF)
            log_betas.append(log_beta)
            log_keeps.append(log_keep)
            his.append(hi)
            los.append((log_keep - hi.astype(F32)).astype(BF))
        afters = [_dot(his[h], later) + _dot(los[h], later) for h in heads]
        ws = []
        for h in heads:
            w = jnp.exp2(log_betas[h] + afters[h] + rs[h])
            if diag:
                w = jnp.where(strict, w, 0.0)
            ws.append(w.astype(BF))
        outs = [_dot(ws[h], vs[h // 2]) for h in heads]
        new_rs = [rs[h] + jnp.sum(log_keeps[h], axis=-1, keepdims=True) for h in heads]
        new_accs = [accs[p] + jnp.where(first, outs[2 * p], outs[2 * p + 1]) for p in range(pairs)]
        return tuple(new_accs), tuple(new_rs)

    def remaining(rs):
        return jnp.max(functools.reduce(jnp.maximum, rs))

    def cond(c):
        return (c[0] < i) & (c[1] > SB_DEAD_LOG2)

    def body(c):
        accs, rs = block(i - 1 - c[0], c[2], c[3], False)
        return c[0] + 1, remaining(rs), accs, rs

    accs = tuple(jnp.zeros((tq, pw), F32) for _ in range(pairs))
    rs = tuple(jnp.zeros((tq, 1), F32) for _ in range(2 * pairs))
    accs, rs = block(i, accs, rs, True)
    out = lax.while_loop(cond, body, (jnp.int32(0), remaining(rs), accs, rs))
    for p in range(pairs):
        o_ref[0, :, p * pw:(p + 1) * pw] = out[2][p].astype(o_ref.dtype)


def _sb_attention(a3, tq=256, pairs=2):
    b, s, _ = a3.shape
    tq = min(tq, s)
    w = 2 * HEAD_DIM * pairs
    groups = SB_WIDTH // w
    return pl.pallas_call(
        functools.partial(_sb_body, tq=tq, pairs=pairs),
        grid=(b, groups, s // tq),
        in_specs=[
            pl.BlockSpec((1, tq, w), lambda bi, p, i: (bi, i, p)),
            pl.BlockSpec((1, s, w), lambda bi, p, i: (bi, 0, groups + p)),
            pl.BlockSpec((1, s, w), lambda bi, p, i: (bi, 0, 2 * groups + p)),
        ],
        out_specs=pl.BlockSpec((1, tq, w), lambda bi, p, i: (bi, i, p)),
        out_shape=jax.ShapeDtypeStruct((b, s, SB_WIDTH), BF),
        compiler_params=_cparams(("parallel", "parallel", "arbitrary")),
        name="sb_attn",
    )(a3, a3, a3)


def _mla_prep_body(cq_ref, ckv_ref, kr_ref, cos_ref, sin_ref, qg_ref, kvg_ref, wq_ref, wkv_ref,
                   qp_ref, kp_ref, v_ref, *, scale):
    nq = _rms(cq_ref[0].astype(F32), qg_ref[...]).astype(BF)
    q = _dot(nq, wq_ref[...])
    nkv = _rms(ckv_ref[0].astype(F32), kvg_ref[...]).astype(BF)
    kv = _dot(nkv, wkv_ref[...])
    cos = cos_ref[...]
    sin = sin_ref[...]
    kr = kr_ref[0].astype(F32)
    k_pe = (kr[:, :128] * cos + kr[:, 128:] * sin).astype(BF)
    width = q.shape[1] // 3
    for p in range(width // 128):
        lo, hi = p * 128, (p + 1) * 128
        qp_ref[0, :, 2 * lo:2 * lo + 128] = (q[:, lo:hi] * scale).astype(BF)
        q_pe = q[:, width + lo:width + hi] * cos + q[:, 2 * width + lo:2 * width + hi] * sin
        qp_ref[0, :, 2 * lo + 128:2 * hi] = (q_pe * scale).astype(BF)
        kp_ref[0, :, 2 * lo:2 * lo + 128] = kv[:, lo:hi].astype(BF)
        kp_ref[0, :, 2 * lo + 128:2 * hi] = k_pe
    v_ref[0] = kv[:, width:].astype(BF)


def _mla_prep(a3, cos, sin, qg, kvg, wq, wkv, tm=512):
    b, s, _ = a3.shape
    tm = min(tm, s)
    width = MLA_HEADS * MLA_NOPE
    scale = (MLA_NOPE + MLA_ROPE) ** -0.5 * LOG2E
    out_sds = lambda n: jax.ShapeDtypeStruct((b, s, n), BF)
    return pl.pallas_call(
        functools.partial(_mla_prep_body, scale=scale),
        grid=(b, s // tm),
        in_specs=[
            pl.BlockSpec((1, tm, MLA_Q_RANK), lambda bi, i: (bi, i, 1536 // MLA_Q_RANK)),
            pl.BlockSpec((1, tm, MLA_KV_RANK), lambda bi, i: (bi, i, 1920 // MLA_KV_RANK)),
            pl.BlockSpec((1, tm, 256), lambda bi, i: (bi, i, 2048 // 256)),
            pl.BlockSpec((tm, 128), lambda bi, i: (i, 0)),
            pl.BlockSpec((tm, 128), lambda bi, i: (i, 0)),
            pl.BlockSpec((1, MLA_Q_RANK), lambda bi, i: (0, 0)),
            pl.BlockSpec((1, MLA_KV_RANK), lambda bi, i: (0, 0)),
            pl.BlockSpec(wq.shape, lambda bi, i: (0, 0)),
            pl.BlockSpec(wkv.shape, lambda bi, i: (0, 0)),
        ],
        out_specs=[
            pl.BlockSpec((1, tm, 2 * width), lambda bi, i: (bi, i, 0)),
            pl.BlockSpec((1, tm, 2 * width), lambda bi, i: (bi, i, 0)),
            pl.BlockSpec((1, tm, width), lambda bi, i: (bi, i, 0)),
        ],
        out_shape=[out_sds(2 * width), out_sds(2 * width), out_sds(width)],
        compiler_params=_cparams(("parallel", "parallel")),
        name="mla_prep",
    )(a3, a3, a3, cos, sin, qg, kvg, wq, wkv)


def _mla_body(q_ref, k_ref, v_ref, o_ref, *, tq, pairs):
    i = pl.program_id(2)
    lane = lax.broadcasted_iota(jnp.int32, (1, 256), 1)
    sel0 = (lane < 64) | ((lane >= 128) & (lane < 160))
    sel1 = ((lane >= 64) & (lane < 128)) | ((lane >= 160) & (lane < 192))
    q_heads = []
    for p in range(pairs):
        q = q_ref[0, :, p * 256:(p + 1) * 256]
        zero = jnp.zeros_like(q)
        q_heads += [jnp.where(sel0, q, zero), jnp.where(sel1, q, zero)]
    row = lax.broadcasted_iota(jnp.int32, (tq, tq), 0)
    col = lax.broadcasted_iota(jnp.int32, (tq, tq), 1)
    causal = col <= row

    first = lax.broadcasted_iota(jnp.int32, (1, 128), 1) < 64

    def block(j, carry, diag):
        start = pl.multiple_of(j * tq, tq)
        heads = range(2 * pairs)
        ks = [k_ref[0, pl.ds(start, tq), p * 256:(p + 1) * 256] for p in range(pairs)]
        v_heads = []
        for p in range(pairs):
            vj = v_ref[0, pl.ds(start, tq), p * 128:(p + 1) * 128]
            ones = jnp.ones_like(vj)
            v_heads += [jnp.where(first, vj, ones), jnp.where(first, ones, vj)]
        ss = [_dot_t(q_heads[h], ks[h // 2]) for h in heads]
        if diag:
            ss = [jnp.where(causal, s, NEG) for s in ss]
        m_news = [jnp.maximum(carry[h][0], jnp.max(ss[h], axis=-1, keepdims=True)) for h in heads]
        prs = [jnp.exp2(ss[h] - m_news[h]).astype(BF) for h in heads]
        pvs = [_dot(prs[h], v_heads[h]) for h in heads]
        return tuple((m_news[h], jnp.exp2(carry[h][0] - m_news[h]) * carry[h][1] + pvs[h]) for h in heads)

    one = (jnp.full((tq, 1), NEG, F32), jnp.zeros((tq, 128), F32))
    carry = lax.fori_loop(0, i, lambda j, c: block(j, c, False), (one,) * (2 * pairs))
    carry = block(i, carry, True)
    for p in range(pairs):
        a0, a1 = carry[2 * p][1], carry[2 * p + 1][1]
        l = pltpu.roll(jnp.where(first, a1, a0), 64, 1)
        o_ref[0, :, p * 128:(p + 1) * 128] = (jnp.where(first, a0, a1) / l).astype(o_ref.dtype)


def _mla_attention(qp, kp, v, tq=512, pairs=2):
    b, s, _ = qp.shape
    tq = min(tq, s)
    groups = MLA_HEADS // (2 * pairs)
    return pl.pallas_call(
        functools.partial(_mla_body, tq=tq, pairs=pairs),
        grid=(b, groups, s // tq),
        in_specs=[
            pl.BlockSpec((1, tq, 256 * pairs), lambda bi, p, i: (bi, i, p)),
            pl.BlockSpec((1, s, 256 * pairs), lambda bi, p, i: (bi, 0, p)),
            pl.BlockSpec((1, s, 128 * pairs), lambda bi, p, i: (bi, 0, p)),
        ],
        out_specs=pl.BlockSpec((1, tq, 128 * pairs), lambda bi, p, i: (bi, i, p)),
        out_shape=jax.ShapeDtypeStruct((b, s, MLA_HEADS * MLA_V), BF),
        compiler_params=_cparams(("parallel", "parallel", "arbitrary")),
        name="mla_attn",
    )(qp, kp, v)


def _dil_body(q_ref, kc_ref, kp_ref, vc_ref, vp_ref, bias_ref, o_ref, lse_ref, *, tl):
    j = pl.program_id(2)
    lane = lax.broadcasted_iota(jnp.int32, (1, 2 * HEAD_DIM), 1)
    first = lane < HEAD_DIM
    qi = lax.broadcasted_iota(jnp.int32, (BAND, 2 * BAND), 0)
    kj = lax.broadcasted_iota(jnp.int32, (BAND, 2 * BAND), 1)
    steps = qi + BAND - kj
    in_window = (steps >= 0) & (steps <= BAND)
    in_window_first = in_window & ((kj >= BAND) | (j > 0))
    pw = 2 * HEAD_DIM
    pairs = DIL_WIDTH // pw
    heads = range(2 * pairs)
    for sb in range(tl // BAND):
        rows = slice(sb * BAND, (sb + 1) * BAND)
        prev = slice((sb - 1) * BAND, sb * BAND)
        k_prev = kp_ref[0] if sb == 0 else kc_ref[0, prev, :]
        v_prev = vp_ref[0] if sb == 0 else vc_ref[0, prev, :]
        kcat = jnp.concatenate([k_prev, kc_ref[0, rows, :]], axis=0)
        vcat = jnp.concatenate([v_prev, vc_ref[0, rows, :]], axis=0)
        valid = in_window_first if sb == 0 else in_window
        qs = q_ref[0, rows, :]
        q_heads, v_heads = [], []
        for p in range(pairs):
            qb = qs[:, p * pw:(p + 1) * pw]
            vb = vcat[:, p * pw:(p + 1) * pw]
            zero, ones = jnp.zeros_like(qb), jnp.ones_like(vb)
            q_heads += [jnp.where(first, qb, zero), jnp.where(first, zero, qb)]
            v_heads += [jnp.where(first, vb, ones), jnp.where(first, ones, vb)]
        ss = [_dot_t(q_heads[h], kcat[:, (h // 2) * pw:(h // 2 + 1) * pw]) + bias_ref[h] for h in heads]
        ss = [jnp.where(valid, s, NEG) for s in ss]
        ms = [jnp.max(s, axis=-1, keepdims=True) for s in ss]
        prs = [jnp.exp2(ss[h] - ms[h]).astype(BF) for h in heads]
        pvs = [_dot(prs[h], v_heads[h]) for h in heads]
        for p in range(pairs):
            a0, a1 = pvs[2 * p], pvs[2 * p + 1]
            cols = slice(p * pw, (p + 1) * pw)
            l = pltpu.roll(jnp.where(first, a1, a0), HEAD_DIM, 1)
            o_ref[0, rows, cols] = (jnp.where(first, a0, a1) / l).astype(o_ref.dtype)
            lse_ref[0, rows, cols] = jnp.where(first, ms[2 * p], ms[2 * p + 1]) * LN2 + jnp.log(l)


def _dil_attention(view, bias, dilation, tl=512):
    r = dilation
    batch, length, _ = view.shape
    tl = min(tl, length)
    sub = tl // BAND

    def cur(c):
        return pl.BlockSpec((1, tl, DIL_WIDTH), lambda bi, res, j: (bi, j, res * 3 + c))

    def prev(c):
        return pl.BlockSpec((1, BAND, DIL_WIDTH),
                            lambda bi, res, j: (bi, jnp.maximum(j * sub - 1, 0), res * 3 + c))

    out_spec = pl.BlockSpec((1, tl, DIL_WIDTH), lambda bi, res, j: (bi, j, res))
    return pl.pallas_call(
        functools.partial(_dil_body, tl=tl),
        grid=(batch, r, length // tl),
        in_specs=[cur(0), cur(1), prev(1), cur(2), prev(2),
                  pl.BlockSpec(bias.shape, lambda bi, res, j: (0, 0, 0))],
        out_specs=[out_spec, out_spec],
        out_shape=[jax.ShapeDtypeStruct((batch, length, r * DIL_WIDTH), BF),
                   jax.ShapeDtypeStruct((batch, length, r * DIL_WIDTH), F32)],
        compiler_params=_cparams(("parallel", "parallel", "arbitrary")),
        name=f"dil_attn_r{r}",
    )(view, view, view, view, view, bias)


def _merge_body(h_ref, oa_ref, ob_ref, oc0_ref, oc1_ref, oc2_ref, l0_ref, l1_ref, l2_ref,
                ga_ref, gb_ref, gc_ref, gbias_ref, wa_ref, wb_ref, wc_ref, wo_ref, o_ref, *scratch):
    tm = h_ref.shape[1]
    scratch = list(scratch)

    def token_major(ref, r):
        if r == 1:
            return ref[0].astype(F32)
        buf = scratch.pop(0)
        chunks = DIL_WIDTH // LANES
        for res in range(r):
            for c in range(chunks):
                lo = res * DIL_WIDTH + c * LANES
                buf[c, pl.ds(res, tm // r, stride=r), :] = ref[0, :, lo:lo + LANES].astype(F32)
        return jnp.concatenate([buf[c] for c in range(chunks)], axis=1)

    dils = [dil for _, dil in DIL_GROUPS]
    l0, l1, l2 = (token_major(ref, r) for ref, r in zip((l0_ref, l1_ref, l2_ref), dils))
    mx = jnp.maximum(jnp.maximum(l0, l1), l2)
    e0, e1, e2 = jnp.exp(l0 - mx), jnp.exp(l1 - mx), jnp.exp(l2 - mx)
    o0, o1, o2 = (token_major(ref, r) for ref, r in zip((oc0_ref, oc1_ref, oc2_ref), dils))
    oc = (e0 * o0 + e1 * o1 + e2 * o2) / (e0 + e1 + e2)
    gbias = gbias_ref[...]
    merged = (jax.nn.sigmoid(ga_ref[0].astype(F32) + gbias[0:1]) * _dot(oa_ref[0], wa_ref[...])
              + jax.nn.sigmoid(gb_ref[0].astype(F32) + gbias[1:2]) * _dot(ob_ref[0], wb_ref[...])
              + jax.nn.sigmoid(gc_ref[0].astype(F32) + gbias[2:3]) * _dot(oc.astype(BF), wc_ref[...]))
    o_ref[0] = h_ref[0] + _dot(merged.astype(BF), wo_ref[...])


def _merge(h3, oa, ob, ocs, lses, gates, gbias, wa, wb, wc, wo, tm=512):
    b, s, d = h3.shape
    tm = min(tm, s)
    row = lambda n: pl.BlockSpec((1, tm, n), lambda bi, i: (bi, i, 0))
    gate = lambda c: pl.BlockSpec((1, tm, d), lambda bi, i: (bi, i, c))
    views = [pl.BlockSpec((1, tm // r, r * DIL_WIDTH), lambda bi, i: (bi, i, 0)) for _, r in DIL_GROUPS]
    n_buf = 2 * sum(1 for _, r in DIL_GROUPS if r > 1)
    return pl.pallas_call(
        _merge_body,
        grid=(b, s // tm),
        in_specs=[row(d), row(SB_WIDTH), row(SB_WIDTH)] + views + views
        + [gate(0), gate(1), gate(2), _resident(gbias.shape),
           _resident(wa.shape), _resident(wb.shape), _resident(wc.shape), _resident(wo.shape)],
        out_specs=row(d),
        out_shape=jax.ShapeDtypeStruct((b, s, d), F32),
        scratch_shapes=[pltpu.VMEM((DIL_WIDTH // LANES, tm, LANES), F32)] * n_buf,
        compiler_params=_cparams(("parallel", "parallel")),
        name="merge",
    )(h3, oa, ob, *ocs, *lses, gates, gates, gates, gbias, wa, wb, wc, wo)


def _memkv_body(m_ref, g_ref, w_ref, o_ref):
    u = _rms(m_ref[0], g_ref[...]).astype(BF)
    o_ref[0] = _dot(u, w_ref[...]).astype(o_ref.dtype)


def _mem_kv(mem, g, w):
    b, m, d = mem.shape
    n = w.shape[1]
    return pl.pallas_call(
        _memkv_body,
        grid=(b,),
        in_specs=[pl.BlockSpec((1, m, d), lambda i: (i, 0, 0)),
                  pl.BlockSpec((1, d), lambda i: (0, 0)),
                  pl.BlockSpec((d, n), lambda i: (0, 0))],
        out_specs=pl.BlockSpec((1, m, n), lambda i: (i, 0, 0)),
        out_shape=jax.ShapeDtypeStruct((b, m, n), BF),
        compiler_params=_cparams(("parallel",)),
        name="mem_kv",
    )(mem, g, w)


def _xattn_body(h_ref, g_ref, kv_ref, wq_ref, wo_ref, o_ref):
    x = h_ref[0]
    u = _rms(x, g_ref[...]).astype(BF)
    q = _dot(u, wq_ref[...]).astype(BF)
    kv = kv_ref[0]
    width = X_HEADS * X_HEAD_DIM
    scale = X_HEAD_DIM ** -0.5
    outs = []
    for h in range(X_HEADS):
        cols = slice(h * X_HEAD_DIM, (h + 1) * X_HEAD_DIM)
        s = _dot_t(q[:, cols], kv[:, cols]) * scale
        m = jnp.max(s, axis=-1, keepdims=True)
        e = jnp.exp(s - m)
        p = e / jnp.sum(e, axis=-1, keepdims=True)
        outs.append(_dot(p.astype(BF), kv[:, width + h * X_HEAD_DIM:width + (h + 1) * X_HEAD_DIM]))
    o = jnp.concatenate(outs, axis=-1).astype(BF)
    o_ref[0] = x + _dot(o, wo_ref[...])


def _xattn(h3, g, kv, wq, wo, tm=512):
    b, s, d = h3.shape
    tm = min(tm, s)
    m = kv.shape[1]
    return pl.pallas_call(
        _xattn_body,
        grid=(b, s // tm),
        in_specs=[pl.BlockSpec((1, tm, d), lambda bi, i: (bi, i, 0)),
                  pl.BlockSpec((1, d), lambda bi, i: (0, 0)),
                  pl.BlockSpec((1, m, kv.shape[2]), lambda bi, i: (bi, 0, 0)),
                  pl.BlockSpec(wq.shape, lambda bi, i: (0, 0)),
                  pl.BlockSpec(wo.shape, lambda bi, i: (0, 0))],
        out_specs=pl.BlockSpec((1, tm, d), lambda bi, i: (bi, i, 0)),
        out_shape=jax.ShapeDtypeStruct((b, s, d), F32),
        compiler_params=_cparams(("parallel", "parallel")),
        name="xattn",
    )(h3, g, kv, wq, wo)


def _rel_bucket(dist):
    exact = REL_BUCKETS // 2
    d = jnp.maximum(dist, exact).astype(F32)
    large = exact + (jnp.log(d / exact) / math.log(REL_MAX_DIST / exact)
                     * (REL_BUCKETS - exact)).astype(jnp.int32)
    return jnp.where(dist < exact, dist, jnp.minimum(large, REL_BUCKETS - 1))


def _band_bias(rel_bias, group, dilation):
    heads = DIL_WIDTH // HEAD_DIM
    qi = jnp.arange(BAND)
    kj = jnp.arange(2 * BAND)
    steps = (qi[:, None] + BAND) - kj[None, :]
    table = rel_bias[:, group * heads:(group + 1) * heads].astype(F32)
    bucket = _rel_bucket(jnp.clip(steps, 0, BAND) * dilation)
    onehot = (bucket[None] == jnp.arange(REL_BUCKETS)[:, None, None]).astype(F32)
    return jnp.einsum('bh,bqk->hqk', table * LOG2E, onehot, precision=lax.Precision.HIGHEST)


def _swap_halves(w, width):
    k, n = w.shape
    w = w.reshape(k, n // width, 2, width // 2)
    return w[:, :, ::-1, :].reshape(k, n)


def _pack_w_in(w_in):
    d = w_in.shape[0]
    qscale = HEAD_DIM ** -0.5 * LOG2E
    sb = jnp.concatenate([w_in[:, :SB_WIDTH] * qscale, w_in[:, SB_WIDTH:1536]], axis=1)
    cq = w_in[:, 1536:1920]
    ckv = w_in[:, 1920:2048]
    kr = w_in[:, 2048:2080]
    zeros = jnp.zeros((d, 64), w_in.dtype)
    kr_sw = _swap_halves(kr, MLA_ROPE)
    seg_a = jnp.concatenate([sb, cq, ckv, kr, kr, zeros, kr_sw, kr_sw, zeros], axis=1)
    group = 3 * DIL_WIDTH
    seg_d = []
    for g in range(len(DIL_GROUPS)):
        lo = 2080 + g * group
        seg_d.append(jnp.concatenate([w_in[:, lo:lo + DIL_WIDTH] * qscale, w_in[:, lo + DIL_WIDTH:lo + group]],
                                     axis=1).astype(BF))
    seg_g = w_in[:, 2080 + len(DIL_GROUPS) * group:]
    return seg_a.astype(BF), seg_d, seg_g.astype(BF)


def _pack_w_uq(w_uq):
    k = w_uq.shape[0]
    w = w_uq.reshape(k, MLA_HEADS, MLA_NOPE + MLA_ROPE)
    nope = w[:, :, :MLA_NOPE].reshape(k, MLA_HEADS * MLA_NOPE)
    rope = w[:, :, MLA_NOPE:]

    def pair_layout(rp):
        rp = rp.reshape(k, MLA_HEADS // 2, 2 * MLA_ROPE)
        pad = jnp.zeros((k, MLA_HEADS // 2, 128 - 2 * MLA_ROPE), rp.dtype)
        return jnp.concatenate([rp, pad], axis=-1).reshape(k, (MLA_HEADS // 2) * 128)

    rope_sw = rope.reshape(k, MLA_HEADS, 2, MLA_ROPE // 2)[:, :, ::-1, :].reshape(k, MLA_HEADS, MLA_ROPE)
    return jnp.concatenate([nope, pair_layout(rope), pair_layout(rope_sw)], axis=1).astype(BF)


def _pack_w_ukv(w_ukv):
    k = w_ukv.shape[0]
    w = w_ukv.reshape(k, MLA_HEADS, MLA_NOPE + MLA_V)
    return jnp.concatenate([w[:, :, :MLA_NOPE].reshape(k, -1), w[:, :, MLA_NOPE:].reshape(k, -1)],
                           axis=1).astype(BF)


def _rope_tables(seq):
    half = MLA_ROPE // 2
    freqs = ROPE_THETA ** (-jnp.arange(half, dtype=F32) / half)
    ang = jnp.arange(seq).astype(F32)[:, None] * freqs[None, :]
    cos, sin = jnp.cos(ang), jnp.sin(ang)
    pad = jnp.zeros((seq, 128 - 2 * MLA_ROPE), F32)
    cos_t = jnp.concatenate([cos, cos, cos, cos, pad], axis=1)
    sin_t = jnp.concatenate([-sin, sin, -sin, sin, pad], axis=1)
    return cos_t, sin_t


def kernel(x, mem, ffn1_norm, ffn1_w_gate, ffn1_w_up, ffn1_w_down, mix_norm, w_in, gate_bias, mla_q_norm, mla_w_uq, mla_kv_norm, mla_w_ukv, w_branch_a, w_branch_b, w_branch_c, w_mix_out, rel_bias, xattn_norm, mem_norm, xattn_w_q, xattn_w_kv, xattn_w_o, ffn2_norm, ffn2_w_gate, ffn2_w_up, ffn2_w_down, final_norm):
    b, s, d = x.shape
    t = b * s
    depth = w_in.shape[0]
    cos_t, sin_t = _rope_tables(s)
    biases = [_band_bias(rel_bias, g, dil) for g, (_, dil) in enumerate(DIL_GROUPS)]
    row = lambda v: v.reshape(1, -1)
    bf = lambda w: w.astype(BF)
    fg = row(final_norm)

    h = x.reshape(t, d)
    for l in range(depth):
        h = _ffn(h, row(ffn1_norm[l]), bf(ffn1_w_gate[l]), bf(ffn1_w_up[l]), bf(ffn1_w_down[l]), fg, False)

        w_a, w_d, w_g = _pack_w_in(w_in[l])
        g_mix = row(mix_norm[l])
        h3 = h.reshape(b, s, d)
        seg_a = _norm_proj(h, g_mix, w_a, tn=w_a.shape[1])
        seg_g = _norm_proj(h, g_mix, w_g, tn=w_g.shape[1] // 2)

        a3 = seg_a.reshape(b, s, seg_a.shape[1])
        o_a = _sb_attention(a3)
        qp, kp, v = _mla_prep(a3, cos_t, sin_t, row(mla_q_norm[l]), row(mla_kv_norm[l]),
                              _pack_w_uq(mla_w_uq[l]), _pack_w_ukv(mla_w_ukv[l]))
        o_b = _mla_attention(qp, kp, v)
        ocs, lses = [], []
        for g, (_, dil) in enumerate(DIL_GROUPS):
            o, lse = _dil_attention(_dil_proj(h3, g_mix, w_d[g], dil), biases[g], dil)
            ocs.append(o)
            lses.append(lse)
        h3 = _merge(h3, o_a, o_b, ocs, lses, seg_g.reshape(b, s, seg_g.shape[1]), gate_bias[l],
                    bf(w_branch_a[l]), bf(w_branch_b[l]), bf(w_branch_c[l]), bf(w_mix_out[l]))

        kv = _mem_kv(mem, row(mem_norm[l]), bf(xattn_w_kv[l]))
        h = _xattn(h3, row(xattn_norm[l]), kv, bf(xattn_w_q[l]), bf(xattn_w_o[l])).reshape(t, d)

        h = _ffn(h, row(ffn2_norm[l]), bf(ffn2_w_gate[l]), bf(ffn2_w_up[l]), bf(ffn2_w_down[l]), fg,
                 l == depth - 1)
    return h.reshape(b, s, d)
```

```python
import functools
import math

import jax
import jax.numpy as jnp
import numpy as np
from jax import lax
from jax.experimental import pallas as pl
from jax.experimental.pallas import tpu as pltpu

BF = jnp.bfloat16
F32 = jnp.float32

EPS = 1e-6
NEG = -1e30
FFN_RESIDUAL = 0.5

SB_WIDTH = 512
MLA_HEADS = 8
MLA_Q_RANK = 384
MLA_KV_RANK = 128
MLA_NOPE = 64
MLA_ROPE = 32
MLA_V = 64
ROPE_THETA = 10000.0
DIL_GROUPS = ((128, 1), (512, 4), (2048, 16))
DIL_WIDTH = 512
REL_BUCKETS = 32
REL_MAX_DIST = 2048
X_HEADS = 4
X_HEAD_DIM = 128
HEAD_DIM = 64
BAND = 128
LANES = 128
SB_DEAD_LOG2 = -150.0
LOG2E = math.log2(math.e)
LN2 = math.log(2.0)

V7X_VMEM_LIMIT = 48 * 1024 * 1024
FF_CHUNK = 256


def _cparams(sem):
    return pltpu.CompilerParams(dimension_semantics=sem, vmem_limit_bytes=V7X_VMEM_LIMIT)


def _rms(x, g):
    ms = jnp.mean(x * x, axis=-1, keepdims=True)
    return x * lax.rsqrt(ms + EPS) * g


def _dot(a, b):
    return jnp.dot(a, b, preferred_element_type=F32)


def _dot_t(a, b):
    return lax.dot_general(a, b, (((1,), (1,)), ((), ())), preferred_element_type=F32)


def _resident(shape):
    nd = len(shape)
    return pl.BlockSpec(shape, lambda *_: (0,) * nd, pipeline_mode=pl.Buffered(1))


def _ffn_body(h_ref, g_ref, wg_ref, wu_ref, wd_ref, fg_ref, o_ref, act_ref, *, final):
    x = h_ref[...]
    u = _rms(x, g_ref[...]).astype(BF)
    d_ff = wg_ref.shape[1]
    for c in range(d_ff // FF_CHUNK):
        sl = slice(c * FF_CHUNK, (c + 1) * FF_CHUNK)
        a = _dot(u, wg_ref[:, sl])
        b = _dot(u, wu_ref[:, sl])
        act_ref[:, sl] = (a * jax.nn.sigmoid(a) * b).astype(BF)
    y = x + FFN_RESIDUAL * _dot(act_ref[...], wd_ref[...])
    if final:
        y = _rms(y, fg_ref[...])
    o_ref[...] = y


def _ffn(h, g, wg, wu, wd, fg, final, tm=512):
    t, d = h.shape
    d_ff = wg.shape[1]
    tm = min(tm, t)
    return pl.pallas_call(
        functools.partial(_ffn_body, final=final),
        grid=(t // tm,),
        in_specs=[
            pl.BlockSpec((tm, d), lambda i: (i, 0)),
            _resident((1, d)),
            _resident((d, d_ff)),
            _resident((d, d_ff)),
            _resident((d_ff, d)),
            _resident((1, d)),
        ],
        out_specs=pl.BlockSpec((tm, d), lambda i: (i, 0)),
        out_shape=jax.ShapeDtypeStruct((t, d), F32),
        scratch_shapes=[pltpu.VMEM((tm, d_ff), BF)],
        compiler_params=_cparams(("parallel",)),
        name="ffn",
    )(h, g, wg, wu, wd, fg)


def _proj_body(h_ref, g_ref, w_ref, o_ref, *, tn):
    u = _rms(h_ref[...], g_ref[...]).astype(BF)
    for c in range(w_ref.shape[1] // tn):
        sl = slice(c * tn, (c + 1) * tn)
        o_ref[:, sl] = _dot(u, w_ref[:, sl]).astype(o_ref.dtype)


def _norm_proj(h, g, w, tn, tm=512):
    t, d = h.shape
    n = w.shape[1]
    tm = min(tm, t)
    return pl.pallas_call(
        functools.partial(_proj_body, tn=tn),
        grid=(t // tm,),
        in_specs=[
            pl.BlockSpec((tm, d), lambda i: (i, 0)),
            _resident((1, d)),
            _resident((d, n)),
        ],
        out_specs=pl.BlockSpec((tm, n), lambda i: (i, 0)),
        out_shape=jax.ShapeDtypeStruct((t, n), BF),
        compiler_params=_cparams(("parallel",)),
        name="norm_proj",
    )(h, g, w)


def _dil_proj_body(h_ref, g_ref, w_ref, o_ref, *stage, r):
    rows = h_ref.shape[1] // r
    n = w_ref.shape[1]
    u = _rms(h_ref[0], g_ref[...])
    if r == 1:
        o_ref[0] = _dot(u.astype(BF), w_ref[...]).astype(o_ref.dtype)
        return
    (stage,) = stage
    chunks = u.shape[1] // LANES
    for c in range(chunks):
        stage[c] = u[:, c * LANES:(c + 1) * LANES]
    for res in range(r):
        ur = jnp.concatenate([stage[c, pl.ds(res, rows, stride=r), :] for c in range(chunks)], axis=1)
        o_ref[0, :, res * n:(res + 1) * n] = _dot(ur.astype(BF), w_ref[...]).astype(o_ref.dtype)


def _dil_proj(h3, g, w, r):
    b, s, d = h3.shape
    n = w.shape[1]
    tm = min(s, max(1024, BAND * r) if r > 1 else 512)
    return pl.pallas_call(
        functools.partial(_dil_proj_body, r=r),
        grid=(b, s // tm),
        in_specs=[
            pl.BlockSpec((1, tm, d), lambda bi, i: (bi, i, 0)),
            pl.BlockSpec((1, d), lambda bi, i: (0, 0)),
            pl.BlockSpec((d, n), lambda bi, i: (0, 0)),
        ],
        out_specs=pl.BlockSpec((1, tm // r, r * n), lambda bi, i: (bi, i, 0)),
        out_shape=jax.ShapeDtypeStruct((b, s // r, r * n), BF),
        scratch_shapes=[pltpu.VMEM((d // LANES, tm, LANES), F32)] if r > 1 else [],
        compiler_params=_cparams(("parallel", "parallel")),
        name=f"dil_proj_r{r}",
    )(h3, g, w)


def _sb_body(q_ref, k_ref, v_ref, o_ref, *, tq, pairs):
    i = pl.program_id(2)
    pw = 2 * HEAD_DIM
    lane = lax.broadcasted_iota(jnp.int32, (1, pw), 1)
    first = lane < HEAD_DIM
    q_heads = []
    for p in range(pairs):
        q = q_ref[0, :, p * pw:(p + 1) * pw]
        zero = jnp.zeros_like(q)
        q_heads += [jnp.where(first, q, zero), jnp.where(first, zero, q)]
    row = lax.broadcasted_iota(jnp.int32, (tq, tq), 0)
    col = lax.broadcasted_iota(jnp.int32, (tq, tq), 1)
    strict = col < row
    later = jnp.where(row > col, 1.0, 0.0).astype(BF)

    def block(j, accs, rs, diag):
        start = pl.multiple_of(j * tq, tq)
        heads = range(2 * pairs)
        ks = [k_ref[0, pl.ds(start, tq), p * pw:(p + 1) * pw] for p in range(pairs)]
        vs = [v_ref[0, pl.ds(start, tq), p * pw:(p + 1) * pw] for p in range(pairs)]
        zs = [_dot_t(q_heads[h], ks[h // 2]) for h in heads]
        log_betas, log_keeps = [], []
        for z in zs:
            sp = jnp.log(1.0 + jnp.exp2(-jnp.abs(z))) * LOG2E
            log_beta = jnp.minimum(z, 0.0) - sp
            log_keep = log_beta - z
            if diag:
                log_keep = jnp.where(strict, log_keep, 0.0)
            log_betas.append(log_beta)
            log_keeps.append(log_keep)
        afters = [_dot(log_keeps[h].astype(BF), later) for h in heads]
        ws = []
        for h in heads:
            w = jnp.exp2(log_betas[h] + afters[h] + rs[h])
            if diag:
                w = jnp.where(strict, w, 0.0)
            ws.append(w.astype(BF))
        outs = [_dot(ws[h], vs[h // 2]) for h in heads]
        new_rs = [rs[h] + jnp.sum(log_keeps[h], axis=-1, keepdims=True) for h in heads]
        new_accs = [accs[p] + jnp.where(first, outs[2 * p], outs[2 * p + 1]) for p in range(pairs)]
        return tuple(new_accs), tuple(new_rs)

    def remaining(rs):
        return jnp.max(functools.reduce(jnp.maximum, rs))

    def cond(c):
        return (c[0] < i) & (c[1] > SB_DEAD_LOG2)

    def body(c):
        accs, rs = block(i - 1 - c[0], c[2], c[3], False)
        return c[0] + 1, remaining(rs), accs, rs

    accs = tuple(jnp.zeros((tq, pw), F32) for _ in range(pairs))
    rs = tuple(jnp.zeros((tq, 1), F32) for _ in range(2 * pairs))
    accs, rs = block(i, accs, rs, True)
    out = lax.while_loop(cond, body, (jnp.int32(0), remaining(rs), accs, rs))
    for p in range(pairs):
        o_ref[0, :, p * pw:(p + 1) * pw] = out[2][p].astype(o_ref.dtype)


def _sb_attention(a3, tq=256, pairs=2):
    b, s, _ = a3.shape
    tq = min(tq, s)
    w = 2 * HEAD_DIM * pairs
    groups = SB_WIDTH // w
    return pl.pallas_call(
        functools.partial(_sb_body, tq=tq, pairs=pairs),
        grid=(b, groups, s // tq),
        in_specs=[
            pl.BlockSpec((1, tq, w), lambda bi, p, i: (bi, i, p)),
            pl.BlockSpec((1, s, w), lambda bi, p, i: (bi, 0, groups + p)),
            pl.BlockSpec((1, s, w), lambda bi, p, i: (bi, 0, 2 * groups + p)),
        ],
        out_specs=pl.BlockSpec((1, tq, w), lambda bi, p, i: (bi, i, p)),
        out_shape=jax.ShapeDtypeStruct((b, s, SB_WIDTH), BF),
        compiler_params=_cparams(("parallel", "parallel", "arbitrary")),
        name="sb_attn",
    )(a3, a3, a3)


def _mla_prep_body(cq_ref, ckv_ref, kr_ref, cos_ref, sin_ref, qg_ref, kvg_ref, wq_ref, wkv_ref,
                   qp_ref, kp_ref, v_ref, *, scale):
    nq = _rms(cq_ref[0].astype(F32), qg_ref[...]).astype(BF)
    q = _dot(nq, wq_ref[...])
    nkv = _rms(ckv_ref[0].astype(F32), kvg_ref[...]).astype(BF)
    kv = _dot(nkv, wkv_ref[...])
    cos = cos_ref[...]
    sin = sin_ref[...]
    kr = kr_ref[0].astype(F32)
    k_pe = (kr[:, :128] * cos + kr[:, 128:] * sin).astype(BF)
    width = q.shape[1] // 3
    for p in range(width // 128):
        lo, hi = p * 128, (p + 1) * 128
        qp_ref[0, :, 2 * lo:2 * lo + 128] = (q[:, lo:hi] * scale).astype(BF)
        q_pe = q[:, width + lo:width + hi] * cos + q[:, 2 * width + lo:2 * width + hi] * sin
        qp_ref[0, :, 2 * lo + 128:2 * hi] = (q_pe * scale).astype(BF)
        kp_ref[0, :, 2 * lo:2 * lo + 128] = kv[:, lo:hi].astype(BF)
        kp_ref[0, :, 2 * lo + 128:2 * hi] = k_pe
    v_ref[0] = kv[:, width:].astype(BF)


def _mla_prep(a3, cos, sin, qg, kvg, wq, wkv, tm=512):
    b, s, _ = a3.shape
    tm = min(tm, s)
    width = MLA_HEADS * MLA_NOPE
    scale = (MLA_NOPE + MLA_ROPE) ** -0.5 * LOG2E
    out_sds = lambda n: jax.ShapeDtypeStruct((b, s, n), BF)
    return pl.pallas_call(
        functools.partial(_mla_prep_body, scale=scale),
        grid=(b, s // tm),
        in_specs=[
            pl.BlockSpec((1, tm, MLA_Q_RANK), lambda bi, i: (bi, i, 1536 // MLA_Q_RANK)),
            pl.BlockSpec((1, tm, MLA_KV_RANK), lambda bi, i: (bi, i, 1920 // MLA_KV_RANK)),
            pl.BlockSpec((1, tm, 256), lambda bi, i: (bi, i, 2048 // 256)),
            pl.BlockSpec((tm, 128), lambda bi, i: (i, 0)),
            pl.BlockSpec((tm, 128), lambda bi, i: (i, 0)),
            pl.BlockSpec((1, MLA_Q_RANK), lambda bi, i: (0, 0)),
            pl.BlockSpec((1, MLA_KV_RANK), lambda bi, i: (0, 0)),
            pl.BlockSpec(wq.shape, lambda bi, i: (0, 0)),
            pl.BlockSpec(wkv.shape, lambda bi, i: (0, 0)),
        ],
        out_specs=[
            pl.BlockSpec((1, tm, 2 * width), lambda bi, i: (bi, i, 0)),
            pl.BlockSpec((1, tm, 2 * width), lambda bi, i: (bi, i, 0)),
            pl.BlockSpec((1, tm, width), lambda bi, i: (bi, i, 0)),
        ],
        out_shape=[out_sds(2 * width), out_sds(2 * width), out_sds(width)],
        compiler_params=_cparams(("parallel", "parallel")),
        name="mla_prep",
    )(a3, a3, a3, cos, sin, qg, kvg, wq, wkv)


def _mla_body(q_ref, k_ref, v_ref, o_ref, *, tq, pairs):
    i = pl.program_id(2)
    lane = lax.broadcasted_iota(jnp.int32, (1, 256), 1)
    sel0 = (lane < 64) | ((lane >= 128) & (lane < 160))
    sel1 = ((lane >= 64) & (lane < 128)) | ((lane >= 160) & (lane < 192))
    q_heads = []
    for p in range(pairs):
        q = q_ref[0, :, p * 256:(p + 1) * 256]
        zero = jnp.zeros_like(q)
        q_heads += [jnp.where(sel0, q, zero), jnp.where(sel1, q, zero)]
    row = lax.broadcasted_iota(jnp.int32, (tq, tq), 0)
    col = lax.broadcasted_iota(jnp.int32, (tq, tq), 1)
    causal = col <= row

    first = lax.broadcasted_iota(jnp.int32, (1, 128), 1) < 64

    def block(j, carry, diag):
        start = pl.multiple_of(j * tq, tq)
        heads = range(2 * pairs)
        ks = [k_ref[0, pl.ds(start, tq), p * 256:(p + 1) * 256] for p in range(pairs)]
        v_heads = []
        for p in range(pairs):
            vj = v_ref[0, pl.ds(start, tq), p * 128:(p + 1) * 128]
            ones = jnp.ones_like(vj)
            v_heads += [jnp.where(first, vj, ones), jnp.where(first, ones, vj)]
        ss = [_dot_t(q_heads[h], ks[h // 2]) for h in heads]
        if diag:
            ss = [jnp.where(causal, s, NEG) for s in ss]
        m_news = [jnp.maximum(carry[h][0], jnp.max(ss[h], axis=-1, keepdims=True)) for h in heads]
        prs = [jnp.exp2(ss[h] - m_news[h]).astype(BF) for h in heads]
        pvs = [_dot(prs[h], v_heads[h]) for h in heads]
        return tuple((m_news[h], jnp.exp2(carry[h][0] - m_news[h]) * carry[h][1] + pvs[h]) for h in heads)

    one = (jnp.full((tq, 1), NEG, F32), jnp.zeros((tq, 128), F32))
    carry = lax.fori_loop(0, i, lambda j, c: block(j, c, False), (one,) * (2 * pairs))
    carry = block(i, carry, True)
    for p in range(pairs):
        a0, a1 = carry[2 * p][1], carry[2 * p + 1][1]
        l = pltpu.roll(jnp.where(first, a1, a0), 64, 1)
        o_ref[0, :, p * 128:(p + 1) * 128] = (jnp.where(first, a0, a1) / l).astype(o_ref.dtype)


def _mla_attention(qp, kp, v, tq=512, pairs=2):
    b, s, _ = qp.shape
    tq = min(tq, s)
    groups = MLA_HEADS // (2 * pairs)
    return pl.pallas_call(
        functools.partial(_mla_body, tq=tq, pairs=pairs),
        grid=(b, groups, s // tq),
        in_specs=[
            pl.BlockSpec((1, tq, 256 * pairs), lambda bi, p, i: (bi, i, p)),
            pl.BlockSpec((1, s, 256 * pairs), lambda bi, p, i: (bi, 0, p)),
            pl.BlockSpec((1, s, 128 * pairs), lambda bi, p, i: (bi, 0, p)),
        ],
        out_specs=pl.BlockSpec((1, tq, 128 * pairs), lambda bi, p, i: (bi, i, p)),
        out_shape=jax.ShapeDtypeStruct((b, s, MLA_HEADS * MLA_V), BF),
        compiler_params=_cparams(("parallel", "parallel", "arbitrary")),
        name="mla_attn",
    )(qp, kp, v)


def _dil_body(q_ref, kc_ref, kp_ref, vc_ref, vp_ref, bias_ref, o_ref, lse_ref, *, tl):
    j = pl.program_id(2)
    lane = lax.broadcasted_iota(jnp.int32, (1, 2 * HEAD_DIM), 1)
    first = lane < HEAD_DIM
    qi = lax.broadcasted_iota(jnp.int32, (BAND, 2 * BAND), 0)
    kj = lax.broadcasted_iota(jnp.int32, (BAND, 2 * BAND), 1)
    steps = qi + BAND - kj
    in_window = (steps >= 0) & (steps <= BAND)
    in_window_first = in_window & ((kj >= BAND) | (j > 0))
    pw = 2 * HEAD_DIM
    pairs = DIL_WIDTH // pw
    heads = range(2 * pairs)
    for sb in range(tl // BAND):
        rows = slice(sb * BAND, (sb + 1) * BAND)
        prev = slice((sb - 1) * BAND, sb * BAND)
        k_prev = kp_ref[0] if sb == 0 else kc_ref[0, prev, :]
        v_prev = vp_ref[0] if sb == 0 else vc_ref[0, prev, :]
        kcat = jnp.concatenate([k_prev, kc_ref[0, rows, :]], axis=0)
        vcat = jnp.concatenate([v_prev, vc_ref[0, rows, :]], axis=0)
        valid = in_window_first if sb == 0 else in_window
        qs = q_ref[0, rows, :]
        q_heads, v_heads = [], []
        for p in range(pairs):
            qb = qs[:, p * pw:(p + 1) * pw]
            vb = vcat[:, p * pw:(p + 1) * pw]
            zero, ones = jnp.zeros_like(qb), jnp.ones_like(vb)
            q_heads += [jnp.where(first, qb, zero), jnp.where(first, zero, qb)]
            v_heads += [jnp.where(first, vb, ones), jnp.where(first, ones, vb)]
        ss = [_dot_t(q_heads[h], kcat[:, (h // 2) * pw:(h // 2 + 1) * pw]) + bias_ref[h] for h in heads]
        ss = [jnp.where(valid, s, NEG) for s in ss]
        ms = [jnp.max(s, axis=-1, keepdims=True) for s in ss]
        prs = [jnp.exp2(ss[h] - ms[h]).astype(BF) for h in heads]
        pvs = [_dot(prs[h], v_heads[h]) for h in heads]
        for p in range(pairs):
            a0, a1 = pvs[2 * p], pvs[2 * p + 1]
            cols = slice(p * pw, (p + 1) * pw)
            l = pltpu.roll(jnp.where(first, a1, a0), HEAD_DIM, 1)
            o_ref[0, rows, cols] = (jnp.where(first, a0, a1) / l).astype(o_ref.dtype)
            lse_ref[0, rows, cols] = jnp.where(first, ms[2 * p], ms[2 * p + 1]) * LN2 + jnp.log(l)


def _dil_attention(view, bias, dilation, tl=512):
    r = dilation
    batch, length, _ = view.shape
    tl = min(tl, length)
    sub = tl // BAND

    def cur(c):
        return pl.BlockSpec((1, tl, DIL_WIDTH), lambda bi, res, j: (bi, j, res * 3 + c))

    def prev(c):
        return pl.BlockSpec((1, BAND, DIL_WIDTH),
                            lambda bi, res, j: (bi, jnp.maximum(j * sub - 1, 0), res * 3 + c))

    out_spec = pl.BlockSpec((1, tl, DIL_WIDTH), lambda bi, res, j: (bi, j, res))
    return pl.pallas_call(
        functools.partial(_dil_body, tl=tl),
        grid=(batch, r, length // tl),
        in_specs=[cur(0), cur(1), prev(1), cur(2), prev(2),
                  pl.BlockSpec(bias.shape, lambda bi, res, j: (0, 0, 0))],
        out_specs=[out_spec, out_spec],
        out_shape=[jax.ShapeDtypeStruct((batch, length, r * DIL_WIDTH), BF),
                   jax.ShapeDtypeStruct((batch, length, r * DIL_WIDTH), F32)],
        compiler_params=_cparams(("parallel", "parallel", "arbitrary")),
        name=f"dil_attn_r{r}",
    )(view, view, view, view, view, bias)


def _merge_body(h_ref, oa_ref, ob_ref, oc0_ref, oc1_ref, oc2_ref, l0_ref, l1_ref, l2_ref,
                ga_ref, gb_ref, gc_ref, gbias_ref, wa_ref, wb_ref, wc_ref, wo_ref, o_ref, *scratch):
    tm = h_ref.shape[1]
    scratch = list(scratch)

    def token_major(ref, r):
        if r == 1:
            return ref[0].astype(F32)
        buf = scratch.pop(0)
        chunks = DIL_WIDTH // LANES
        for res in range(r):
            for c in range(chunks):
                lo = res * DIL_WIDTH + c * LANES
                buf[c, pl.ds(res, tm // r, stride=r), :] = ref[0, :, lo:lo + LANES].astype(F32)
        return jnp.concatenate([buf[c] for c in range(chunks)], axis=1)

    dils = [dil for _, dil in DIL_GROUPS]
    l0, l1, l2 = (token_major(ref, r) for ref, r in zip((l0_ref, l1_ref, l2_ref), dils))
    mx = jnp.maximum(jnp.maximum(l0, l1), l2)
    e0, e1, e2 = jnp.exp(l0 - mx), jnp.exp(l1 - mx), jnp.exp(l2 - mx)
    o0, o1, o2 = (token_major(ref, r) for ref, r in zip((oc0_ref, oc1_ref, oc2_ref), dils))
    oc = (e0 * o0 + e1 * o1 + e2 * o2) / (e0 + e1 + e2)
    gbias = gbias_ref[...]
    merged = (jax.nn.sigmoid(ga_ref[0].astype(F32) + gbias[0:1]) * _dot(oa_ref[0], wa_ref[...])
              + jax.nn.sigmoid(gb_ref[0].astype(F32) + gbias[1:2]) * _dot(ob_ref[0], wb_ref[...])
              + jax.nn.sigmoid(gc_ref[0].astype(F32) + gbias[2:3]) * _dot(oc.astype(BF), wc_ref[...]))
    o_ref[0] = h_ref[0] + _dot(merged.astype(BF), wo_ref[...])


def _merge(h3, oa, ob, ocs, lses, gates, gbias, wa, wb, wc, wo, tm=512):
    b, s, d = h3.shape
    tm = min(tm, s)
    row = lambda n: pl.BlockSpec((1, tm, n), lambda bi, i: (bi, i, 0))
    gate = lambda c: pl.BlockSpec((1, tm, d), lambda bi, i: (bi, i, c))
    views = [pl.BlockSpec((1, tm // r, r * DIL_WIDTH), lambda bi, i: (bi, i, 0)) for _, r in DIL_GROUPS]
    n_buf = 2 * sum(1 for _, r in DIL_GROUPS if r > 1)
    return pl.pallas_call(
        _merge_body,
        grid=(b, s // tm),
        in_specs=[row(d), row(SB_WIDTH), row(SB_WIDTH)] + views + views
        + [gate(0), gate(1), gate(2), _resident(gbias.shape),
           _resident(wa.shape), _resident(wb.shape), _resident(wc.shape), _resident(wo.shape)],
        out_specs=row(d),
        out_shape=jax.ShapeDtypeStruct((b, s, d), F32),
        scratch_shapes=[pltpu.VMEM((DIL_WIDTH // LANES, tm, LANES), F32)] * n_buf,
        compiler_params=_cparams(("parallel", "parallel")),
        name="merge",
    )(h3, oa, ob, *ocs, *lses, gates, gates, gates, gbias, wa, wb, wc, wo)


def _memkv_body(m_ref, g_ref, w_ref, o_ref):
    u = _rms(m_ref[0], g_ref[...]).astype(BF)
    o_ref[0] = _dot(u, w_ref[...]).astype(o_ref.dtype)


def _mem_kv(mem, g, w):
    b, m, d = mem.shape
    n = w.shape[1]
    return pl.pallas_call(
        _memkv_body,
        grid=(b,),
        in_specs=[pl.BlockSpec((1, m, d), lambda i: (i, 0, 0)),
                  pl.BlockSpec((1, d), lambda i: (0, 0)),
                  pl.BlockSpec((d, n), lambda i: (0, 0))],
        out_specs=pl.BlockSpec((1, m, n), lambda i: (i, 0, 0)),
        out_shape=jax.ShapeDtypeStruct((b, m, n), BF),
        compiler_params=_cparams(("parallel",)),
        name="mem_kv",
    )(mem, g, w)


def _xattn_body(h_ref, g_ref, kv_ref, wq_ref, wo_ref, o_ref):
    x = h_ref[0]
    u = _rms(x, g_ref[...]).astype(BF)
    q = _dot(u, wq_ref[...]).astype(BF)
    kv = kv_ref[0]
    width = X_HEADS * X_HEAD_DIM
    scale = X_HEAD_DIM ** -0.5
    outs = []
    for h in range(X_HEADS):
        cols = slice(h * X_HEAD_DIM, (h + 1) * X_HEAD_DIM)
        s = _dot_t(q[:, cols], kv[:, cols]) * scale
        m = jnp.max(s, axis=-1, keepdims=True)
        e = jnp.exp(s - m)
        p = e / jnp.sum(e, axis=-1, keepdims=True)
        outs.append(_dot(p.astype(BF), kv[:, width + h * X_HEAD_DIM:width + (h + 1) * X_HEAD_DIM]))
    o = jnp.concatenate(outs, axis=-1).astype(BF)
    o_ref[0] = x + _dot(o, wo_ref[...])


def _xattn(h3, g, kv, wq, wo, tm=512):
    b, s, d = h3.shape
    tm = min(tm, s)
    m = kv.shape[1]
    return pl.pallas_call(
        _xattn_body,
        grid=(b, s // tm),
        in_specs=[pl.BlockSpec((1, tm, d), lambda bi, i: (bi, i, 0)),
                  pl.BlockSpec((1, d), lambda bi, i: (0, 0)),
                  pl.BlockSpec((1, m, kv.shape[2]), lambda bi, i: (bi, 0, 0)),
                  pl.BlockSpec(wq.shape, lambda bi, i: (0, 0)),
                  pl.BlockSpec(wo.shape, lambda bi, i: (0, 0))],
        out_specs=pl.BlockSpec((1, tm, d), lambda bi, i: (bi, i, 0)),
        out_shape=jax.ShapeDtypeStruct((b, s, d), F32),
        compiler_params=_cparams(("parallel", "parallel")),
        name="xattn",
    )(h3, g, kv, wq, wo)


def _rel_bucket(dist):
    exact = REL_BUCKETS // 2
    d = jnp.maximum(dist, exact).astype(F32)
    large = exact + (jnp.log(d / exact) / math.log(REL_MAX_DIST / exact)
                     * (REL_BUCKETS - exact)).astype(jnp.int32)
    return jnp.where(dist < exact, dist, jnp.minimum(large, REL_BUCKETS - 1))


def _band_bias(rel_bias, group, dilation):
    heads = DIL_WIDTH // HEAD_DIM
    qi = jnp.arange(BAND)
    kj = jnp.arange(2 * BAND)
    steps = (qi[:, None] + BAND) - kj[None, :]
    table = rel_bias[:, group * heads:(group + 1) * heads].astype(F32)
    bucket = _rel_bucket(jnp.clip(steps, 0, BAND) * dilation)
    onehot = (bucket[None] == jnp.arange(REL_BUCKETS)[:, None, None]).astype(F32)
    return jnp.einsum('bh,bqk->hqk', table * LOG2E, onehot, precision=lax.Precision.HIGHEST)


def _swap_halves(w, width):
    k, n = w.shape
    w = w.reshape(k, n // width, 2, width // 2)
    return w[:, :, ::-1, :].reshape(k, n)


def _pack_w_in(w_in):
    d = w_in.shape[0]
    qscale = HEAD_DIM ** -0.5 * LOG2E
    sb = jnp.concatenate([w_in[:, :SB_WIDTH] * qscale, w_in[:, SB_WIDTH:1536]], axis=1)
    cq = w_in[:, 1536:1920]
    ckv = w_in[:, 1920:2048]
    kr = w_in[:, 2048:2080]
    zeros = jnp.zeros((d, 64), w_in.dtype)
    kr_sw = _swap_halves(kr, MLA_ROPE)
    seg_a = jnp.concatenate([sb, cq, ckv, kr, kr, zeros, kr_sw, kr_sw, zeros], axis=1)
    group = 3 * DIL_WIDTH
    seg_d = []
    for g in range(len(DIL_GROUPS)):
        lo = 2080 + g * group
        seg_d.append(jnp.concatenate([w_in[:, lo:lo + DIL_WIDTH] * qscale, w_in[:, lo + DIL_WIDTH:lo + group]],
                                     axis=1).astype(BF))
    seg_g = w_in[:, 2080 + len(DIL_GROUPS) * group:]
    return seg_a.astype(BF), seg_d, seg_g.astype(BF)


def _pack_w_uq(w_uq):
    k = w_uq.shape[0]
    w = w_uq.reshape(k, MLA_HEADS, MLA_NOPE + MLA_ROPE)
    nope = w[:, :, :MLA_NOPE].reshape(k, MLA_HEADS * MLA_NOPE)
    rope = w[:, :, MLA_NOPE:]

    def pair_layout(rp):
        rp = rp.reshape(k, MLA_HEADS // 2, 2 * MLA_ROPE)
        pad = jnp.zeros((k, MLA_HEADS // 2, 128 - 2 * MLA_ROPE), rp.dtype)
        return jnp.concatenate([rp, pad], axis=-1).reshape(k, (MLA_HEADS // 2) * 128)

    rope_sw = rope.reshape(k, MLA_HEADS, 2, MLA_ROPE // 2)[:, :, ::-1, :].reshape(k, MLA_HEADS, MLA_ROPE)
    return jnp.concatenate([nope, pair_layout(rope), pair_layout(rope_sw)], axis=1).astype(BF)


def _pack_w_ukv(w_ukv):
    k = w_ukv.shape[0]
    w = w_ukv.reshape(k, MLA_HEADS, MLA_NOPE + MLA_V)
    return jnp.concatenate([w[:, :, :MLA_NOPE].reshape(k, -1), w[:, :, MLA_NOPE:].reshape(k, -1)],
                           axis=1).astype(BF)


def _rope_tables(seq):
    half = MLA_ROPE // 2
    freqs = ROPE_THETA ** (-jnp.arange(half, dtype=F32) / half)
    ang = jnp.arange(seq).astype(F32)[:, None] * freqs[None, :]
    cos, sin = jnp.cos(ang), jnp.sin(ang)
    pad = jnp.zeros((seq, 128 - 2 * MLA_ROPE), F32)
    cos_t = jnp.concatenate([cos, cos, cos, cos, pad], axis=1)
    sin_t = jnp.concatenate([-sin, sin, -sin, sin, pad], axis=1)
    return cos_t, sin_t


def kernel(x, mem, ffn1_norm, ffn1_w_gate, ffn1_w_up, ffn1_w_down, mix_norm, w_in, gate_bias, mla_q_norm, mla_w_uq, mla_kv_norm, mla_w_ukv, w_branch_a, w_branch_b, w_branch_c, w_mix_out, rel_bias, xattn_norm, mem_norm, xattn_w_q, xattn_w_kv, xattn_w_o, ffn2_norm, ffn2_w_gate, ffn2_w_up, ffn2_w_down, final_norm):
    b, s, d = x.shape
    t = b * s
    depth = w_in.shape[0]
    cos_t, sin_t = _rope_tables(s)
    biases = [_band_bias(rel_bias, g, dil) for g, (_, dil) in enumerate(DIL_GROUPS)]
    row = lambda v: v.reshape(1, -1)
    bf = lambda w: w.astype(BF)
    fg = row(final_norm)

    h = x.reshape(t, d)
    for l in range(depth):
        h = _ffn(h, row(ffn1_norm[l]), bf(ffn1_w_gate[l]), bf(ffn1_w_up[l]), bf(ffn1_w_down[l]), fg, False)

        w_a, w_d, w_g = _pack_w_in(w_in[l])
        g_mix = row(mix_norm[l])
        h3 = h.reshape(b, s, d)
        seg_a = _norm_proj(h, g_mix, w_a, tn=768)
        seg_g = _norm_proj(h, g_mix, w_g, tn=768)

        a3 = seg_a.reshape(b, s, seg_a.shape[1])
        o_a = _sb_attention(a3)
        qp, kp, v = _mla_prep(a3, cos_t, sin_t, row(mla_q_norm[l]), row(mla_kv_norm[l]),
                              _pack_w_uq(mla_w_uq[l]), _pack_w_ukv(mla_w_ukv[l]))
        o_b = _mla_attention(qp, kp, v)
        ocs, lses = [], []
        for g, (_, dil) in enumerate(DIL_GROUPS):
            o, lse = _dil_attention(_dil_proj(h3, g_mix, w_d[g], dil), biases[g], dil)
            ocs.append(o)
            lses.append(lse)
        h3 = _merge(h3, o_a, o_b, ocs, lses, seg_g.reshape(b, s, seg_g.shape[1]), gate_bias[l],
                    bf(w_branch_a[l]), bf(w_branch_b[l]), bf(w_branch_c[l]), bf(w_mix_out[l]))

        kv = _mem_kv(mem, row(mem_norm[l]), bf(xattn_w_kv[l]))
        h = _xattn(h3, row(xattn_norm[l]), kv, bf(xattn_w_q[l]), bf(xattn_w_o[l])).reshape(t, d)

        h = _ffn(h, row(ffn2_norm[l]), bf(ffn2_w_gate[l]), bf(ffn2_w_up[l]), bf(ffn2_w_down[l]), fg,
                 l == depth - 1)
    return h.reshape(b, s, d)
```

```python
import functools
import math

import jax
import jax.numpy as jnp
import numpy as np
from jax import lax
from jax.experimental import pallas as pl
from jax.experimental.pallas import tpu as pltpu

BF = jnp.bfloat16
F32 = jnp.float32

EPS = 1e-6
NEG = -1e30
FFN_RESIDUAL = 0.5

SB_WIDTH = 512
MLA_HEADS = 8
MLA_Q_RANK = 384
MLA_KV_RANK = 128
MLA_NOPE = 64
MLA_ROPE = 32
MLA_V = 64
ROPE_THETA = 10000.0
DIL_GROUPS = ((128, 1), (512, 4), (2048, 16))
DIL_WIDTH = 512
REL_BUCKETS = 32
REL_MAX_DIST = 2048
X_HEADS = 4
X_HEAD_DIM = 128
HEAD_DIM = 64
BAND = 128
LANES = 128
SB_DEAD_LOG2 = -150.0
LOG2E = math.log2(math.e)
LN2 = math.log(2.0)

V7X_VMEM_LIMIT = 48 * 1024 * 1024
FF_CHUNK = 256
PROJ_CHUNK = 768


def _cparams(sem):
    return pltpu.CompilerParams(dimension_semantics=sem, vmem_limit_bytes=V7X_VMEM_LIMIT)


def _rms(x, g):
    ms = jnp.mean(x * x, axis=-1, keepdims=True)
    return x * lax.rsqrt(ms + EPS) * g


def _sigmoid(x):
    return 0.5 * jnp.tanh(0.5 * x) + 0.5


def _dot(a, b):
    return jnp.dot(a, b, preferred_element_type=F32)


def _dot_t(a, b):
    return lax.dot_general(a, b, (((1,), (1,)), ((), ())), preferred_element_type=F32)


def _resident(shape):
    nd = len(shape)
    return pl.BlockSpec(shape, lambda *_: (0,) * nd, pipeline_mode=pl.Buffered(1))


def _ffn_body(h_ref, g_ref, wg_ref, wu_ref, wd_ref, fg_ref, o_ref, act_ref, *, final):
    x = h_ref[...]
    u = _rms(x, g_ref[...]).astype(BF)
    d_ff = wg_ref.shape[1]
    for c in range(d_ff // FF_CHUNK):
        sl = slice(c * FF_CHUNK, (c + 1) * FF_CHUNK)
        a = _dot(u, wg_ref[:, sl])
        b = _dot(u, wu_ref[:, sl])
        act_ref[:, sl] = (a * jax.nn.sigmoid(a) * b).astype(BF)
    y = x + FFN_RESIDUAL * _dot(act_ref[...], wd_ref[...])
    if final:
        y = _rms(y, fg_ref[...])
    o_ref[...] = y


def _ffn(h, g, wg, wu, wd, fg, final, tm=512):
    t, d = h.shape
    d_ff = wg.shape[1]
    tm = min(tm, t)
    return pl.pallas_call(
        functools.partial(_ffn_body, final=final),
        grid=(t // tm,),
        in_specs=[
            pl.BlockSpec((tm, d), lambda i: (i, 0)),
            _resident((1, d)),
            _resident((d, d_ff)),
            _resident((d, d_ff)),
            _resident((d_ff, d)),
            _resident((1, d)),
        ],
        out_specs=pl.BlockSpec((tm, d), lambda i: (i, 0)),
        out_shape=jax.ShapeDtypeStruct((t, d), F32),
        scratch_shapes=[pltpu.VMEM((tm, d_ff), BF)],
        compiler_params=_cparams(("parallel",)),
        name="ffn",
    )(h, g, wg, wu, wd, fg)


def _proj_body(h_ref, g_ref, w_ref, o_ref, *, tn):
    u = _rms(h_ref[...], g_ref[...]).astype(BF)
    for c in range(w_ref.shape[1] // tn):
        sl = slice(c * tn, (c + 1) * tn)
        o_ref[:, sl] = _dot(u, w_ref[:, sl]).astype(o_ref.dtype)


def _norm_proj(h, g, w, tn, tm=512):
    t, d = h.shape
    n = w.shape[1]
    tm = min(tm, t)
    return pl.pallas_call(
        functools.partial(_proj_body, tn=tn),
        grid=(t // tm,),
        in_specs=[
            pl.BlockSpec((tm, d), lambda i: (i, 0)),
            _resident((1, d)),
            _resident((d, n)),
        ],
        out_specs=pl.BlockSpec((tm, n), lambda i: (i, 0)),
        out_shape=jax.ShapeDtypeStruct((t, n), BF),
        compiler_params=_cparams(("parallel",)),
        name="norm_proj",
    )(h, g, w)


def _dil_proj_body(h_ref, g_ref, w_ref, o_ref, *stage, r):
    rows = h_ref.shape[1] // r
    n = w_ref.shape[1]
    u = _rms(h_ref[0], g_ref[...])
    if r == 1:
        ub = u.astype(BF)
        for c in range(n // PROJ_CHUNK):
            sl = slice(c * PROJ_CHUNK, (c + 1) * PROJ_CHUNK)
            o_ref[0, :, sl] = _dot(ub, w_ref[:, sl]).astype(o_ref.dtype)
        return
    (stage,) = stage
    chunks = u.shape[1] // LANES
    for c in range(chunks):
        stage[c] = u[:, c * LANES:(c + 1) * LANES]
    for res in range(r):
        ur = jnp.concatenate([stage[c, pl.ds(res, rows, stride=r), :] for c in range(chunks)], axis=1)
        o_ref[0, :, res * n:(res + 1) * n] = _dot(ur.astype(BF), w_ref[...]).astype(o_ref.dtype)


def _dil_proj(h3, g, w, r):
    b, s, d = h3.shape
    n = w.shape[1]
    tm = min(s, max(1024, BAND * r) if r > 1 else 512)
    return pl.pallas_call(
        functools.partial(_dil_proj_body, r=r),
        grid=(b, s // tm),
        in_specs=[
            pl.BlockSpec((1, tm, d), lambda bi, i: (bi, i, 0)),
            pl.BlockSpec((1, d), lambda bi, i: (0, 0)),
            pl.BlockSpec((d, n), lambda bi, i: (0, 0)),
        ],
        out_specs=pl.BlockSpec((1, tm // r, r * n), lambda bi, i: (bi, i, 0)),
        out_shape=jax.ShapeDtypeStruct((b, s // r, r * n), BF),
        scratch_shapes=[pltpu.VMEM((d // LANES, tm, LANES), F32)] if r > 1 else [],
        compiler_params=_cparams(("parallel", "parallel")),
        name=f"dil_proj_r{r}",
    )(h3, g, w)


def _sb_body(q_ref, k_ref, v_ref, o_ref, *, tq, pairs):
    i = pl.program_id(2)
    pw = 2 * HEAD_DIM
    lane = lax.broadcasted_iota(jnp.int32, (1, pw), 1)
    first = lane < HEAD_DIM
    q_heads = []
    for p in range(pairs):
        q = q_ref[0, :, p * pw:(p + 1) * pw]
        zero = jnp.zeros_like(q)
        q_heads += [jnp.where(first, q, zero), jnp.where(first, zero, q)]
    row = lax.broadcasted_iota(jnp.int32, (tq, tq), 0)
    col = lax.broadcasted_iota(jnp.int32, (tq, tq), 1)
    strict = col < row
    later = jnp.where(row > col, 1.0, 0.0).astype(BF)

    def block(j, accs, rs, diag):
        start = pl.multiple_of(j * tq, tq)
        heads = range(2 * pairs)
        ks = [k_ref[0, pl.ds(start, tq), p * pw:(p + 1) * pw] for p in range(pairs)]
        vs = [v_ref[0, pl.ds(start, tq), p * pw:(p + 1) * pw] for p in range(pairs)]
        zs = [_dot_t(q_heads[h], ks[h // 2]) for h in heads]
        log_betas, log_keeps = [], []
        for z in zs:
            sp = jnp.log(1.0 + jnp.exp2(-jnp.abs(z))) * LOG2E
            log_beta = jnp.minimum(z, 0.0) - sp
            log_keep = log_beta - z
            if diag:
                log_keep = jnp.where(strict, log_keep, 0.0)
            log_betas.append(log_beta)
            log_keeps.append(log_keep)
        afters = [_dot(log_keeps[h].astype(BF), later) for h in heads]
        ws = []
        for h in heads:
            w = jnp.exp2(log_betas[h] + afters[h] + rs[h])
            if diag:
                w = jnp.where(strict, w, 0.0)
            ws.append(w.astype(BF))
        outs = [_dot(ws[h], vs[h // 2]) for h in heads]
        new_rs = [rs[h] + jnp.sum(log_keeps[h], axis=-1, keepdims=True) for h in heads]
        new_accs = [accs[p] + jnp.where(first, outs[2 * p], outs[2 * p + 1]) for p in range(pairs)]
        return tuple(new_accs), tuple(new_rs)

    def remaining(rs):
        return jnp.max(functools.reduce(jnp.maximum, rs))

    def cond(c):
        return (c[0] < i) & (c[1] > SB_DEAD_LOG2)

    def body(c):
        accs, rs = block(i - 1 - c[0], c[2], c[3], False)
        return c[0] + 1, remaining(rs), accs, rs

    accs = tuple(jnp.zeros((tq, pw), F32) for _ in range(pairs))
    rs = tuple(jnp.zeros((tq, 1), F32) for _ in range(2 * pairs))
    accs, rs = block(i, accs, rs, True)
    out = lax.while_loop(cond, body, (jnp.int32(0), remaining(rs), accs, rs))
    for p in range(pairs):
        o_ref[0, :, p * pw:(p + 1) * pw] = out[2][p].astype(o_ref.dtype)


def _sb_attention(a3, tq=256, pairs=4):
    b, s, _ = a3.shape
    tq = min(tq, s)
    w = 2 * HEAD_DIM * pairs
    groups = SB_WIDTH // w
    return pl.pallas_call(
        functools.partial(_sb_body, tq=tq, pairs=pairs),
        grid=(b, groups, s // tq),
        in_specs=[
            pl.BlockSpec((1, tq, w), lambda bi, p, i: (bi, i, p)),
            pl.BlockSpec((1, s, w), lambda bi, p, i: (bi, 0, groups + p)),
            pl.BlockSpec((1, s, w), lambda bi, p, i: (bi, 0, 2 * groups + p)),
        ],
        out_specs=pl.BlockSpec((1, tq, w), lambda bi, p, i: (bi, i, p)),
        out_shape=jax.ShapeDtypeStruct((b, s, SB_WIDTH), BF),
        compiler_params=_cparams(("parallel", "parallel", "arbitrary")),
        name="sb_attn",
    )(a3, a3, a3)


def _mla_prep_body(cq_ref, ckv_ref, kr_ref, cos_ref, sin_ref, qg_ref, kvg_ref, wq_ref, wkv_ref,
                   qp_ref, kp_ref, v_ref, *, scale):
    nq = _rms(cq_ref[0].astype(F32), qg_ref[...]).astype(BF)
    q = _dot(nq, wq_ref[...])
    nkv = _rms(ckv_ref[0].astype(F32), kvg_ref[...]).astype(BF)
    kv = _dot(nkv, wkv_ref[...])
    cos = cos_ref[...]
    sin = sin_ref[...]
    kr = kr_ref[0].astype(F32)
    k_pe = (kr[:, :128] * cos + kr[:, 128:] * sin).astype(BF)
    width = q.shape[1] // 3
    for p in range(width // 128):
        lo, hi = p * 128, (p + 1) * 128
        qp_ref[0, :, 2 * lo:2 * lo + 128] = (q[:, lo:hi] * scale).astype(BF)
        q_pe = q[:, width + lo:width + hi] * cos + q[:, 2 * width + lo:2 * width + hi] * sin
        qp_ref[0, :, 2 * lo + 128:2 * hi] = (q_pe * scale).astype(BF)
        kp_ref[0, :, 2 * lo:2 * lo + 128] = kv[:, lo:hi].astype(BF)
        kp_ref[0, :, 2 * lo + 128:2 * hi] = k_pe
    v_ref[0] = kv[:, width:].astype(BF)


def _mla_prep(a3, cos, sin, qg, kvg, wq, wkv, tm=512):
    b, s, _ = a3.shape
    tm = min(tm, s)
    width = MLA_HEADS * MLA_NOPE
    scale = (MLA_NOPE + MLA_ROPE) ** -0.5 * LOG2E
    out_sds = lambda n: jax.ShapeDtypeStruct((b, s, n), BF)
    return pl.pallas_call(
        functools.partial(_mla_prep_body, scale=scale),
        grid=(b, s // tm),
        in_specs=[
            pl.BlockSpec((1, tm, MLA_Q_RANK), lambda bi, i: (bi, i, 1536 // MLA_Q_RANK)),
            pl.BlockSpec((1, tm, MLA_KV_RANK), lambda bi, i: (bi, i, 1920 // MLA_KV_RANK)),
            pl.BlockSpec((1, tm, 256), lambda bi, i: (bi, i, 2048 // 256)),
            pl.BlockSpec((tm, 128), lambda bi, i: (i, 0)),
            pl.BlockSpec((tm, 128), lambda bi, i: (i, 0)),
            pl.BlockSpec((1, MLA_Q_RANK), lambda bi, i: (0, 0)),
            pl.BlockSpec((1, MLA_KV_RANK), lambda bi, i: (0, 0)),
            pl.BlockSpec(wq.shape, lambda bi, i: (0, 0)),
            pl.BlockSpec(wkv.shape, lambda bi, i: (0, 0)),
        ],
        out_specs=[
            pl.BlockSpec((1, tm, 2 * width), lambda bi, i: (bi, i, 0)),
            pl.BlockSpec((1, tm, 2 * width), lambda bi, i: (bi, i, 0)),
            pl.BlockSpec((1, tm, width), lambda bi, i: (bi, i, 0)),
        ],
        out_shape=[out_sds(2 * width), out_sds(2 * width), out_sds(width)],
        compiler_params=_cparams(("parallel", "parallel")),
        name="mla_prep",
    )(a3, a3, a3, cos, sin, qg, kvg, wq, wkv)


def _mla_body(q_ref, k_ref, v_ref, o_ref, *, tq, pairs):
    i = pl.program_id(2)
    lane = lax.broadcasted_iota(jnp.int32, (1, 256), 1)
    sel0 = (lane < 64) | ((lane >= 128) & (lane < 160))
    sel1 = ((lane >= 64) & (lane < 128)) | ((lane >= 160) & (lane < 192))
    q_heads = []
    for p in range(pairs):
        q = q_ref[0, :, p * 256:(p + 1) * 256]
        zero = jnp.zeros_like(q)
        q_heads += [jnp.where(sel0, q, zero), jnp.where(sel1, q, zero)]
    row = lax.broadcasted_iota(jnp.int32, (tq, tq), 0)
    col = lax.broadcasted_iota(jnp.int32, (tq, tq), 1)
    causal = col <= row

    first = lax.broadcasted_iota(jnp.int32, (1, 128), 1) < 64

    def block(j, carry, diag):
        start = pl.multiple_of(j * tq, tq)
        heads = range(2 * pairs)
        ks = [k_ref[0, pl.ds(start, tq), p * 256:(p + 1) * 256] for p in range(pairs)]
        v_heads = []
        for p in range(pairs):
            vj = v_ref[0, pl.ds(start, tq), p * 128:(p + 1) * 128]
            ones = jnp.ones_like(vj)
            v_heads += [jnp.where(first, vj, ones), jnp.where(first, ones, vj)]
        ss = [_dot_t(q_heads[h], ks[h // 2]) for h in heads]
        if diag:
            ss = [jnp.where(causal, s, NEG) for s in ss]
        m_news = [jnp.maximum(carry[h][0], jnp.max(ss[h], axis=-1, keepdims=True)) for h in heads]
        prs = [jnp.exp2(ss[h] - m_news[h]).astype(BF) for h in heads]
        pvs = [_dot(prs[h], v_heads[h]) for h in heads]
        return tuple((m_news[h], jnp.exp2(carry[h][0] - m_news[h]) * carry[h][1] + pvs[h]) for h in heads)

    one = (jnp.full((tq, 1), NEG, F32), jnp.zeros((tq, 128), F32))
    carry = lax.fori_loop(0, i, lambda j, c: block(j, c, False), (one,) * (2 * pairs))
    carry = block(i, carry, True)
    for p in range(pairs):
        a0, a1 = carry[2 * p][1], carry[2 * p + 1][1]
        l = pltpu.roll(jnp.where(first, a1, a0), 64, 1)
        o_ref[0, :, p * 128:(p + 1) * 128] = (jnp.where(first, a0, a1) / l).astype(o_ref.dtype)


def _mla_attention(qp, kp, v, tq=512, pairs=2):
    b, s, _ = qp.shape
    tq = min(tq, s)
    groups = MLA_HEADS // (2 * pairs)
    return pl.pallas_call(
        functools.partial(_mla_body, tq=tq, pairs=pairs),
        grid=(b, groups, s // tq),
        in_specs=[
            pl.BlockSpec((1, tq, 256 * pairs), lambda bi, p, i: (bi, i, p)),
            pl.BlockSpec((1, s, 256 * pairs), lambda bi, p, i: (bi, 0, p)),
            pl.BlockSpec((1, s, 128 * pairs), lambda bi, p, i: (bi, 0, p)),
        ],
        out_specs=pl.BlockSpec((1, tq, 128 * pairs), lambda bi, p, i: (bi, i, p)),
        out_shape=jax.ShapeDtypeStruct((b, s, MLA_HEADS * MLA_V), BF),
        compiler_params=_cparams(("parallel", "parallel", "arbitrary")),
        name="mla_attn",
    )(qp, kp, v)


def _dil_body(q_ref, kc_ref, kp_ref, vc_ref, vp_ref, bias_ref, o_ref, lse_ref, *, tl):
    j = pl.program_id(2)
    lane = lax.broadcasted_iota(jnp.int32, (1, 2 * HEAD_DIM), 1)
    first = lane < HEAD_DIM
    qi = lax.broadcasted_iota(jnp.int32, (BAND, 2 * BAND), 0)
    kj = lax.broadcasted_iota(jnp.int32, (BAND, 2 * BAND), 1)
    steps = qi + BAND - kj
    in_window = (steps >= 0) & (steps <= BAND)
    in_window_first = in_window & ((kj >= BAND) | (j > 0))
    pw = 2 * HEAD_DIM
    pairs = DIL_WIDTH // pw
    heads = range(2 * pairs)
    for sb in range(tl // BAND):
        rows = slice(sb * BAND, (sb + 1) * BAND)
        prev = slice((sb - 1) * BAND, sb * BAND)
        k_prev = kp_ref[0] if sb == 0 else kc_ref[0, prev, :]
        v_prev = vp_ref[0] if sb == 0 else vc_ref[0, prev, :]
        kcat = jnp.concatenate([k_prev, kc_ref[0, rows, :]], axis=0)
        vcat = jnp.concatenate([v_prev, vc_ref[0, rows, :]], axis=0)
        valid = in_window_first if sb == 0 else in_window
        qs = q_ref[0, rows, :]
        q_heads, v_heads = [], []
        for p in range(pairs):
            qb = qs[:, p * pw:(p + 1) * pw]
            vb = vcat[:, p * pw:(p + 1) * pw]
            zero, ones = jnp.zeros_like(qb), jnp.ones_like(vb)
            q_heads += [jnp.where(first, qb, zero), jnp.where(first, zero, qb)]
            v_heads += [jnp.where(first, vb, ones), jnp.where(first, ones, vb)]
        ss = [_dot_t(q_heads[h], kcat[:, (h // 2) * pw:(h // 2 + 1) * pw]) + bias_ref[h] for h in heads]
        ss = [jnp.where(valid, s, NEG) for s in ss]
        ms = [jnp.max(s, axis=-1, keepdims=True) for s in ss]
        prs = [jnp.exp2(ss[h] - ms[h]).astype(BF) for h in heads]
        pvs = [_dot(prs[h], v_heads[h]) for h in heads]
        for p in range(pairs):
            a0, a1 = pvs[2 * p], pvs[2 * p + 1]
            cols = slice(p * pw, (p + 1) * pw)
            l = pltpu.roll(jnp.where(first, a1, a0), HEAD_DIM, 1)
            o_ref[0, rows, cols] = (jnp.where(first, a0, a1) / l).astype(o_ref.dtype)
            lse_ref[0, rows, cols] = jnp.where(first, ms[2 * p], ms[2 * p + 1]) * LN2 + jnp.log(l)


def _dil_attention(view, bias, dilation, tl=512):
    r = dilation
    batch, length, _ = view.shape
    tl = min(tl, length)
    sub = tl // BAND

    def cur(c):
        return pl.BlockSpec((1, tl, DIL_WIDTH), lambda bi, res, j: (bi, j, res * 3 + c))

    def prev(c):
        return pl.BlockSpec((1, BAND, DIL_WIDTH),
                            lambda bi, res, j: (bi, jnp.maximum(j * sub - 1, 0), res * 3 + c))

    out_spec = pl.BlockSpec((1, tl, DIL_WIDTH), lambda bi, res, j: (bi, j, res))
    return pl.pallas_call(
        functools.partial(_dil_body, tl=tl),
        grid=(batch, r, length // tl),
        in_specs=[cur(0), cur(1), prev(1), cur(2), prev(2),
                  pl.BlockSpec(bias.shape, lambda bi, res, j: (0, 0, 0))],
        out_specs=[out_spec, out_spec],
        out_shape=[jax.ShapeDtypeStruct((batch, length, r * DIL_WIDTH), BF),
                   jax.ShapeDtypeStruct((batch, length, r * DIL_WIDTH), F32)],
        compiler_params=_cparams(("parallel", "parallel", "arbitrary")),
        name=f"dil_attn_r{r}",
    )(view, view, view, view, view, bias)


def _merge_body(h_ref, oa_ref, ob_ref, oc0_ref, oc1_ref, oc2_ref, l0_ref, l1_ref, l2_ref,
                ga_ref, gb_ref, gc_ref, gbias_ref, wa_ref, wb_ref, wc_ref, wo_ref, o_ref, *scratch):
    tm = h_ref.shape[1]
    scratch = list(scratch)

    def token_major(ref, r):
        if r == 1:
            return ref[0].astype(F32)
        buf = scratch.pop(0)
        chunks = DIL_WIDTH // LANES
        for res in range(r):
            for c in range(chunks):
                lo = res * DIL_WIDTH + c * LANES
                buf[c, pl.ds(res, tm // r, stride=r), :] = ref[0, :, lo:lo + LANES].astype(F32)
        return jnp.concatenate([buf[c] for c in range(chunks)], axis=1)

    dils = [dil for _, dil in DIL_GROUPS]
    da = _dot(oa_ref[0], wa_ref[...])
    db = _dot(ob_ref[0], wb_ref[...])
    l0, l1, l2 = (token_major(ref, r) for ref, r in zip((l0_ref, l1_ref, l2_ref), dils))
    mx = jnp.maximum(jnp.maximum(l0, l1), l2)
    e0, e1, e2 = jnp.exp(l0 - mx), jnp.exp(l1 - mx), jnp.exp(l2 - mx)
    o0, o1, o2 = (token_major(ref, r) for ref, r in zip((oc0_ref, oc1_ref, oc2_ref), dils))
    oc = (e0 * o0 + e1 * o1 + e2 * o2) / (e0 + e1 + e2)
    gbias = gbias_ref[...]
    dc = _dot(oc.astype(BF), wc_ref[...])
    merged = (_sigmoid(ga_ref[0].astype(F32) + gbias[0:1]) * da
              + _sigmoid(gb_ref[0].astype(F32) + gbias[1:2]) * db
              + _sigmoid(gc_ref[0].astype(F32) + gbias[2:3]) * dc)
    o_ref[0] = h_ref[0] + _dot(merged.astype(BF), wo_ref[...])


def _merge(h3, oa, ob, ocs, lses, gates, gbias, wa, wb, wc, wo, tm=512):
    b, s, d = h3.shape
    tm = min(tm, s)
    row = lambda n: pl.BlockSpec((1, tm, n), lambda bi, i: (bi, i, 0))
    gate = lambda c: pl.BlockSpec((1, tm, d), lambda bi, i: (bi, i, c))
    views = [pl.BlockSpec((1, tm // r, r * DIL_WIDTH), lambda bi, i: (bi, i, 0)) for _, r in DIL_GROUPS]
    n_buf = 2 * sum(1 for _, r in DIL_GROUPS if r > 1)
    return pl.pallas_call(
        _merge_body,
        grid=(b, s // tm),
        in_specs=[row(d), row(SB_WIDTH), row(SB_WIDTH)] + views + views
        + [gate(0), gate(1), gate(2), _resident(gbias.shape),
           _resident(wa.shape), _resident(wb.shape), _resident(wc.shape), _resident(wo.shape)],
        out_specs=row(d),
        out_shape=jax.ShapeDtypeStruct((b, s, d), F32),
        scratch_shapes=[pltpu.VMEM((DIL_WIDTH // LANES, tm, LANES), F32)] * n_buf,
        compiler_params=_cparams(("parallel", "parallel")),
        name="merge",
    )(h3, oa, ob, *ocs, *lses, gates, gates, gates, gbias, wa, wb, wc, wo)


def _memkv_body(m_ref, g_ref, w_ref, o_ref):
    u = _rms(m_ref[0], g_ref[...]).astype(BF)
    o_ref[0] = _dot(u, w_ref[...]).astype(o_ref.dtype)


def _mem_kv(mem, g, w):
    b, m, d = mem.shape
    n = w.shape[1]
    return pl.pallas_call(
        _memkv_body,
        grid=(b,),
        in_specs=[pl.BlockSpec((1, m, d), lambda i: (i, 0, 0)),
                  pl.BlockSpec((1, d), lambda i: (0, 0)),
                  pl.BlockSpec((d, n), lambda i: (0, 0))],
        out_specs=pl.BlockSpec((1, m, n), lambda i: (i, 0, 0)),
        out_shape=jax.ShapeDtypeStruct((b, m, n), BF),
        compiler_params=_cparams(("parallel",)),
        name="mem_kv",
    )(mem, g, w)


def _xattn_body(h_ref, g_ref, kv_ref, wq_ref, wo_ref, o_ref):
    x = h_ref[0]
    u = _rms(x, g_ref[...]).astype(BF)
    q = _dot(u, wq_ref[...]).astype(BF)
    kv = kv_ref[0]
    width = X_HEADS * X_HEAD_DIM
    scale = X_HEAD_DIM ** -0.5
    outs = []
    for h in range(X_HEADS):
        cols = slice(h * X_HEAD_DIM, (h + 1) * X_HEAD_DIM)
        s = _dot_t(q[:, cols], kv[:, cols]) * scale
        m = jnp.max(s, axis=-1, keepdims=True)
        e = jnp.exp(s - m)
        p = e / jnp.sum(e, axis=-1, keepdims=True)
        outs.append(_dot(p.astype(BF), kv[:, width + h * X_HEAD_DIM:width + (h + 1) * X_HEAD_DIM]))
    o = jnp.concatenate(outs, axis=-1).astype(BF)
    o_ref[0] = x + _dot(o, wo_ref[...])


def _xattn(h3, g, kv, wq, wo, tm=512):
    b, s, d = h3.shape
    tm = min(tm, s)
    m = kv.shape[1]
    return pl.pallas_call(
        _xattn_body,
        grid=(b, s // tm),
        in_specs=[pl.BlockSpec((1, tm, d), lambda bi, i: (bi, i, 0)),
                  pl.BlockSpec((1, d), lambda bi, i: (0, 0)),
                  pl.BlockSpec((1, m, kv.shape[2]), lambda bi, i: (bi, 0, 0)),
                  pl.BlockSpec(wq.shape, lambda bi, i: (0, 0)),
                  pl.BlockSpec(wo.shape, lambda bi, i: (0, 0))],
        out_specs=pl.BlockSpec((1, tm, d), lambda bi, i: (bi, i, 0)),
        out_shape=jax.ShapeDtypeStruct((b, s, d), F32),
        compiler_params=_cparams(("parallel", "parallel")),
        name="xattn",
    )(h3, g, kv, wq, wo)


def _rel_bucket(dist):
    exact = REL_BUCKETS // 2
    d = jnp.maximum(dist, exact).astype(F32)
    large = exact + (jnp.log(d / exact) / math.log(REL_MAX_DIST / exact)
                     * (REL_BUCKETS - exact)).astype(jnp.int32)
    return jnp.where(dist < exact, dist, jnp.minimum(large, REL_BUCKETS - 1))


def _band_bias(rel_bias, group, dilation):
    heads = DIL_WIDTH // HEAD_DIM
    qi = jnp.arange(BAND)
    kj = jnp.arange(2 * BAND)
    steps = (qi[:, None] + BAND) - kj[None, :]
    table = rel_bias[:, group * heads:(group + 1) * heads].astype(F32)
    bucket = _rel_bucket(jnp.clip(steps, 0, BAND) * dilation)
    onehot = (bucket[None] == jnp.arange(REL_BUCKETS)[:, None, None]).astype(F32)
    return jnp.einsum('bh,bqk->hqk', table * LOG2E, onehot, precision=lax.Precision.HIGHEST)


def _swap_halves(w, width):
    k, n = w.shape
    w = w.reshape(k, n // width, 2, width // 2)
    return w[:, :, ::-1, :].reshape(k, n)


def _pack_w_in(w_in):
    d = w_in.shape[0]
    qscale = HEAD_DIM ** -0.5 * LOG2E
    sb = jnp.concatenate([w_in[:, :SB_WIDTH] * qscale, w_in[:, SB_WIDTH:1536]], axis=1)
    cq = w_in[:, 1536:1920]
    ckv = w_in[:, 1920:2048]
    kr = w_in[:, 2048:2080]
    zeros = jnp.zeros((d, 64), w_in.dtype)
    kr_sw = _swap_halves(kr, MLA_ROPE)
    seg_a = jnp.concatenate([sb, cq, ckv, kr, kr, zeros, kr_sw, kr_sw, zeros], axis=1)
    group = 3 * DIL_WIDTH
    seg_d = []
    for g in range(len(DIL_GROUPS)):
        lo = 2080 + g * group
        seg_d.append(jnp.concatenate([w_in[:, lo:lo + DIL_WIDTH] * qscale, w_in[:, lo + DIL_WIDTH:lo + group]],
                                     axis=1).astype(BF))
    seg_g = w_in[:, 2080 + len(DIL_GROUPS) * group:]
    return seg_a.astype(BF), seg_d, seg_g.astype(BF)


def _pack_w_uq(w_uq):
    k = w_uq.shape[0]
    w = w_uq.reshape(k, MLA_HEADS, MLA_NOPE + MLA_ROPE)
    nope = w[:, :, :MLA_NOPE].reshape(k, MLA_HEADS * MLA_NOPE)
    rope = w[:, :, MLA_NOPE:]

    def pair_layout(rp):
        rp = rp.reshape(k, MLA_HEADS // 2, 2 * MLA_ROPE)
        pad = jnp.zeros((k, MLA_HEADS // 2, 128 - 2 * MLA_ROPE), rp.dtype)
        return jnp.concatenate([rp, pad], axis=-1).reshape(k, (MLA_HEADS // 2) * 128)

    rope_sw = rope.reshape(k, MLA_HEADS, 2, MLA_ROPE // 2)[:, :, ::-1, :].reshape(k, MLA_HEADS, MLA_ROPE)
    return jnp.concatenate([nope, pair_layout(rope), pair_layout(rope_sw)], axis=1).astype(BF)


def _pack_w_ukv(w_ukv):
    k = w_ukv.shape[0]
    w = w_ukv.reshape(k, MLA_HEADS, MLA_NOPE + MLA_V)
    return jnp.concatenate([w[:, :, :MLA_NOPE].reshape(k, -1), w[:, :, MLA_NOPE:].reshape(k, -1)],
                           axis=1).astype(BF)


def _rope_tables(seq):
    half = MLA_ROPE // 2
    freqs = ROPE_THETA ** (-jnp.arange(half, dtype=F32) / half)
    ang = jnp.arange(seq).astype(F32)[:, None] * freqs[None, :]
    cos, sin = jnp.cos(ang), jnp.sin(ang)
    pad = jnp.zeros((seq, 128 - 2 * MLA_ROPE), F32)
    cos_t = jnp.concatenate([cos, cos, cos, cos, pad], axis=1)
    sin_t = jnp.concatenate([-sin, sin, -sin, sin, pad], axis=1)
    return cos_t, sin_t


def kernel(x, mem, ffn1_norm, ffn1_w_gate, ffn1_w_up, ffn1_w_down, mix_norm, w_in, gate_bias, mla_q_norm, mla_w_uq, mla_kv_norm, mla_w_ukv, w_branch_a, w_branch_b, w_branch_c, w_mix_out, rel_bias, xattn_norm, mem_norm, xattn_w_q, xattn_w_kv, xattn_w_o, ffn2_norm, ffn2_w_gate, ffn2_w_up, ffn2_w_down, final_norm):
    b, s, d = x.shape
    t = b * s
    depth = w_in.shape[0]
    cos_t, sin_t = _rope_tables(s)
    biases = [_band_bias(rel_bias, g, dil) for g, (_, dil) in enumerate(DIL_GROUPS)]
    row = lambda v: v.reshape(1, -1)
    bf = lambda w: w.astype(BF)
    fg = row(final_norm)

    h = x.reshape(t, d)
    for l in range(depth):
        h = _ffn(h, row(ffn1_norm[l]), bf(ffn1_w_gate[l]), bf(ffn1_w_up[l]), bf(ffn1_w_down[l]), fg, False)

        w_a, w_d, w_g = _pack_w_in(w_in[l])
        g_mix = row(mix_norm[l])
        h3 = h.reshape(b, s, d)
        seg_a = _norm_proj(h, g_mix, w_a, tn=PROJ_CHUNK)
        seg_g = _norm_proj(h, g_mix, w_g, tn=PROJ_CHUNK)

        a3 = seg_a.reshape(b, s, seg_a.shape[1])
        o_a = _sb_attention(a3)
        qp, kp, v = _mla_prep(a3, cos_t, sin_t, row(mla_q_norm[l]), row(mla_kv_norm[l]),
                              _pack_w_uq(mla_w_uq[l]), _pack_w_ukv(mla_w_ukv[l]))
        o_b = _mla_attention(qp, kp, v)
        ocs, lses = [], []
        for g, (_, dil) in enumerate(DIL_GROUPS):
            o, lse = _dil_attention(_dil_proj(h3, g_mix, w_d[g], dil), biases[g], dil)
            ocs.append(o)
            lses.append(lse)
        h3 = _merge(h3, o_a, o_b, ocs, lses, seg_g.reshape(b, s, seg_g.shape[1]), gate_bias[l],
                    bf(w_branch_a[l]), bf(w_branch_b[l]), bf(w_branch_c[l]), bf(w_mix_out[l]))

        kv = _mem_kv(mem, row(mem_norm[l]), bf(xattn_w_kv[l]))
        h = _xattn(h3, row(xattn_norm[l]), kv, bf(xattn_w_q[l]), bf(xattn_w_o[l])).reshape(t, d)

        h = _ffn(h, row(ffn2_norm[l]), bf(ffn2_w_gate[l]), bf(ffn2_w_up[l]), bf(ffn2_w_down[l]), fg,
                 l == depth - 1)
    return h.reshape(b, s, d)
```

```python
import functools
import math

import jax
import jax.numpy as jnp
import numpy as np
from jax import lax
from jax.experimental import pallas as pl
from jax.experimental.pallas import tpu as pltpu

BF = jnp.bfloat16
F32 = jnp.float32

EPS = 1e-6
NEG = -1e30
FFN_RESIDUAL = 0.5

SB_WIDTH = 512
MLA_HEADS = 8
MLA_Q_RANK = 384
MLA_KV_RANK = 128
MLA_NOPE = 64
MLA_ROPE = 32
MLA_V = 64
ROPE_THETA = 10000.0
DIL_GROUPS = ((128, 1), (512, 4), (2048, 16))
DIL_WIDTH = 512
REL_BUCKETS = 32
REL_MAX_DIST = 2048
X_HEADS = 4
X_HEAD_DIM = 128
HEAD_DIM = 64
BAND = 128
LANES = 128
SB_DEAD_LOG2 = -150.0
LOG2E = math.log2(math.e)
LN2 = math.log(2.0)

V7X_VMEM_LIMIT = 48 * 1024 * 1024
FF_CHUNK = 256
PROJ_CHUNK = 768


def _cparams(sem):
    return pltpu.CompilerParams(dimension_semantics=sem, vmem_limit_bytes=V7X_VMEM_LIMIT)


def _rms(x, g):
    ms = jnp.mean(x * x, axis=-1, keepdims=True)
    return x * lax.rsqrt(ms + EPS) * g


def _sigmoid(x):
    return 0.5 * jnp.tanh(0.5 * x) + 0.5


def _dot(a, b):
    return jnp.dot(a, b, preferred_element_type=F32)


def _dot_t(a, b):
    return lax.dot_general(a, b, (((1,), (1,)), ((), ())), preferred_element_type=F32)


def _resident(shape):
    nd = len(shape)
    return pl.BlockSpec(shape, lambda *_: (0,) * nd, pipeline_mode=pl.Buffered(1))


def _ffn_body(h_ref, g_ref, wg_ref, wu_ref, wd_ref, fg_ref, o_ref, act_ref, *, final):
    x = h_ref[...]
    u = _rms(x, g_ref[...]).astype(BF)
    d_ff = wg_ref.shape[1]
    for c in range(d_ff // FF_CHUNK):
        sl = slice(c * FF_CHUNK, (c + 1) * FF_CHUNK)
        a = _dot(u, wg_ref[:, sl])
        b = _dot(u, wu_ref[:, sl])
        act_ref[:, sl] = (a * jax.nn.sigmoid(a) * b).astype(BF)
    y = x + FFN_RESIDUAL * _dot(act_ref[...], wd_ref[...])
    if final:
        y = _rms(y, fg_ref[...])
    o_ref[...] = y


def _ffn(h, g, wg, wu, wd, fg, final, tm=512):
    t, d = h.shape
    d_ff = wg.shape[1]
    tm = min(tm, t)
    return pl.pallas_call(
        functools.partial(_ffn_body, final=final),
        grid=(t // tm,),
        in_specs=[
            pl.BlockSpec((tm, d), lambda i: (i, 0)),
            _resident((1, d)),
            _resident((d, d_ff)),
            _resident((d, d_ff)),
            _resident((d_ff, d)),
            _resident((1, d)),
        ],
        out_specs=pl.BlockSpec((tm, d), lambda i: (i, 0)),
        out_shape=jax.ShapeDtypeStruct((t, d), F32),
        scratch_shapes=[pltpu.VMEM((tm, d_ff), BF)],
        compiler_params=_cparams(("parallel",)),
        name="ffn",
    )(h, g, wg, wu, wd, fg)


def _proj_body(h_ref, g_ref, w_ref, o_ref, *, tn):
    u = _rms(h_ref[...], g_ref[...]).astype(BF)
    for c in range(w_ref.shape[1] // tn):
        sl = slice(c * tn, (c + 1) * tn)
        o_ref[:, sl] = _dot(u, w_ref[:, sl]).astype(o_ref.dtype)


def _norm_proj(h, g, w, tn, tm=512):
    t, d = h.shape
    n = w.shape[1]
    tm = min(tm, t)
    return pl.pallas_call(
        functools.partial(_proj_body, tn=tn),
        grid=(t // tm,),
        in_specs=[
            pl.BlockSpec((tm, d), lambda i: (i, 0)),
            _resident((1, d)),
            _resident((d, n)),
        ],
        out_specs=pl.BlockSpec((tm, n), lambda i: (i, 0)),
        out_shape=jax.ShapeDtypeStruct((t, n), BF),
        compiler_params=_cparams(("parallel",)),
        name="norm_proj",
    )(h, g, w)


def _dil_proj_body(h_ref, g_ref, w_ref, o_ref, *stage, r):
    rows = h_ref.shape[1] // r
    n = w_ref.shape[1]
    u = _rms(h_ref[0], g_ref[...])
    if r == 1:
        ub = u.astype(BF)
        for c in range(n // PROJ_CHUNK):
            sl = slice(c * PROJ_CHUNK, (c + 1) * PROJ_CHUNK)
            o_ref[0, :, sl] = _dot(ub, w_ref[:, sl]).astype(o_ref.dtype)
        return
    (stage,) = stage
    chunks = u.shape[1] // LANES
    for c in range(chunks):
        stage[c] = u[:, c * LANES:(c + 1) * LANES]
    for res in range(r):
        ur = jnp.concatenate([stage[c, pl.ds(res, rows, stride=r), :] for c in range(chunks)], axis=1)
        o_ref[0, :, res * n:(res + 1) * n] = _dot(ur.astype(BF), w_ref[...]).astype(o_ref.dtype)


def _dil_proj(h3, g, w, r):
    b, s, d = h3.shape
    n = w.shape[1]
    tm = min(s, max(1024, BAND * r) if r > 1 else 512)
    return pl.pallas_call(
        functools.partial(_dil_proj_body, r=r),
        grid=(b, s // tm),
        in_specs=[
            pl.BlockSpec((1, tm, d), lambda bi, i: (bi, i, 0)),
            pl.BlockSpec((1, d), lambda bi, i: (0, 0)),
            pl.BlockSpec((d, n), lambda bi, i: (0, 0)),
        ],
        out_specs=pl.BlockSpec((1, tm // r, r * n), lambda bi, i: (bi, i, 0)),
        out_shape=jax.ShapeDtypeStruct((b, s // r, r * n), BF),
        scratch_shapes=[pltpu.VMEM((d // LANES, tm, LANES), F32)] if r > 1 else [],
        compiler_params=_cparams(("parallel", "parallel")),
        name=f"dil_proj_r{r}",
    )(h3, g, w)


def _sb_body(q_ref, k_ref, v_ref, o_ref, *, tq, pairs):
    i = pl.program_id(2)
    pw = 2 * HEAD_DIM
    lane = lax.broadcasted_iota(jnp.int32, (1, pw), 1)
    first = lane < HEAD_DIM
    q_heads = []
    for p in range(pairs):
        q = q_ref[0, :, p * pw:(p + 1) * pw]
        zero = jnp.zeros_like(q)
        q_heads += [jnp.where(first, q, zero), jnp.where(first, zero, q)]
    row = lax.broadcasted_iota(jnp.int32, (tq, tq), 0)
    col = lax.broadcasted_iota(jnp.int32, (tq, tq), 1)
    strict = col < row
    later = jnp.where(row > col, 1.0, 0.0).astype(BF)

    def block(j, accs, rs, diag):
        start = pl.multiple_of(j * tq, tq)
        heads = range(2 * pairs)
        ks = [k_ref[0, pl.ds(start, tq), p * pw:(p + 1) * pw] for p in range(pairs)]
        vs = [v_ref[0, pl.ds(start, tq), p * pw:(p + 1) * pw] for p in range(pairs)]
        zs = [_dot_t(q_heads[h], ks[h // 2]) for h in heads]
        log_betas, log_keeps = [], []
        for z in zs:
            sp = jnp.log(1.0 + jnp.exp2(-jnp.abs(z))) * LOG2E
            log_beta = jnp.minimum(z, 0.0) - sp
            log_keep = log_beta - z
            if diag:
                log_keep = jnp.where(strict, log_keep, 0.0)
            log_betas.append(log_beta)
            log_keeps.append(log_keep)
        afters = [_dot(log_keeps[h].astype(BF), later) for h in heads]
        ws = []
        for h in heads:
            w = jnp.exp2(log_betas[h] + afters[h] + rs[h])
            if diag:
                w = jnp.where(strict, w, 0.0)
            ws.append(w.astype(BF))
        outs = [_dot(ws[h], vs[h // 2]) for h in heads]
        new_rs = [rs[h] + jnp.sum(log_keeps[h], axis=-1, keepdims=True) for h in heads]
        new_accs = [accs[p] + jnp.where(first, outs[2 * p], outs[2 * p + 1]) for p in range(pairs)]
        return tuple(new_accs), tuple(new_rs)

    def remaining(rs):
        return jnp.max(functools.reduce(jnp.maximum, rs))

    def cond(c):
        return (c[0] < i) & (c[1] > SB_DEAD_LOG2)

    def body(c):
        accs, rs = block(i - 1 - c[0], c[2], c[3], False)
        return c[0] + 1, remaining(rs), accs, rs

    accs = tuple(jnp.zeros((tq, pw), F32) for _ in range(pairs))
    rs = tuple(jnp.zeros((tq, 1), F32) for _ in range(2 * pairs))
    accs, rs = block(i, accs, rs, True)
    out = lax.while_loop(cond, body, (jnp.int32(0), remaining(rs), accs, rs))
    for p in range(pairs):
        o_ref[0, :, p * pw:(p + 1) * pw] = out[2][p].astype(o_ref.dtype)


def _sb_attention(a3, tq=256, pairs=4):
    b, s, _ = a3.shape
    tq = min(tq, s)
    w = 2 * HEAD_DIM * pairs
    groups = SB_WIDTH // w
    return pl.pallas_call(
        functools.partial(_sb_body, tq=tq, pairs=pairs),
        grid=(b, groups, s // tq),
        in_specs=[
            pl.BlockSpec((1, tq, w), lambda bi, p, i: (bi, i, p)),
            pl.BlockSpec((1, s, w), lambda bi, p, i: (bi, 0, groups + p)),
            pl.BlockSpec((1, s, w), lambda bi, p, i: (bi, 0, 2 * groups + p)),
        ],
        out_specs=pl.BlockSpec((1, tq, w), lambda bi, p, i: (bi, i, p)),
        out_shape=jax.ShapeDtypeStruct((b, s, SB_WIDTH), BF),
        compiler_params=_cparams(("parallel", "parallel", "arbitrary")),
        name="sb_attn",
    )(a3, a3, a3)


def _mla_prep_body(cq_ref, ckv_ref, kr_ref, cos_ref, sin_ref, qg_ref, kvg_ref, wq_ref, wkv_ref,
                   qp_ref, kp_ref, v_ref, *, scale):
    nq = _rms(cq_ref[0].astype(F32), qg_ref[...]).astype(BF)
    q = _dot(nq, wq_ref[...])
    nkv = _rms(ckv_ref[0].astype(F32), kvg_ref[...]).astype(BF)
    kv = _dot(nkv, wkv_ref[...])
    cos = cos_ref[...]
    sin = sin_ref[...]
    kr = kr_ref[0].astype(F32)
    k_pe = (kr[:, :128] * cos + kr[:, 128:] * sin).astype(BF)
    width = q.shape[1] // 3
    for p in range(width // 128):
        lo, hi = p * 128, (p + 1) * 128
        qp_ref[0, :, 2 * lo:2 * lo + 128] = (q[:, lo:hi] * scale).astype(BF)
        q_pe = q[:, width + lo:width + hi] * cos + q[:, 2 * width + lo:2 * width + hi] * sin
        qp_ref[0, :, 2 * lo + 128:2 * hi] = (q_pe * scale).astype(BF)
        kp_ref[0, :, 2 * lo:2 * lo + 128] = kv[:, lo:hi].astype(BF)
        kp_ref[0, :, 2 * lo + 128:2 * hi] = k_pe
    v_ref[0] = kv[:, width:].astype(BF)


def _mla_prep(a3, cos, sin, qg, kvg, wq, wkv, tm=512):
    b, s, _ = a3.shape
    tm = min(tm, s)
    width = MLA_HEADS * MLA_NOPE
    scale = (MLA_NOPE + MLA_ROPE) ** -0.5 * LOG2E
    out_sds = lambda n: jax.ShapeDtypeStruct((b, s, n), BF)
    return pl.pallas_call(
        functools.partial(_mla_prep_body, scale=scale),
        grid=(b, s // tm),
        in_specs=[
            pl.BlockSpec((1, tm, MLA_Q_RANK), lambda bi, i: (bi, i, 1536 // MLA_Q_RANK)),
            pl.BlockSpec((1, tm, MLA_KV_RANK), lambda bi, i: (bi, i, 1920 // MLA_KV_RANK)),
            pl.BlockSpec((1, tm, 256), lambda bi, i: (bi, i, 2048 // 256)),
            pl.BlockSpec((tm, 128), lambda bi, i: (i, 0)),
            pl.BlockSpec((tm, 128), lambda bi, i: (i, 0)),
            pl.BlockSpec((1, MLA_Q_RANK), lambda bi, i: (0, 0)),
            pl.BlockSpec((1, MLA_KV_RANK), lambda bi, i: (0, 0)),
            pl.BlockSpec(wq.shape, lambda bi, i: (0, 0)),
            pl.BlockSpec(wkv.shape, lambda bi, i: (0, 0)),
        ],
        out_specs=[
            pl.BlockSpec((1, tm, 2 * width), lambda bi, i: (bi, i, 0)),
            pl.BlockSpec((1, tm, 2 * width), lambda bi, i: (bi, i, 0)),
            pl.BlockSpec((1, tm, width), lambda bi, i: (bi, i, 0)),
        ],
        out_shape=[out_sds(2 * width), out_sds(2 * width), out_sds(width)],
        compiler_params=_cparams(("parallel", "parallel")),
        name="mla_prep",
    )(a3, a3, a3, cos, sin, qg, kvg, wq, wkv)


def _mla_body(q_ref, k_ref, v_ref, o_ref, *, tq, pairs):
    i = pl.program_id(2)
    lane = lax.broadcasted_iota(jnp.int32, (1, 256), 1)
    sel0 = (lane < 64) | ((lane >= 128) & (lane < 160))
    sel1 = ((lane >= 64) & (lane < 128)) | ((lane >= 160) & (lane < 192))
    q_heads = []
    for p in range(pairs):
        q = q_ref[0, :, p * 256:(p + 1) * 256]
        zero = jnp.zeros_like(q)
        q_heads += [jnp.where(sel0, q, zero), jnp.where(sel1, q, zero)]
    row = lax.broadcasted_iota(jnp.int32, (tq, tq), 0)
    col = lax.broadcasted_iota(jnp.int32, (tq, tq), 1)
    causal = col <= row

    first = lax.broadcasted_iota(jnp.int32, (1, 128), 1) < 64

    def block(j, carry, diag):
        start = pl.multiple_of(j * tq, tq)
        heads = range(2 * pairs)
        ks = [k_ref[0, pl.ds(start, tq), p * 256:(p + 1) * 256] for p in range(pairs)]
        v_heads = []
        for p in range(pairs):
            vj = v_ref[0, pl.ds(start, tq), p * 128:(p + 1) * 128]
            ones = jnp.ones_like(vj)
            v_heads += [jnp.where(first, vj, ones), jnp.where(first, ones, vj)]
        ss = [_dot_t(q_heads[h], ks[h // 2]) for h in heads]
        if diag:
            ss = [jnp.where(causal, s, NEG) for s in ss]
        m_news = [jnp.maximum(carry[h][0], jnp.max(ss[h], axis=-1, keepdims=True)) for h in heads]
        prs = [jnp.exp2(ss[h] - m_news[h]).astype(BF) for h in heads]
        pvs = [_dot(prs[h], v_heads[h]) for h in heads]
        return tuple((m_news[h], jnp.exp2(carry[h][0] - m_news[h]) * carry[h][1] + pvs[h]) for h in heads)

    one = (jnp.full((tq, 1), NEG, F32), jnp.zeros((tq, 128), F32))
    carry = lax.fori_loop(0, i, lambda j, c: block(j, c, False), (one,) * (2 * pairs))
    carry = block(i, carry, True)
    for p in range(pairs):
        a0, a1 = carry[2 * p][1], carry[2 * p + 1][1]
        l = pltpu.roll(jnp.where(first, a1, a0), 64, 1)
        o_ref[0, :, p * 128:(p + 1) * 128] = (jnp.where(first, a0, a1) / l).astype(o_ref.dtype)


def _mla_attention(qp, kp, v, tq=512, pairs=2):
    b, s, _ = qp.shape
    tq = min(tq, s)
    groups = MLA_HEADS // (2 * pairs)
    return pl.pallas_call(
        functools.partial(_mla_body, tq=tq, pairs=pairs),
        grid=(b, groups, s // tq),
        in_specs=[
            pl.BlockSpec((1, tq, 256 * pairs), lambda bi, p, i: (bi, i, p)),
            pl.BlockSpec((1, s, 256 * pairs), lambda bi, p, i: (bi, 0, p)),
            pl.BlockSpec((1, s, 128 * pairs), lambda bi, p, i: (bi, 0, p)),
        ],
        out_specs=pl.BlockSpec((1, tq, 128 * pairs), lambda bi, p, i: (bi, i, p)),
        out_shape=jax.ShapeDtypeStruct((b, s, MLA_HEADS * MLA_V), BF),
        compiler_params=_cparams(("parallel", "parallel", "arbitrary")),
        name="mla_attn",
    )(qp, kp, v)


def _dil_body(q_ref, kc_ref, kp_ref, vc_ref, vp_ref, bias_ref, o_ref, lse_ref, *, tl):
    j = pl.program_id(2)
    lane = lax.broadcasted_iota(jnp.int32, (1, 2 * HEAD_DIM), 1)
    first = lane < HEAD_DIM
    qi = lax.broadcasted_iota(jnp.int32, (BAND, 2 * BAND), 0)
    kj = lax.broadcasted_iota(jnp.int32, (BAND, 2 * BAND), 1)
    steps = qi + BAND - kj
    in_window = (steps >= 0) & (steps <= BAND)
    in_window_first = in_window & ((kj >= BAND) | (j > 0))
    pw = 2 * HEAD_DIM
    pairs = DIL_WIDTH // pw
    heads = range(2 * pairs)
    for sb in range(tl // BAND):
        rows = slice(sb * BAND, (sb + 1) * BAND)
        prev = slice((sb - 1) * BAND, sb * BAND)
        k_prev = kp_ref[0] if sb == 0 else kc_ref[0, prev, :]
        v_prev = vp_ref[0] if sb == 0 else vc_ref[0, prev, :]
        kcat = jnp.concatenate([k_prev, kc_ref[0, rows, :]], axis=0)
        vcat = jnp.concatenate([v_prev, vc_ref[0, rows, :]], axis=0)
        valid = in_window_first if sb == 0 else in_window
        qs = q_ref[0, rows, :]
        q_heads, v_heads = [], []
        for p in range(pairs):
            qb = qs[:, p * pw:(p + 1) * pw]
            vb = vcat[:, p * pw:(p + 1) * pw]
            zero, ones = jnp.zeros_like(qb), jnp.ones_like(vb)
            q_heads += [jnp.where(first, qb, zero), jnp.where(first, zero, qb)]
            v_heads += [jnp.where(first, vb, ones), jnp.where(first, ones, vb)]
        ss = [_dot_t(q_heads[h], kcat[:, (h // 2) * pw:(h // 2 + 1) * pw]) + bias_ref[h] for h in heads]
        ss = [jnp.where(valid, s, NEG) for s in ss]
        ms = [jnp.max(s, axis=-1, keepdims=True) for s in ss]
        prs = [jnp.exp2(ss[h] - ms[h]).astype(BF) for h in heads]
        pvs = [_dot(prs[h], v_heads[h]) for h in heads]
        for p in range(pairs):
            a0, a1 = pvs[2 * p], pvs[2 * p + 1]
            cols = slice(p * pw, (p + 1) * pw)
            l = pltpu.roll(jnp.where(first, a1, a0), HEAD_DIM, 1)
            o_ref[0, rows, cols] = (jnp.where(first, a0, a1) / l).astype(o_ref.dtype)
            lse_ref[0, rows, cols] = jnp.where(first, ms[2 * p], ms[2 * p + 1]) * LN2 + jnp.log(l)


def _dil_attention(view, bias, dilation, tl=512):
    r = dilation
    batch, length, _ = view.shape
    tl = min(tl, length)
    sub = tl // BAND

    def cur(c):
        return pl.BlockSpec((1, tl, DIL_WIDTH), lambda bi, res, j: (bi, j, res * 3 + c))

    def prev(c):
        return pl.BlockSpec((1, BAND, DIL_WIDTH),
                            lambda bi, res, j: (bi, jnp.maximum(j * sub - 1, 0), res * 3 + c))

    out_spec = pl.BlockSpec((1, tl, DIL_WIDTH), lambda bi, res, j: (bi, j, res))
    return pl.pallas_call(
        functools.partial(_dil_body, tl=tl),
        grid=(batch, r, length // tl),
        in_specs=[cur(0), cur(1), prev(1), cur(2), prev(2),
                  pl.BlockSpec(bias.shape, lambda bi, res, j: (0, 0, 0))],
        out_specs=[out_spec, out_spec],
        out_shape=[jax.ShapeDtypeStruct((batch, length, r * DIL_WIDTH), BF),
                   jax.ShapeDtypeStruct((batch, length, r * DIL_WIDTH), F32)],
        compiler_params=_cparams(("parallel", "parallel", "arbitrary")),
        name=f"dil_attn_r{r}",
    )(view, view, view, view, view, bias)


def _merge_body(h_ref, oa_ref, ob_ref, oc0_ref, oc1_ref, oc2_ref, l0_ref, l1_ref, l2_ref,
                gn_ref, wg_ref, gbias_ref, wa_ref, wb_ref, wc_ref, wo_ref, o_ref, *scratch):
    tm, d = h_ref.shape[1:]
    scratch = list(scratch)
    h = h_ref[0]
    u = _rms(h, gn_ref[...]).astype(BF)

    def gate(c):
        return _sigmoid(_dot(u, wg_ref[:, c * d:(c + 1) * d]) + gbias_ref[c:c + 1, :])

    def token_major(ref, r):
        if r == 1:
            return ref[0].astype(F32)
        buf = scratch.pop(0)
        chunks = DIL_WIDTH // LANES
        for res in range(r):
            for c in range(chunks):
                lo = res * DIL_WIDTH + c * LANES
                buf[c, pl.ds(res, tm // r, stride=r), :] = ref[0, :, lo:lo + LANES].astype(F32)
        return jnp.concatenate([buf[c] for c in range(chunks)], axis=1)

    dils = [dil for _, dil in DIL_GROUPS]
    merged = gate(0) * _dot(oa_ref[0], wa_ref[...]) + gate(1) * _dot(ob_ref[0], wb_ref[...])
    gate_c = gate(2)
    l0, l1, l2 = (token_major(ref, r) for ref, r in zip((l0_ref, l1_ref, l2_ref), dils))
    mx = jnp.maximum(jnp.maximum(l0, l1), l2)
    e0, e1, e2 = jnp.exp(l0 - mx), jnp.exp(l1 - mx), jnp.exp(l2 - mx)
    o0, o1, o2 = (token_major(ref, r) for ref, r in zip((oc0_ref, oc1_ref, oc2_ref), dils))
    oc = (e0 * o0 + e1 * o1 + e2 * o2) / (e0 + e1 + e2)
    merged = merged + gate_c * _dot(oc.astype(BF), wc_ref[...])
    o_ref[0] = h + _dot(merged.astype(BF), wo_ref[...])


def _merge(h3, oa, ob, ocs, lses, g_mix, w_gate, gbias, wa, wb, wc, wo, tm=512):
    b, s, d = h3.shape
    tm = min(tm, s)
    row = lambda n: pl.BlockSpec((1, tm, n), lambda bi, i: (bi, i, 0))
    views = [pl.BlockSpec((1, tm // r, r * DIL_WIDTH), lambda bi, i: (bi, i, 0)) for _, r in DIL_GROUPS]
    n_buf = 2 * sum(1 for _, r in DIL_GROUPS if r > 1)
    params = (g_mix, w_gate, gbias, wa, wb, wc, wo)
    return pl.pallas_call(
        _merge_body,
        grid=(b, s // tm),
        in_specs=[row(d), row(SB_WIDTH), row(SB_WIDTH)] + views + views + [_resident(p.shape) for p in params],
        out_specs=row(d),
        out_shape=jax.ShapeDtypeStruct((b, s, d), F32),
        scratch_shapes=[pltpu.VMEM((DIL_WIDTH // LANES, tm, LANES), F32)] * n_buf,
        compiler_params=_cparams(("parallel", "parallel")),
        name="merge",
    )(h3, oa, ob, *ocs, *lses, *params)


def _memkv_body(m_ref, g_ref, w_ref, o_ref):
    u = _rms(m_ref[0], g_ref[...]).astype(BF)
    o_ref[0] = _dot(u, w_ref[...]).astype(o_ref.dtype)


def _mem_kv(mem, g, w):
    b, m, d = mem.shape
    n = w.shape[1]
    return pl.pallas_call(
        _memkv_body,
        grid=(b,),
        in_specs=[pl.BlockSpec((1, m, d), lambda i: (i, 0, 0)),
                  pl.BlockSpec((1, d), lambda i: (0, 0)),
                  pl.BlockSpec((d, n), lambda i: (0, 0))],
        out_specs=pl.BlockSpec((1, m, n), lambda i: (i, 0, 0)),
        out_shape=jax.ShapeDtypeStruct((b, m, n), BF),
        compiler_params=_cparams(("parallel",)),
        name="mem_kv",
    )(mem, g, w)


def _xattn_body(h_ref, g_ref, kv_ref, wq_ref, wo_ref, o_ref):
    x = h_ref[0]
    u = _rms(x, g_ref[...]).astype(BF)
    q = _dot(u, wq_ref[...]).astype(BF)
    kv = kv_ref[0]
    width = X_HEADS * X_HEAD_DIM
    scale = X_HEAD_DIM ** -0.5
    outs = []
    for h in range(X_HEADS):
        cols = slice(h * X_HEAD_DIM, (h + 1) * X_HEAD_DIM)
        s = _dot_t(q[:, cols], kv[:, cols]) * scale
        m = jnp.max(s, axis=-1, keepdims=True)
        e = jnp.exp(s - m)
        p = e / jnp.sum(e, axis=-1, keepdims=True)
        outs.append(_dot(p.astype(BF), kv[:, width + h * X_HEAD_DIM:width + (h + 1) * X_HEAD_DIM]))
    o = jnp.concatenate(outs, axis=-1).astype(BF)
    o_ref[0] = x + _dot(o, wo_ref[...])


def _xattn(h3, g, kv, wq, wo, tm=512):
    b, s, d = h3.shape
    tm = min(tm, s)
    m = kv.shape[1]
    return pl.pallas_call(
        _xattn_body,
        grid=(b, s // tm),
        in_specs=[pl.BlockSpec((1, tm, d), lambda bi, i: (bi, i, 0)),
                  pl.BlockSpec((1, d), lambda bi, i: (0, 0)),
                  pl.BlockSpec((1, m, kv.shape[2]), lambda bi, i: (bi, 0, 0)),
                  pl.BlockSpec(wq.shape, lambda bi, i: (0, 0)),
                  pl.BlockSpec(wo.shape, lambda bi, i: (0, 0))],
        out_specs=pl.BlockSpec((1, tm, d), lambda bi, i: (bi, i, 0)),
        out_shape=jax.ShapeDtypeStruct((b, s, d), F32),
        compiler_params=_cparams(("parallel", "parallel")),
        name="xattn",
    )(h3, g, kv, wq, wo)


def _rel_bucket(dist):
    exact = REL_BUCKETS // 2
    d = jnp.maximum(dist, exact).astype(F32)
    large = exact + (jnp.log(d / exact) / math.log(REL_MAX_DIST / exact)
                     * (REL_BUCKETS - exact)).astype(jnp.int32)
    return jnp.where(dist < exact, dist, jnp.minimum(large, REL_BUCKETS - 1))


def _band_bias(rel_bias, group, dilation):
    heads = DIL_WIDTH // HEAD_DIM
    qi = jnp.arange(BAND)
    kj = jnp.arange(2 * BAND)
    steps = (qi[:, None] + BAND) - kj[None, :]
    table = rel_bias[:, group * heads:(group + 1) * heads].astype(F32)
    bucket = _rel_bucket(jnp.clip(steps, 0, BAND) * dilation)
    onehot = (bucket[None] == jnp.arange(REL_BUCKETS)[:, None, None]).astype(F32)
    return jnp.einsum('bh,bqk->hqk', table * LOG2E, onehot, precision=lax.Precision.HIGHEST)


def _swap_halves(w, width):
    k, n = w.shape
    w = w.reshape(k, n // width, 2, width // 2)
    return w[:, :, ::-1, :].reshape(k, n)


def _pack_w_in(w_in):
    d = w_in.shape[0]
    qscale = HEAD_DIM ** -0.5 * LOG2E
    sb = jnp.concatenate([w_in[:, :SB_WIDTH] * qscale, w_in[:, SB_WIDTH:1536]], axis=1)
    cq = w_in[:, 1536:1920]
    ckv = w_in[:, 1920:2048]
    kr = w_in[:, 2048:2080]
    zeros = jnp.zeros((d, 64), w_in.dtype)
    kr_sw = _swap_halves(kr, MLA_ROPE)
    seg_a = jnp.concatenate([sb, cq, ckv, kr, kr, zeros, kr_sw, kr_sw, zeros], axis=1)
    group = 3 * DIL_WIDTH
    seg_d = []
    for g in range(len(DIL_GROUPS)):
        lo = 2080 + g * group
        seg_d.append(jnp.concatenate([w_in[:, lo:lo + DIL_WIDTH] * qscale, w_in[:, lo + DIL_WIDTH:lo + group]],
                                     axis=1).astype(BF))
    seg_g = w_in[:, 2080 + len(DIL_GROUPS) * group:]
    return seg_a.astype(BF), seg_d, seg_g.astype(BF)


def _pack_w_uq(w_uq):
    k = w_uq.shape[0]
    w = w_uq.reshape(k, MLA_HEADS, MLA_NOPE + MLA_ROPE)
    nope = w[:, :, :MLA_NOPE].reshape(k, MLA_HEADS * MLA_NOPE)
    rope = w[:, :, MLA_NOPE:]

    def pair_layout(rp):
        rp = rp.reshape(k, MLA_HEADS // 2, 2 * MLA_ROPE)
        pad = jnp.zeros((k, MLA_HEADS // 2, 128 - 2 * MLA_ROPE), rp.dtype)
        return jnp.concatenate([rp, pad], axis=-1).reshape(k, (MLA_HEADS // 2) * 128)

    rope_sw = rope.reshape(k, MLA_HEADS, 2, MLA_ROPE // 2)[:, :, ::-1, :].reshape(k, MLA_HEADS, MLA_ROPE)
    return jnp.concatenate([nope, pair_layout(rope), pair_layout(rope_sw)], axis=1).astype(BF)


def _pack_w_ukv(w_ukv):
    k = w_ukv.shape[0]
    w = w_ukv.reshape(k, MLA_HEADS, MLA_NOPE + MLA_V)
    return jnp.concatenate([w[:, :, :MLA_NOPE].reshape(k, -1), w[:, :, MLA_NOPE:].reshape(k, -1)],
                           axis=1).astype(BF)


def _rope_tables(seq):
    half = MLA_ROPE // 2
    freqs = ROPE_THETA ** (-jnp.arange(half, dtype=F32) / half)
    ang = jnp.arange(seq).astype(F32)[:, None] * freqs[None, :]
    cos, sin = jnp.cos(ang), jnp.sin(ang)
    pad = jnp.zeros((seq, 128 - 2 * MLA_ROPE), F32)
    cos_t = jnp.concatenate([cos, cos, cos, cos, pad], axis=1)
    sin_t = jnp.concatenate([-sin, sin, -sin, sin, pad], axis=1)
    return cos_t, sin_t


def kernel(x, mem, ffn1_norm, ffn1_w_gate, ffn1_w_up, ffn1_w_down, mix_norm, w_in, gate_bias, mla_q_norm, mla_w_uq, mla_kv_norm, mla_w_ukv, w_branch_a, w_branch_b, w_branch_c, w_mix_out, rel_bias, xattn_norm, mem_norm, xattn_w_q, xattn_w_kv, xattn_w_o, ffn2_norm, ffn2_w_gate, ffn2_w_up, ffn2_w_down, final_norm):
    b, s, d = x.shape
    t = b * s
    depth = w_in.shape[0]
    cos_t, sin_t = _rope_tables(s)
    biases = [_band_bias(rel_bias, g, dil) for g, (_, dil) in enumerate(DIL_GROUPS)]
    row = lambda v: v.reshape(1, -1)
    bf = lambda w: w.astype(BF)
    fg = row(final_norm)

    h = x.reshape(t, d)
    for l in range(depth):
        h = _ffn(h, row(ffn1_norm[l]), bf(ffn1_w_gate[l]), bf(ffn1_w_up[l]), bf(ffn1_w_down[l]), fg, False)

        w_a, w_d, w_g = _pack_w_in(w_in[l])
        g_mix = row(mix_norm[l])
        h3 = h.reshape(b, s, d)
        seg_a = _norm_proj(h, g_mix, w_a, tn=PROJ_CHUNK)

        a3 = seg_a.reshape(b, s, seg_a.shape[1])
        o_a = _sb_attention(a3)
        qp, kp, v = _mla_prep(a3, cos_t, sin_t, row(mla_q_norm[l]), row(mla_kv_norm[l]),
                              _pack_w_uq(mla_w_uq[l]), _pack_w_ukv(mla_w_ukv[l]))
        o_b = _mla_attention(qp, kp, v)
        ocs, lses = [], []
        for g, (_, dil) in enumerate(DIL_GROUPS):
            o, lse = _dil_attention(_dil_proj(h3, g_mix, w_d[g], dil), biases[g], dil)
            ocs.append(o)
            lses.append(lse)
        h3 = _merge(h3, o_a, o_b, ocs, lses, g_mix, w_g, gate_bias[l],
                    bf(w_branch_a[l]), bf(w_branch_b[l]), bf(w_branch_c[l]), bf(w_mix_out[l]))

        kv = _mem_kv(mem, row(mem_norm[l]), bf(xattn_w_kv[l]))
        h = _xattn(h3, row(xattn_norm[l]), kv, bf(xattn_w_q[l]), bf(xattn_w_o[l])).reshape(t, d)

        h = _ffn(h, row(ffn2_norm[l]), bf(ffn2_w_gate[l]), bf(ffn2_w_up[l]), bf(ffn2_w_down[l]), fg,
                 l == depth - 1)
    return h.reshape(b, s, d)
```

```python
import functools
import math

import jax
import jax.numpy as jnp
import numpy as np
from jax import lax
from jax.experimental import pallas as pl
from jax.experimental.pallas import tpu as pltpu

BF = jnp.bfloat16
F32 = jnp.float32

EPS = 1e-6
NEG = -1e30
FFN_RESIDUAL = 0.5

SB_WIDTH = 512
MLA_HEADS = 8
MLA_Q_RANK = 384
MLA_KV_RANK = 128
MLA_NOPE = 64
MLA_ROPE = 32
MLA_V = 64
ROPE_THETA = 10000.0
DIL_GROUPS = ((128, 1), (512, 4), (2048, 16))
DIL_WIDTH = 512
REL_BUCKETS = 32
REL_MAX_DIST = 2048
X_HEADS = 4
X_HEAD_DIM = 128
HEAD_DIM = 64
BAND = 128
LANES = 128
SB_DEAD_LOG2 = -150.0
LOG2E = math.log2(math.e)
LN2 = math.log(2.0)

V7X_VMEM_LIMIT = 48 * 1024 * 1024
FF_CHUNK = 256
MLA_BLOCK = 512
PROJ_CHUNK = 768


def _cparams(sem):
    return pltpu.CompilerParams(dimension_semantics=sem, vmem_limit_bytes=V7X_VMEM_LIMIT)


def _rms(x, g):
    ms = jnp.mean(x * x, axis=-1, keepdims=True)
    return x * lax.rsqrt(ms + EPS) * g


def _sigmoid(x):
    return 0.5 * jnp.tanh(0.5 * x) + 0.5


def _dot(a, b):
    return jnp.dot(a, b, preferred_element_type=F32)


def _dot_t(a, b):
    return lax.dot_general(a, b, (((1,), (1,)), ((), ())), preferred_element_type=F32)


def _resident(shape):
    nd = len(shape)
    return pl.BlockSpec(shape, lambda *_: (0,) * nd, pipeline_mode=pl.Buffered(1))


def _ffn_body(h_ref, g_ref, wg_ref, wu_ref, wd_ref, fg_ref, o_ref, act_ref, *, final):
    x = h_ref[...]
    u = _rms(x, g_ref[...]).astype(BF)
    d_ff = wg_ref.shape[1]
    for c in range(d_ff // FF_CHUNK):
        sl = slice(c * FF_CHUNK, (c + 1) * FF_CHUNK)
        a = _dot(u, wg_ref[:, sl])
        b = _dot(u, wu_ref[:, sl])
        act_ref[:, sl] = (a * jax.nn.sigmoid(a) * b).astype(BF)
    y = x + FFN_RESIDUAL * _dot(act_ref[...], wd_ref[...])
    if final:
        y = _rms(y, fg_ref[...])
    o_ref[...] = y


def _ffn(h, g, wg, wu, wd, fg, final, tm=512):
    t, d = h.shape
    d_ff = wg.shape[1]
    tm = min(tm, t)
    return pl.pallas_call(
        functools.partial(_ffn_body, final=final),
        grid=(t // tm,),
        in_specs=[
            pl.BlockSpec((tm, d), lambda i: (i, 0)),
            _resident((1, d)),
            _resident((d, d_ff)),
            _resident((d, d_ff)),
            _resident((d_ff, d)),
            _resident((1, d)),
        ],
        out_specs=pl.BlockSpec((tm, d), lambda i: (i, 0)),
        out_shape=jax.ShapeDtypeStruct((t, d), F32),
        scratch_shapes=[pltpu.VMEM((tm, d_ff), BF)],
        compiler_params=_cparams(("parallel",)),
        name="ffn",
    )(h, g, wg, wu, wd, fg)


def _proj_body(h_ref, g_ref, w_ref, o_ref, *, tn):
    u = _rms(h_ref[...], g_ref[...]).astype(BF)
    for c in range(w_ref.shape[1] // tn):
        sl = slice(c * tn, (c + 1) * tn)
        o_ref[:, sl] = _dot(u, w_ref[:, sl]).astype(o_ref.dtype)


def _norm_proj(h, g, w, tn, tm=512):
    t, d = h.shape
    n = w.shape[1]
    tm = min(tm, t)
    return pl.pallas_call(
        functools.partial(_proj_body, tn=tn),
        grid=(t // tm,),
        in_specs=[
            pl.BlockSpec((tm, d), lambda i: (i, 0)),
            _resident((1, d)),
            _resident((d, n)),
        ],
        out_specs=pl.BlockSpec((tm, n), lambda i: (i, 0)),
        out_shape=jax.ShapeDtypeStruct((t, n), BF),
        compiler_params=_cparams(("parallel",)),
        name="norm_proj",
    )(h, g, w)


def _dil_proj_body(h_ref, g_ref, w_ref, o_ref, *stage, r):
    rows = h_ref.shape[1] // r
    n = w_ref.shape[1]
    u = _rms(h_ref[0], g_ref[...])
    if r == 1:
        ub = u.astype(BF)
        for c in range(n // PROJ_CHUNK):
            sl = slice(c * PROJ_CHUNK, (c + 1) * PROJ_CHUNK)
            o_ref[0, :, sl] = _dot(ub, w_ref[:, sl]).astype(o_ref.dtype)
        return
    (stage,) = stage
    chunks = u.shape[1] // LANES
    for c in range(chunks):
        stage[c] = u[:, c * LANES:(c + 1) * LANES]
    for res in range(r):
        ur = jnp.concatenate([stage[c, pl.ds(res, rows, stride=r), :] for c in range(chunks)], axis=1)
        o_ref[0, :, res * n:(res + 1) * n] = _dot(ur.astype(BF), w_ref[...]).astype(o_ref.dtype)


def _dil_proj(h3, g, w, r):
    b, s, d = h3.shape
    n = w.shape[1]
    tm = min(s, max(1024, BAND * r) if r > 1 else 512)
    return pl.pallas_call(
        functools.partial(_dil_proj_body, r=r),
        grid=(b, s // tm),
        in_specs=[
            pl.BlockSpec((1, tm, d), lambda bi, i: (bi, i, 0)),
            pl.BlockSpec((1, d), lambda bi, i: (0, 0)),
            pl.BlockSpec((d, n), lambda bi, i: (0, 0)),
        ],
        out_specs=pl.BlockSpec((1, tm // r, r * n), lambda bi, i: (bi, i, 0)),
        out_shape=jax.ShapeDtypeStruct((b, s // r, r * n), BF),
        scratch_shapes=[pltpu.VMEM((d // LANES, tm, LANES), F32)] if r > 1 else [],
        compiler_params=_cparams(("parallel", "parallel")),
        name=f"dil_proj_r{r}",
    )(h3, g, w)


def _sb_body(q_ref, k_ref, v_ref, o_ref, *, tq, pairs):
    i = pl.program_id(2)
    pw = 2 * HEAD_DIM
    lane = lax.broadcasted_iota(jnp.int32, (1, pw), 1)
    first = lane < HEAD_DIM
    q_heads = []
    for p in range(pairs):
        q = q_ref[0, :, p * pw:(p + 1) * pw]
        zero = jnp.zeros_like(q)
        q_heads += [jnp.where(first, q, zero), jnp.where(first, zero, q)]
    row = lax.broadcasted_iota(jnp.int32, (tq, tq), 0)
    col = lax.broadcasted_iota(jnp.int32, (tq, tq), 1)
    strict = col < row
    later = jnp.where(row > col, 1.0, 0.0).astype(BF)

    def block(j, accs, rs, diag):
        start = pl.multiple_of(j * tq, tq)
        heads = range(2 * pairs)
        ks = [k_ref[0, pl.ds(start, tq), p * pw:(p + 1) * pw] for p in range(pairs)]
        vs = [v_ref[0, pl.ds(start, tq), p * pw:(p + 1) * pw] for p in range(pairs)]
        zs = [_dot_t(q_heads[h], ks[h // 2]) for h in heads]
        log_betas, log_keeps = [], []
        for z in zs:
            sp = jnp.log(1.0 + jnp.exp2(-jnp.abs(z))) * LOG2E
            log_beta = jnp.minimum(z, 0.0) - sp
            log_keep = log_beta - z
            if diag:
                log_keep = jnp.where(strict, log_keep, 0.0)
            log_betas.append(log_beta)
            log_keeps.append(log_keep)
        afters = [_dot(log_keeps[h].astype(BF), later) for h in heads]
        ws = []
        for h in heads:
            w = jnp.exp2(log_betas[h] + afters[h] + rs[h])
            if diag:
                w = jnp.where(strict, w, 0.0)
            ws.append(w.astype(BF))
        outs = [_dot(ws[h], vs[h // 2]) for h in heads]
        new_rs = [rs[h] + jnp.sum(log_keeps[h], axis=-1, keepdims=True) for h in heads]
        new_accs = [accs[p] + jnp.where(first, outs[2 * p], outs[2 * p + 1]) for p in range(pairs)]
        return tuple(new_accs), tuple(new_rs)

    def remaining(rs):
        return jnp.max(functools.reduce(jnp.maximum, rs))

    def cond(c):
        return (c[0] < i) & (c[1] > SB_DEAD_LOG2)

    def body(c):
        accs, rs = block(i - 1 - c[0], c[2], c[3], False)
        return c[0] + 1, remaining(rs), accs, rs

    accs = tuple(jnp.zeros((tq, pw), F32) for _ in range(pairs))
    rs = tuple(jnp.zeros((tq, 1), F32) for _ in range(2 * pairs))
    accs, rs = block(i, accs, rs, True)
    out = lax.while_loop(cond, body, (jnp.int32(0), remaining(rs), accs, rs))
    for p in range(pairs):
        o_ref[0, :, p * pw:(p + 1) * pw] = out[2][p].astype(o_ref.dtype)


def _sb_attention(a3, tq=256, pairs=4):
    b, s, _ = a3.shape
    tq = min(tq, s)
    w = 2 * HEAD_DIM * pairs
    groups = SB_WIDTH // w
    return pl.pallas_call(
        functools.partial(_sb_body, tq=tq, pairs=pairs),
        grid=(b, groups, s // tq),
        in_specs=[
            pl.BlockSpec((1, tq, w), lambda bi, p, i: (bi, i, p)),
            pl.BlockSpec((1, s, w), lambda bi, p, i: (bi, 0, groups + p)),
            pl.BlockSpec((1, s, w), lambda bi, p, i: (bi, 0, 2 * groups + p)),
        ],
        out_specs=pl.BlockSpec((1, tq, w), lambda bi, p, i: (bi, i, p)),
        out_shape=jax.ShapeDtypeStruct((b, s, SB_WIDTH), BF),
        compiler_params=_cparams(("parallel", "parallel", "arbitrary")),
        name="sb_attn",
    )(a3, a3, a3)


def _mla_prep_body(cq_ref, ckv_ref, kr_ref, cos_ref, sin_ref, qg_ref, kvg_ref, wq_ref, wkv_ref,
                   qp_ref, kp_ref, v_ref, *, scale):
    nq = _rms(cq_ref[0].astype(F32), qg_ref[...]).astype(BF)
    q = _dot(nq, wq_ref[...])
    nkv = _rms(ckv_ref[0].astype(F32), kvg_ref[...]).astype(BF)
    kv = _dot(nkv, wkv_ref[...])
    cos = cos_ref[...]
    sin = sin_ref[...]
    kr = kr_ref[0].astype(F32)
    k_pe = (kr[:, :128] * cos + kr[:, 128:] * sin).astype(BF)
    width = q.shape[1] // 3
    for p in range(width // 128):
        lo, hi = p * 128, (p + 1) * 128
        qp_ref[0, :, 2 * lo:2 * lo + 128] = (q[:, lo:hi] * scale).astype(BF)
        q_pe = q[:, width + lo:width + hi] * cos + q[:, 2 * width + lo:2 * width + hi] * sin
        qp_ref[0, :, 2 * lo + 128:2 * hi] = (q_pe * scale).astype(BF)
        kp_ref[0, :, 2 * lo:2 * lo + 128] = kv[:, lo:hi].astype(BF)
        kp_ref[0, :, 2 * lo + 128:2 * hi] = k_pe
    v_ref[0, 0] = kv[:, width:].T.astype(BF)


def _mla_prep(a3, cos, sin, qg, kvg, wq, wkv, tm):
    b, s, _ = a3.shape
    tm = min(tm, s)
    width = MLA_HEADS * MLA_NOPE
    scale = (MLA_NOPE + MLA_ROPE) ** -0.5 * LOG2E
    out_sds = lambda n: jax.ShapeDtypeStruct((b, s, n), BF)
    return pl.pallas_call(
        functools.partial(_mla_prep_body, scale=scale),
        grid=(b, s // tm),
        in_specs=[
            pl.BlockSpec((1, tm, MLA_Q_RANK), lambda bi, i: (bi, i, 1536 // MLA_Q_RANK)),
            pl.BlockSpec((1, tm, MLA_KV_RANK), lambda bi, i: (bi, i, 1920 // MLA_KV_RANK)),
            pl.BlockSpec((1, tm, 256), lambda bi, i: (bi, i, 2048 // 256)),
            pl.BlockSpec((tm, 128), lambda bi, i: (i, 0)),
            pl.BlockSpec((tm, 128), lambda bi, i: (i, 0)),
            pl.BlockSpec((1, MLA_Q_RANK), lambda bi, i: (0, 0)),
            pl.BlockSpec((1, MLA_KV_RANK), lambda bi, i: (0, 0)),
            pl.BlockSpec(wq.shape, lambda bi, i: (0, 0)),
            pl.BlockSpec(wkv.shape, lambda bi, i: (0, 0)),
        ],
        out_specs=[
            pl.BlockSpec((1, tm, 2 * width), lambda bi, i: (bi, i, 0)),
            pl.BlockSpec((1, tm, 2 * width), lambda bi, i: (bi, i, 0)),
            pl.BlockSpec((1, 1, width, tm), lambda bi, i: (bi, i, 0, 0)),
        ],
        out_shape=[out_sds(2 * width), out_sds(2 * width),
                   jax.ShapeDtypeStruct((b, s // tm, width, tm), BF)],
        compiler_params=_cparams(("parallel", "parallel")),
        name="mla_prep",
    )(a3, a3, a3, cos, sin, qg, kvg, wq, wkv)


def _mla_body(q_ref, k_ref, v_ref, o_ref, *, tq, pairs):
    i = pl.program_id(2)
    lane = lax.broadcasted_iota(jnp.int32, (1, 256), 1)
    sel0 = (lane < 64) | ((lane >= 128) & (lane < 160))
    sel1 = ((lane >= 64) & (lane < 128)) | ((lane >= 160) & (lane < 192))
    q_heads = []
    for p in range(pairs):
        q = q_ref[0, :, p * 256:(p + 1) * 256]
        zero = jnp.zeros_like(q)
        q_heads += [jnp.where(sel0, q, zero), jnp.where(sel1, q, zero)]
    key = lax.broadcasted_iota(jnp.int32, (tq, tq), 0)
    qry = lax.broadcasted_iota(jnp.int32, (tq, tq), 1)
    causal = key <= qry
    first = lax.broadcasted_iota(jnp.int32, (128, 1), 0) < 64

    def block(j, carry, diag):
        start = pl.multiple_of(j * tq, tq)
        heads = range(2 * pairs)
        ks = [k_ref[0, pl.ds(start, tq), p * 256:(p + 1) * 256] for p in range(pairs)]
        v_heads = []
        for p in range(pairs):
            vt = v_ref[0, j, p * 128:(p + 1) * 128, :]
            ones = jnp.ones_like(vt)
            v_heads += [jnp.where(first, vt, ones), jnp.where(first, ones, vt)]
        ss = [_dot_t(ks[h // 2], q_heads[h]) for h in heads]
        if diag:
            ss = [jnp.where(causal, s, NEG) for s in ss]
        m_news = [jnp.maximum(carry[h][0], jnp.max(ss[h], axis=0, keepdims=True)) for h in heads]
        prs = [jnp.exp2(ss[h] - m_news[h]).astype(BF) for h in heads]
        pvs = [_dot(v_heads[h], prs[h]) for h in heads]
        return tuple((m_news[h], jnp.exp2(carry[h][0] - m_news[h]) * carry[h][1] + pvs[h]) for h in heads)

    one = (jnp.full((1, tq), NEG, F32), jnp.zeros((128, tq), F32))
    carry = lax.fori_loop(0, i, lambda j, c: block(j, c, False), (one,) * (2 * pairs))
    carry = block(i, carry, True)
    for p in range(pairs):
        a0, a1 = carry[2 * p][1], carry[2 * p + 1][1]
        lsw = jnp.where(first, a1, a0)
        l = jnp.concatenate([lsw[64:], lsw[:64]], axis=0)
        out_t = jnp.where(first, a0, a1) / l
        o_ref[0, :, p * 128:(p + 1) * 128] = out_t.T.astype(o_ref.dtype)


def _mla_attention(qp, kp, vt, pairs=2):
    b, s, _ = qp.shape
    nblk, width, tq = vt.shape[1:]
    groups = MLA_HEADS // (2 * pairs)
    return pl.pallas_call(
        functools.partial(_mla_body, tq=tq, pairs=pairs),
        grid=(b, groups, s // tq),
        in_specs=[
            pl.BlockSpec((1, tq, 256 * pairs), lambda bi, p, i: (bi, i, p)),
            pl.BlockSpec((1, s, 256 * pairs), lambda bi, p, i: (bi, 0, p)),
            pl.BlockSpec((1, nblk, 128 * pairs, tq), lambda bi, p, i: (bi, 0, p, 0)),
        ],
        out_specs=pl.BlockSpec((1, tq, 128 * pairs), lambda bi, p, i: (bi, i, p)),
        out_shape=jax.ShapeDtypeStruct((b, s, MLA_HEADS * MLA_V), BF),
        compiler_params=_cparams(("parallel", "parallel", "arbitrary")),
        name="mla_attn",
    )(qp, kp, vt)


def _dil_body(q_ref, kc_ref, kp_ref, vc_ref, vp_ref, bias_ref, o_ref, lse_ref, *, tl):
    j = pl.program_id(2)
    lane = lax.broadcasted_iota(jnp.int32, (1, 2 * HEAD_DIM), 1)
    first = lane < HEAD_DIM
    qi = lax.broadcasted_iota(jnp.int32, (BAND, 2 * BAND), 0)
    kj = lax.broadcasted_iota(jnp.int32, (BAND, 2 * BAND), 1)
    steps = qi + BAND - kj
    in_window = (steps >= 0) & (steps <= BAND)
    in_window_first = in_window & ((kj >= BAND) | (j > 0))
    pw = 2 * HEAD_DIM
    pairs = DIL_WIDTH // pw
    heads = range(2 * pairs)
    for sb in range(tl // BAND):
        rows = slice(sb * BAND, (sb + 1) * BAND)
        prev = slice((sb - 1) * BAND, sb * BAND)
        k_prev = kp_ref[0] if sb == 0 else kc_ref[0, prev, :]
        v_prev = vp_ref[0] if sb == 0 else vc_ref[0, prev, :]
        kcat = jnp.concatenate([k_prev, kc_ref[0, rows, :]], axis=0)
        vcat = jnp.concatenate([v_prev, vc_ref[0, rows, :]], axis=0)
        valid = in_window_first if sb == 0 else in_window
        qs = q_ref[0, rows, :]
        q_heads, v_heads = [], []
        for p in range(pairs):
            qb = qs[:, p * pw:(p + 1) * pw]
            vb = vcat[:, p * pw:(p + 1) * pw]
            zero, ones = jnp.zeros_like(qb), jnp.ones_like(vb)
            q_heads += [jnp.where(first, qb, zero), jnp.where(first, zero, qb)]
            v_heads += [jnp.where(first, vb, ones), jnp.where(first, ones, vb)]
        ss = [_dot_t(q_heads[h], kcat[:, (h // 2) * pw:(h // 2 + 1) * pw]) + bias_ref[h] for h in heads]
        ss = [jnp.where(valid, s, NEG) for s in ss]
        ms = [jnp.max(s, axis=-1, keepdims=True) for s in ss]
        prs = [jnp.exp2(ss[h] - ms[h]).astype(BF) for h in heads]
        pvs = [_dot(prs[h], v_heads[h]) for h in heads]
        for p in range(pairs):
            a0, a1 = pvs[2 * p], pvs[2 * p + 1]
            cols = slice(p * pw, (p + 1) * pw)
            l = pltpu.roll(jnp.where(first, a1, a0), HEAD_DIM, 1)
            o_ref[0, rows, cols] = (jnp.where(first, a0, a1) / l).astype(o_ref.dtype)
            lse_ref[0, rows, cols] = jnp.where(first, ms[2 * p], ms[2 * p + 1]) * LN2 + jnp.log(l)


def _dil_attention(view, bias, dilation, tl=512):
    r = dilation
    batch, length, _ = view.shape
    tl = min(tl, length)
    sub = tl // BAND

    def cur(c):
        return pl.BlockSpec((1, tl, DIL_WIDTH), lambda bi, res, j: (bi, j, res * 3 + c))

    def prev(c):
        return pl.BlockSpec((1, BAND, DIL_WIDTH),
                            lambda bi, res, j: (bi, jnp.maximum(j * sub - 1, 0), res * 3 + c))

    out_spec = pl.BlockSpec((1, tl, DIL_WIDTH), lambda bi, res, j: (bi, j, res))
    return pl.pallas_call(
        functools.partial(_dil_body, tl=tl),
        grid=(batch, r, length // tl),
        in_specs=[cur(0), cur(1), prev(1), cur(2), prev(2),
                  pl.BlockSpec(bias.shape, lambda bi, res, j: (0, 0, 0))],
        out_specs=[out_spec, out_spec],
        out_shape=[jax.ShapeDtypeStruct((batch, length, r * DIL_WIDTH), BF),
                   jax.ShapeDtypeStruct((batch, length, r * DIL_WIDTH), F32)],
        compiler_params=_cparams(("parallel", "parallel", "arbitrary")),
        name=f"dil_attn_r{r}",
    )(view, view, view, view, view, bias)


def _merge_body(h_ref, oa_ref, ob_ref, oc0_ref, oc1_ref, oc2_ref, l0_ref, l1_ref, l2_ref,
                gn_ref, wg_ref, gbias_ref, wa_ref, wb_ref, wc_ref, wo_ref, o_ref, *scratch):
    tm, d = h_ref.shape[1:]
    scratch = list(scratch)
    h = h_ref[0]
    u = _rms(h, gn_ref[...]).astype(BF)

    def gate(c):
        return _sigmoid(_dot(u, wg_ref[:, c * d:(c + 1) * d]) + gbias_ref[c:c + 1, :])

    def token_major(ref, r):
        if r == 1:
            return ref[0].astype(F32)
        buf = scratch.pop(0)
        chunks = DIL_WIDTH // LANES
        for res in range(r):
            for c in range(chunks):
                lo = res * DIL_WIDTH + c * LANES
                buf[c, pl.ds(res, tm // r, stride=r), :] = ref[0, :, lo:lo + LANES].astype(F32)
        return jnp.concatenate([buf[c] for c in range(chunks)], axis=1)

    dils = [dil for _, dil in DIL_GROUPS]
    merged = gate(0) * _dot(oa_ref[0], wa_ref[...]) + gate(1) * _dot(ob_ref[0], wb_ref[...])
    gate_c = gate(2)
    l0, l1, l2 = (token_major(ref, r) for ref, r in zip((l0_ref, l1_ref, l2_ref), dils))
    mx = jnp.maximum(jnp.maximum(l0, l1), l2)
    e0, e1, e2 = jnp.exp(l0 - mx), jnp.exp(l1 - mx), jnp.exp(l2 - mx)
    o0, o1, o2 = (token_major(ref, r) for ref, r in zip((oc0_ref, oc1_ref, oc2_ref), dils))
    oc = (e0 * o0 + e1 * o1 + e2 * o2) / (e0 + e1 + e2)
    merged = merged + gate_c * _dot(oc.astype(BF), wc_ref[...])
    o_ref[0] = h + _dot(merged.astype(BF), wo_ref[...])


def _merge(h3, oa, ob, ocs, lses, g_mix, w_gate, gbias, wa, wb, wc, wo, tm=512):
    b, s, d = h3.shape
    tm = min(tm, s)
    row = lambda n: pl.BlockSpec((1, tm, n), lambda bi, i: (bi, i, 0))
    views = [pl.BlockSpec((1, tm // r, r * DIL_WIDTH), lambda bi, i: (bi, i, 0)) for _, r in DIL_GROUPS]
    n_buf = 2 * sum(1 for _, r in DIL_GROUPS if r > 1)
    params = (g_mix, w_gate, gbias, wa, wb, wc, wo)
    return pl.pallas_call(
        _merge_body,
        grid=(b, s // tm),
        in_specs=[row(d), row(SB_WIDTH), row(SB_WIDTH)] + views + views + [_resident(p.shape) for p in params],
        out_specs=row(d),
        out_shape=jax.ShapeDtypeStruct((b, s, d), F32),
        scratch_shapes=[pltpu.VMEM((DIL_WIDTH // LANES, tm, LANES), F32)] * n_buf,
        compiler_params=_cparams(("parallel", "parallel")),
        name="merge",
    )(h3, oa, ob, *ocs, *lses, *params)


def _memkv_body(m_ref, g_ref, w_ref, o_ref):
    u = _rms(m_ref[0], g_ref[...]).astype(BF)
    o_ref[0] = _dot(u, w_ref[...]).astype(o_ref.dtype)


def _mem_kv(mem, g, w):
    b, m, d = mem.shape
    n = w.shape[1]
    return pl.pallas_call(
        _memkv_body,
        grid=(b,),
        in_specs=[pl.BlockSpec((1, m, d), lambda i: (i, 0, 0)),
                  pl.BlockSpec((1, d), lambda i: (0, 0)),
                  pl.BlockSpec((d, n), lambda i: (0, 0))],
        out_specs=pl.BlockSpec((1, m, n), lambda i: (i, 0, 0)),
        out_shape=jax.ShapeDtypeStruct((b, m, n), BF),
        compiler_params=_cparams(("parallel",)),
        name="mem_kv",
    )(mem, g, w)


def _xattn_body(h_ref, g_ref, kv_ref, wq_ref, wo_ref, o_ref):
    x = h_ref[0]
    u = _rms(x, g_ref[...]).astype(BF)
    q = _dot(u, wq_ref[...]).astype(BF)
    kv = kv_ref[0]
    width = X_HEADS * X_HEAD_DIM
    scale = X_HEAD_DIM ** -0.5
    outs = []
    for h in range(X_HEADS):
        cols = slice(h * X_HEAD_DIM, (h + 1) * X_HEAD_DIM)
        s = _dot_t(q[:, cols], kv[:, cols]) * scale
        m = jnp.max(s, axis=-1, keepdims=True)
        e = jnp.exp(s - m)
        p = e / jnp.sum(e, axis=-1, keepdims=True)
        outs.append(_dot(p.astype(BF), kv[:, width + h * X_HEAD_DIM:width + (h + 1) * X_HEAD_DIM]))
    o = jnp.concatenate(outs, axis=-1).astype(BF)
    o_ref[0] = x + _dot(o, wo_ref[...])


def _xattn(h3, g, kv, wq, wo, tm=512):
    b, s, d = h3.shape
    tm = min(tm, s)
    m = kv.shape[1]
    return pl.pallas_call(
        _xattn_body,
        grid=(b, s // tm),
        in_specs=[pl.BlockSpec((1, tm, d), lambda bi, i: (bi, i, 0)),
                  pl.BlockSpec((1, d), lambda bi, i: (0, 0)),
                  pl.BlockSpec((1, m, kv.shape[2]), lambda bi, i: (bi, 0, 0)),
                  pl.BlockSpec(wq.shape, lambda bi, i: (0, 0)),
                  pl.BlockSpec(wo.shape, lambda bi, i: (0, 0))],
        out_specs=pl.BlockSpec((1, tm, d), lambda bi, i: (bi, i, 0)),
        out_shape=jax.ShapeDtypeStruct((b, s, d), F32),
        compiler_params=_cparams(("parallel", "parallel")),
        name="xattn",
    )(h3, g, kv, wq, wo)


def _rel_bucket(dist):
    exact = REL_BUCKETS // 2
    d = jnp.maximum(dist, exact).astype(F32)
    large = exact + (jnp.log(d / exact) / math.log(REL_MAX_DIST / exact)
                     * (REL_BUCKETS - exact)).astype(jnp.int32)
    return jnp.where(dist < exact, dist, jnp.minimum(large, REL_BUCKETS - 1))


def _band_bias(rel_bias, group, dilation):
    heads = DIL_WIDTH // HEAD_DIM
    qi = jnp.arange(BAND)
    kj = jnp.arange(2 * BAND)
    steps = (qi[:, None] + BAND) - kj[None, :]
    table = rel_bias[:, group * heads:(group + 1) * heads].astype(F32)
    bucket = _rel_bucket(jnp.clip(steps, 0, BAND) * dilation)
    onehot = (bucket[None] == jnp.arange(REL_BUCKETS)[:, None, None]).astype(F32)
    return jnp.einsum('bh,bqk->hqk', table * LOG2E, onehot, precision=lax.Precision.HIGHEST)


def _swap_halves(w, width):
    k, n = w.shape
    w = w.reshape(k, n // width, 2, width // 2)
    return w[:, :, ::-1, :].reshape(k, n)


def _pack_w_in(w_in):
    d = w_in.shape[0]
    qscale = HEAD_DIM ** -0.5 * LOG2E
    sb = jnp.concatenate([w_in[:, :SB_WIDTH] * qscale, w_in[:, SB_WIDTH:1536]], axis=1)
    cq = w_in[:, 1536:1920]
    ckv = w_in[:, 1920:2048]
    kr = w_in[:, 2048:2080]
    zeros = jnp.zeros((d, 64), w_in.dtype)
    kr_sw = _swap_halves(kr, MLA_ROPE)
    seg_a = jnp.concatenate([sb, cq, ckv, kr, kr, zeros, kr_sw, kr_sw, zeros], axis=1)
    group = 3 * DIL_WIDTH
    seg_d = []
    for g in range(len(DIL_GROUPS)):
        lo = 2080 + g * group
        seg_d.append(jnp.concatenate([w_in[:, lo:lo + DIL_WIDTH] * qscale, w_in[:, lo + DIL_WIDTH:lo + group]],
                                     axis=1).astype(BF))
    seg_g = w_in[:, 2080 + len(DIL_GROUPS) * group:]
    return seg_a.astype(BF), seg_d, seg_g.astype(BF)


def _pack_w_uq(w_uq):
    k = w_uq.shape[0]
    w = w_uq.reshape(k, MLA_HEADS, MLA_NOPE + MLA_ROPE)
    nope = w[:, :, :MLA_NOPE].reshape(k, MLA_HEADS * MLA_NOPE)
    rope = w[:, :, MLA_NOPE:]

    def pair_layout(rp):
        rp = rp.reshape(k, MLA_HEADS // 2, 2 * MLA_ROPE)
        pad = jnp.zeros((k, MLA_HEADS // 2, 128 - 2 * MLA_ROPE), rp.dtype)
        return jnp.concatenate([rp, pad], axis=-1).reshape(k, (MLA_HEADS // 2) * 128)

    rope_sw = rope.reshape(k, MLA_HEADS, 2, MLA_ROPE // 2)[:, :, ::-1, :].reshape(k, MLA_HEADS, MLA_ROPE)
    return jnp.concatenate([nope, pair_layout(rope), pair_layout(rope_sw)], axis=1).astype(BF)


def _pack_w_ukv(w_ukv):
    k = w_ukv.shape[0]
    w = w_ukv.reshape(k, MLA_HEADS, MLA_NOPE + MLA_V)
    return jnp.concatenate([w[:, :, :MLA_NOPE].reshape(k, -1), w[:, :, MLA_NOPE:].reshape(k, -1)],
                           axis=1).astype(BF)


def _rope_tables(seq):
    half = MLA_ROPE // 2
    freqs = ROPE_THETA ** (-jnp.arange(half, dtype=F32) / half)
    ang = jnp.arange(seq).astype(F32)[:, None] * freqs[None, :]
    cos, sin = jnp.cos(ang), jnp.sin(ang)
    pad = jnp.zeros((seq, 128 - 2 * MLA_ROPE), F32)
    cos_t = jnp.concatenate([cos, cos, cos, cos, pad], axis=1)
    sin_t = jnp.concatenate([-sin, sin, -sin, sin, pad], axis=1)
    return cos_t, sin_t


def kernel(x, mem, ffn1_norm, ffn1_w_gate, ffn1_w_up, ffn1_w_down, mix_norm, w_in, gate_bias, mla_q_norm, mla_w_uq, mla_kv_norm, mla_w_ukv, w_branch_a, w_branch_b, w_branch_c, w_mix_out, rel_bias, xattn_norm, mem_norm, xattn_w_q, xattn_w_kv, xattn_w_o, ffn2_norm, ffn2_w_gate, ffn2_w_up, ffn2_w_down, final_norm):
    b, s, d = x.shape
    t = b * s
    depth = w_in.shape[0]
    cos_t, sin_t = _rope_tables(s)
    biases = [_band_bias(rel_bias, g, dil) for g, (_, dil) in enumerate(DIL_GROUPS)]
    row = lambda v: v.reshape(1, -1)
    bf = lambda w: w.astype(BF)
    fg = row(final_norm)

    h = x.reshape(t, d)
    for l in range(depth):
        h = _ffn(h, row(ffn1_norm[l]), bf(ffn1_w_gate[l]), bf(ffn1_w_up[l]), bf(ffn1_w_down[l]), fg, False)

        w_a, w_d, w_g = _pack_w_in(w_in[l])
        g_mix = row(mix_norm[l])
        h3 = h.reshape(b, s, d)
        seg_a = _norm_proj(h, g_mix, w_a, tn=PROJ_CHUNK)

        a3 = seg_a.reshape(b, s, seg_a.shape[1])
        o_a = _sb_attention(a3)
        qp, kp, vt = _mla_prep(a3, cos_t, sin_t, row(mla_q_norm[l]), row(mla_kv_norm[l]),
                               _pack_w_uq(mla_w_uq[l]), _pack_w_ukv(mla_w_ukv[l]), tm=MLA_BLOCK)
        o_b = _mla_attention(qp, kp, vt)
        ocs, lses = [], []
        for g, (_, dil) in enumerate(DIL_GROUPS):
            o, lse = _dil_attention(_dil_proj(h3, g_mix, w_d[g], dil), biases[g], dil)
            ocs.append(o)
            lses.append(lse)
        h3 = _merge(h3, o_a, o_b, ocs, lses, g_mix, w_g, gate_bias[l],
                    bf(w_branch_a[l]), bf(w_branch_b[l]), bf(w_branch_c[l]), bf(w_mix_out[l]))

        kv = _mem_kv(mem, row(mem_norm[l]), bf(xattn_w_kv[l]))
        h = _xattn(h3, row(xattn_norm[l]), kv, bf(xattn_w_q[l]), bf(xattn_w_o[l])).reshape(t, d)

        h = _ffn(h, row(ffn2_norm[l]), bf(ffn2_w_gate[l]), bf(ffn2_w_up[l]), bf(ffn2_w_down[l]), fg,
                 l == depth - 1)
    return h.reshape(b, s, d)
```

```python
import functools
import math

import jax
import jax.numpy as jnp
import numpy as np
from jax import lax
from jax.experimental import pallas as pl
from jax.experimental.pallas import tpu as pltpu

BF = jnp.bfloat16
F32 = jnp.float32

EPS = 1e-6
NEG = -1e30
FFN_RESIDUAL = 0.5

SB_WIDTH = 512
MLA_HEADS = 8
MLA_Q_RANK = 384
MLA_KV_RANK = 128
MLA_NOPE = 64
MLA_ROPE = 32
MLA_V = 64
ROPE_THETA = 10000.0
DIL_GROUPS = ((128, 1), (512, 4), (2048, 16))
DIL_WIDTH = 512
REL_BUCKETS = 32
REL_MAX_DIST = 2048
X_HEADS = 4
X_HEAD_DIM = 128
HEAD_DIM = 64
BAND = 128
LANES = 128
SB_DEAD_LOG2 = -150.0
LOG2E = math.log2(math.e)
LN2 = math.log(2.0)

V7X_VMEM_LIMIT = 48 * 1024 * 1024
FF_CHUNK = 256
MLA_BLOCK = 512
PROJ_CHUNK = 768


def _cparams(sem):
    return pltpu.CompilerParams(dimension_semantics=sem, vmem_limit_bytes=V7X_VMEM_LIMIT)


def _rms(x, g):
    ms = jnp.mean(x * x, axis=-1, keepdims=True)
    return x * lax.rsqrt(ms + EPS) * g


def _sigmoid(x):
    return 0.5 * jnp.tanh(0.5 * x) + 0.5


def _dot(a, b):
    return jnp.dot(a, b, preferred_element_type=F32)


def _dot_t(a, b):
    return lax.dot_general(a, b, (((1,), (1,)), ((), ())), preferred_element_type=F32)


def _resident(shape):
    nd = len(shape)
    return pl.BlockSpec(shape, lambda *_: (0,) * nd, pipeline_mode=pl.Buffered(1))


def _ffn_body(h_ref, g_ref, wg_ref, wu_ref, wd_ref, fg_ref, o_ref, act_ref, *, final):
    x = h_ref[...]
    u = _rms(x, g_ref[...]).astype(BF)
    d_ff = wg_ref.shape[1]
    for c in range(d_ff // FF_CHUNK):
        sl = slice(c * FF_CHUNK, (c + 1) * FF_CHUNK)
        a = _dot(u, wg_ref[:, sl])
        b = _dot(u, wu_ref[:, sl])
        act_ref[:, sl] = (a * jax.nn.sigmoid(a) * b).astype(BF)
    y = x + FFN_RESIDUAL * _dot(act_ref[...], wd_ref[...])
    if final:
        y = _rms(y, fg_ref[...])
    o_ref[...] = y


def _ffn(h, g, wg, wu, wd, fg, final, tm=512):
    t, d = h.shape
    d_ff = wg.shape[1]
    tm = min(tm, t)
    return pl.pallas_call(
        functools.partial(_ffn_body, final=final),
        grid=(t // tm,),
        in_specs=[
            pl.BlockSpec((tm, d), lambda i: (i, 0)),
            _resident((1, d)),
            _resident((d, d_ff)),
            _resident((d, d_ff)),
            _resident((d_ff, d)),
            _resident((1, d)),
        ],
        out_specs=pl.BlockSpec((tm, d), lambda i: (i, 0)),
        out_shape=jax.ShapeDtypeStruct((t, d), F32),
        scratch_shapes=[pltpu.VMEM((tm, d_ff), BF)],
        compiler_params=_cparams(("parallel",)),
        name="ffn",
    )(h, g, wg, wu, wd, fg)


def _proj_body(h_ref, g_ref, w_ref, o_ref, *, tn):
    u = _rms(h_ref[...], g_ref[...]).astype(BF)
    for c in range(w_ref.shape[1] // tn):
        sl = slice(c * tn, (c + 1) * tn)
        o_ref[:, sl] = _dot(u, w_ref[:, sl]).astype(o_ref.dtype)


def _norm_proj(h, g, w, tn, tm=512):
    t, d = h.shape
    n = w.shape[1]
    tm = min(tm, t)
    return pl.pallas_call(
        functools.partial(_proj_body, tn=tn),
        grid=(t // tm,),
        in_specs=[
            pl.BlockSpec((tm, d), lambda i: (i, 0)),
            _resident((1, d)),
            _resident((d, n)),
        ],
        out_specs=pl.BlockSpec((tm, n), lambda i: (i, 0)),
        out_shape=jax.ShapeDtypeStruct((t, n), BF),
        compiler_params=_cparams(("parallel",)),
        name="norm_proj",
    )(h, g, w)


def _dil_proj_body(h_ref, g_ref, w_ref, o_ref, *stage, r):
    rows = h_ref.shape[1] // r
    n = w_ref.shape[1]
    u = _rms(h_ref[0], g_ref[...])
    if r == 1:
        ub = u.astype(BF)
        for c in range(n // PROJ_CHUNK):
            sl = slice(c * PROJ_CHUNK, (c + 1) * PROJ_CHUNK)
            o_ref[0, :, sl] = _dot(ub, w_ref[:, sl]).astype(o_ref.dtype)
        return
    (stage,) = stage
    chunks = u.shape[1] // LANES
    for c in range(chunks):
        stage[c] = u[:, c * LANES:(c + 1) * LANES]
    for res in range(r):
        ur = jnp.concatenate([stage[c, pl.ds(res, rows, stride=r), :] for c in range(chunks)], axis=1)
        o_ref[0, :, res * n:(res + 1) * n] = _dot(ur.astype(BF), w_ref[...]).astype(o_ref.dtype)


def _dil_proj(h3, g, w, r):
    b, s, d = h3.shape
    n = w.shape[1]
    tm = min(s, max(1024, BAND * r) if r > 1 else 512)
    return pl.pallas_call(
        functools.partial(_dil_proj_body, r=r),
        grid=(b, s // tm),
        in_specs=[
            pl.BlockSpec((1, tm, d), lambda bi, i: (bi, i, 0)),
            pl.BlockSpec((1, d), lambda bi, i: (0, 0)),
            pl.BlockSpec((d, n), lambda bi, i: (0, 0)),
        ],
        out_specs=pl.BlockSpec((1, tm // r, r * n), lambda bi, i: (bi, i, 0)),
        out_shape=jax.ShapeDtypeStruct((b, s // r, r * n), BF),
        scratch_shapes=[pltpu.VMEM((d // LANES, tm, LANES), F32)] if r > 1 else [],
        compiler_params=_cparams(("parallel", "parallel")),
        name=f"dil_proj_r{r}",
    )(h3, g, w)


def _sb_body(q_ref, k_ref, v_ref, o_ref, *, tq, pairs):
    i = pl.program_id(2)
    pw = 2 * HEAD_DIM
    lane = lax.broadcasted_iota(jnp.int32, (1, pw), 1)
    first = lane < HEAD_DIM
    q_heads = []
    for p in range(pairs):
        q = q_ref[0, :, p * pw:(p + 1) * pw]
        zero = jnp.zeros_like(q)
        q_heads += [jnp.where(first, q, zero), jnp.where(first, zero, q)]
    row = lax.broadcasted_iota(jnp.int32, (tq, tq), 0)
    col = lax.broadcasted_iota(jnp.int32, (tq, tq), 1)
    strict = col < row
    later = jnp.where(row > col, 1.0, 0.0).astype(BF)

    def block(j, accs, rs, diag):
        start = pl.multiple_of(j * tq, tq)
        heads = range(2 * pairs)
        ks = [k_ref[0, pl.ds(start, tq), p * pw:(p + 1) * pw] for p in range(pairs)]
        vs = [v_ref[0, pl.ds(start, tq), p * pw:(p + 1) * pw] for p in range(pairs)]
        zs = [_dot_t(q_heads[h], ks[h // 2]) for h in heads]
        log_betas, log_keeps = [], []
        for z in zs:
            sp = jnp.log(1.0 + jnp.exp2(-jnp.abs(z))) * LOG2E
            log_beta = jnp.minimum(z, 0.0) - sp
            log_keep = log_beta - z
            if diag:
                log_keep = jnp.where(strict, log_keep, 0.0)
            log_betas.append(log_beta)
            log_keeps.append(log_keep)
        afters = [_dot(log_keeps[h].astype(BF), later) for h in heads]
        ws = []
        for h in heads:
            w = jnp.exp2(log_betas[h] + afters[h] + rs[h])
            if diag:
                w = jnp.where(strict, w, 0.0)
            ws.append(w.astype(BF))
        outs = [_dot(ws[h], vs[h // 2]) for h in heads]
        new_rs = [rs[h] + jnp.sum(log_keeps[h], axis=-1, keepdims=True) for h in heads]
        new_accs = [accs[p] + jnp.where(first, outs[2 * p], outs[2 * p + 1]) for p in range(pairs)]
        return tuple(new_accs), tuple(new_rs)

    def remaining(rs):
        return jnp.max(functools.reduce(jnp.maximum, rs))

    def cond(c):
        return (c[0] < i) & (c[1] > SB_DEAD_LOG2)

    def body(c):
        accs, rs = block(i - 1 - c[0], c[2], c[3], False)
        return c[0] + 1, remaining(rs), accs, rs

    accs = tuple(jnp.zeros((tq, pw), F32) for _ in range(pairs))
    rs = tuple(jnp.zeros((tq, 1), F32) for _ in range(2 * pairs))
    accs, rs = block(i, accs, rs, True)
    out = lax.while_loop(cond, body, (jnp.int32(0), remaining(rs), accs, rs))
    for p in range(pairs):
        o_ref[0, :, p * pw:(p + 1) * pw] = out[2][p].astype(o_ref.dtype)


def _sb_attention(a3, tq=256, pairs=4):
    b, s, _ = a3.shape
    tq = min(tq, s)
    w = 2 * HEAD_DIM * pairs
    groups = SB_WIDTH // w
    return pl.pallas_call(
        functools.partial(_sb_body, tq=tq, pairs=pairs),
        grid=(b, groups, s // tq),
        in_specs=[
            pl.BlockSpec((1, tq, w), lambda bi, p, i: (bi, i, p)),
            pl.BlockSpec((1, s, w), lambda bi, p, i: (bi, 0, groups + p)),
            pl.BlockSpec((1, s, w), lambda bi, p, i: (bi, 0, 2 * groups + p)),
        ],
        out_specs=pl.BlockSpec((1, tq, w), lambda bi, p, i: (bi, i, p)),
        out_shape=jax.ShapeDtypeStruct((b, s, SB_WIDTH), BF),
        compiler_params=_cparams(("parallel", "parallel", "arbitrary")),
        name="sb_attn",
    )(a3, a3, a3)


def _mla_prep_body(cq_ref, ckv_ref, kr_ref, cos_ref, sin_ref, qg_ref, kvg_ref, wq_ref, wkv_ref,
                   qp_ref, kp_ref, v_ref, *, scale):
    nq = _rms(cq_ref[0].astype(F32), qg_ref[...]).astype(BF)
    q = _dot(nq, wq_ref[...])
    nkv = _rms(ckv_ref[0].astype(F32), kvg_ref[...]).astype(BF)
    kv = _dot(nkv, wkv_ref[...])
    cos = cos_ref[...]
    sin = sin_ref[...]
    kr = kr_ref[0].astype(F32)
    k_pe = (kr[:, :128] * cos + kr[:, 128:] * sin).astype(BF)
    width = q.shape[1] // 3
    for p in range(width // 128):
        lo, hi = p * 128, (p + 1) * 128
        qp_ref[0, :, 2 * lo:2 * lo + 128] = (q[:, lo:hi] * scale).astype(BF)
        q_pe = q[:, width + lo:width + hi] * cos + q[:, 2 * width + lo:2 * width + hi] * sin
        qp_ref[0, :, 2 * lo + 128:2 * hi] = (q_pe * scale).astype(BF)
        kp_ref[0, :, 2 * lo:2 * lo + 128] = kv[:, lo:hi].astype(BF)
        kp_ref[0, :, 2 * lo + 128:2 * hi] = k_pe
    v_ref[0, 0] = kv[:, width:].T.astype(BF)


def _mla_prep(a3, cos, sin, qg, kvg, wq, wkv, tm):
    b, s, _ = a3.shape
    tm = min(tm, s)
    width = MLA_HEADS * MLA_NOPE
    scale = (MLA_NOPE + MLA_ROPE) ** -0.5 * LOG2E
    out_sds = lambda n: jax.ShapeDtypeStruct((b, s, n), BF)
    return pl.pallas_call(
        functools.partial(_mla_prep_body, scale=scale),
        grid=(b, s // tm),
        in_specs=[
            pl.BlockSpec((1, tm, MLA_Q_RANK), lambda bi, i: (bi, i, 1536 // MLA_Q_RANK)),
            pl.BlockSpec((1, tm, MLA_KV_RANK), lambda bi, i: (bi, i, 1920 // MLA_KV_RANK)),
            pl.BlockSpec((1, tm, 256), lambda bi, i: (bi, i, 2048 // 256)),
            pl.BlockSpec((tm, 128), lambda bi, i: (i, 0)),
            pl.BlockSpec((tm, 128), lambda bi, i: (i, 0)),
            pl.BlockSpec((1, MLA_Q_RANK), lambda bi, i: (0, 0)),
            pl.BlockSpec((1, MLA_KV_RANK), lambda bi, i: (0, 0)),
            pl.BlockSpec(wq.shape, lambda bi, i: (0, 0)),
            pl.BlockSpec(wkv.shape, lambda bi, i: (0, 0)),
        ],
        out_specs=[
            pl.BlockSpec((1, tm, 2 * width), lambda bi, i: (bi, i, 0)),
            pl.BlockSpec((1, tm, 2 * width), lambda bi, i: (bi, i, 0)),
            pl.BlockSpec((1, 1, width, tm), lambda bi, i: (bi, i, 0, 0)),
        ],
        out_shape=[out_sds(2 * width), out_sds(2 * width),
                   jax.ShapeDtypeStruct((b, s // tm, width, tm), BF)],
        compiler_params=_cparams(("parallel", "parallel")),
        name="mla_prep",
    )(a3, a3, a3, cos, sin, qg, kvg, wq, wkv)


def _mla_body(q_ref, k_ref, v_ref, o_ref, *, tq, pairs):
    i = pl.program_id(2)
    lane = lax.broadcasted_iota(jnp.int32, (1, 256), 1)
    sel0 = (lane < 64) | ((lane >= 128) & (lane < 160))
    sel1 = ((lane >= 64) & (lane < 128)) | ((lane >= 160) & (lane < 192))
    q_heads = []
    for p in range(pairs):
        q = q_ref[0, :, p * 256:(p + 1) * 256]
        zero = jnp.zeros_like(q)
        q_heads += [jnp.where(sel0, q, zero), jnp.where(sel1, q, zero)]
    key = lax.broadcasted_iota(jnp.int32, (tq, tq), 0)
    qry = lax.broadcasted_iota(jnp.int32, (tq, tq), 1)
    causal = key <= qry
    first = lax.broadcasted_iota(jnp.int32, (128, 1), 0) < 64

    def block(j, carry, diag, nblk=1):
        start = pl.multiple_of(j * tq, tq)
        heads = range(2 * pairs)
        ks = [k_ref[0, pl.ds(start, nblk * tq), p * 256:(p + 1) * 256] for p in range(pairs)]
        v_heads = []
        for p in range(pairs):
            vt = jnp.concatenate([v_ref[0, j + n, p * 128:(p + 1) * 128, :] for n in range(nblk)], axis=1)
            ones = jnp.ones_like(vt)
            v_heads += [jnp.where(first, vt, ones), jnp.where(first, ones, vt)]
        ss = [_dot_t(ks[h // 2], q_heads[h]) for h in heads]
        if diag:
            ss = [jnp.where(causal, s, NEG) for s in ss]
        m_news = [jnp.maximum(carry[h][0], jnp.max(ss[h], axis=0, keepdims=True)) for h in heads]
        prs = [jnp.exp2(ss[h] - m_news[h]).astype(BF) for h in heads]
        pvs = [_dot(v_heads[h], prs[h]) for h in heads]
        return tuple((m_news[h], jnp.exp2(carry[h][0] - m_news[h]) * carry[h][1] + pvs[h]) for h in heads)

    one = (jnp.full((1, tq), NEG, F32), jnp.zeros((128, tq), F32))
    carry = lax.fori_loop(0, i // 2, lambda t, c: block(2 * t, c, False, 2), (one,) * (2 * pairs))
    carry = lax.cond(i % 2 == 1, lambda c: block(i - 1, c, False), lambda c: c, carry)
    carry = block(i, carry, True)
    for p in range(pairs):
        a0, a1 = carry[2 * p][1], carry[2 * p + 1][1]
        lsw = jnp.where(first, a1, a0)
        l = jnp.concatenate([lsw[64:], lsw[:64]], axis=0)
        out_t = jnp.where(first, a0, a1) / l
        o_ref[0, :, p * 128:(p + 1) * 128] = out_t.T.astype(o_ref.dtype)


def _mla_attention(qp, kp, vt, pairs=2):
    b, s, _ = qp.shape
    nblk, width, tq = vt.shape[1:]
    groups = MLA_HEADS // (2 * pairs)
    return pl.pallas_call(
        functools.partial(_mla_body, tq=tq, pairs=pairs),
        grid=(b, groups, s // tq),
        in_specs=[
            pl.BlockSpec((1, tq, 256 * pairs), lambda bi, p, i: (bi, i, p)),
            pl.BlockSpec((1, s, 256 * pairs), lambda bi, p, i: (bi, 0, p)),
            pl.BlockSpec((1, nblk, 128 * pairs, tq), lambda bi, p, i: (bi, 0, p, 0)),
        ],
        out_specs=pl.BlockSpec((1, tq, 128 * pairs), lambda bi, p, i: (bi, i, p)),
        out_shape=jax.ShapeDtypeStruct((b, s, MLA_HEADS * MLA_V), BF),
        compiler_params=_cparams(("parallel", "parallel", "arbitrary")),
        name="mla_attn",
    )(qp, kp, vt)


def _dil_body(q_ref, kc_ref, kp_ref, vc_ref, vp_ref, bias_ref, o_ref, lse_ref, *, tl):
    j = pl.program_id(2)
    lane = lax.broadcasted_iota(jnp.int32, (1, 2 * HEAD_DIM), 1)
    first = lane < HEAD_DIM
    kj = lax.broadcasted_iota(jnp.int32, (BAND, 2 * BAND), 1)
    has_prev = (kj >= BAND) | (j > 0)
    pw = 2 * HEAD_DIM
    pairs = DIL_WIDTH // pw
    heads = range(2 * pairs)
    for sb in range(tl // BAND):
        rows = slice(sb * BAND, (sb + 1) * BAND)
        prev = slice((sb - 1) * BAND, sb * BAND)
        k_prev = kp_ref[0] if sb == 0 else kc_ref[0, prev, :]
        v_prev = vp_ref[0] if sb == 0 else vc_ref[0, prev, :]
        kcat = jnp.concatenate([k_prev, kc_ref[0, rows, :]], axis=0)
        vcat = jnp.concatenate([v_prev, vc_ref[0, rows, :]], axis=0)
        qs = q_ref[0, rows, :]
        q_heads, v_heads = [], []
        for p in range(pairs):
            qb = qs[:, p * pw:(p + 1) * pw]
            vb = vcat[:, p * pw:(p + 1) * pw]
            zero, ones = jnp.zeros_like(qb), jnp.ones_like(vb)
            q_heads += [jnp.where(first, qb, zero), jnp.where(first, zero, qb)]
            v_heads += [jnp.where(first, vb, ones), jnp.where(first, ones, vb)]
        ss = [_dot_t(q_heads[h], kcat[:, (h // 2) * pw:(h // 2 + 1) * pw]) + bias_ref[h] for h in heads]
        if sb == 0:
            ss = [jnp.where(has_prev, s, NEG) for s in ss]
        ms = [jnp.max(s, axis=-1, keepdims=True) for s in ss]
        prs = [jnp.exp2(ss[h] - ms[h]).astype(BF) for h in heads]
        pvs = [_dot(prs[h], v_heads[h]) for h in heads]
        for p in range(pairs):
            a0, a1 = pvs[2 * p], pvs[2 * p + 1]
            cols = slice(p * pw, (p + 1) * pw)
            l = pltpu.roll(jnp.where(first, a1, a0), HEAD_DIM, 1)
            o_ref[0, rows, cols] = (jnp.where(first, a0, a1) / l).astype(o_ref.dtype)
            lse_ref[0, rows, cols] = jnp.where(first, ms[2 * p], ms[2 * p + 1]) * LN2 + jnp.log(l)


def _dil_attention(view, bias, dilation, tl=512):
    r = dilation
    batch, length, _ = view.shape
    tl = min(tl, length)
    sub = tl // BAND

    def cur(c):
        return pl.BlockSpec((1, tl, DIL_WIDTH), lambda bi, res, j: (bi, j, res * 3 + c))

    def prev(c):
        return pl.BlockSpec((1, BAND, DIL_WIDTH),
                            lambda bi, res, j: (bi, jnp.maximum(j * sub - 1, 0), res * 3 + c))

    out_spec = pl.BlockSpec((1, tl, DIL_WIDTH), lambda bi, res, j: (bi, j, res))
    return pl.pallas_call(
        functools.partial(_dil_body, tl=tl),
        grid=(batch, r, length // tl),
        in_specs=[cur(0), cur(1), prev(1), cur(2), prev(2),
                  pl.BlockSpec(bias.shape, lambda bi, res, j: (0, 0, 0))],
        out_specs=[out_spec, out_spec],
        out_shape=[jax.ShapeDtypeStruct((batch, length, r * DIL_WIDTH), BF),
                   jax.ShapeDtypeStruct((batch, length, r * DIL_WIDTH), F32)],
        compiler_params=_cparams(("parallel", "parallel", "arbitrary")),
        name=f"dil_attn_r{r}",
    )(view, view, view, view, view, bias)


def _merge_body(h_ref, oa_ref, ob_ref, oc0_ref, oc1_ref, oc2_ref, l0_ref, l1_ref, l2_ref,
                gn_ref, wg_ref, gbias_ref, wa_ref, wb_ref, wc_ref, wo_ref, o_ref, *scratch):
    tm, d = h_ref.shape[1:]
    scratch = list(scratch)
    h = h_ref[0]
    u = _rms(h, gn_ref[...]).astype(BF)

    def gate(c):
        return _sigmoid(_dot(u, wg_ref[:, c * d:(c + 1) * d]) + gbias_ref[c:c + 1, :])

    def token_major(ref, r):
        if r == 1:
            return ref[0].astype(F32)
        buf = scratch.pop(0)
        chunks = DIL_WIDTH // LANES
        for res in range(r):
            for c in range(chunks):
                lo = res * DIL_WIDTH + c * LANES
                buf[c, pl.ds(res, tm // r, stride=r), :] = ref[0, :, lo:lo + LANES].astype(F32)
        return jnp.concatenate([buf[c] for c in range(chunks)], axis=1)

    dils = [dil for _, dil in DIL_GROUPS]
    merged = gate(0) * _dot(oa_ref[0], wa_ref[...]) + gate(1) * _dot(ob_ref[0], wb_ref[...])
    gate_c = gate(2)
    l0, l1, l2 = (token_major(ref, r) for ref, r in zip((l0_ref, l1_ref, l2_ref), dils))
    mx = jnp.maximum(jnp.maximum(l0, l1), l2)
    e0, e1, e2 = jnp.exp(l0 - mx), jnp.exp(l1 - mx), jnp.exp(l2 - mx)
    o0, o1, o2 = (token_major(ref, r) for ref, r in zip((oc0_ref, oc1_ref, oc2_ref), dils))
    oc = (e0 * o0 + e1 * o1 + e2 * o2) / (e0 + e1 + e2)
    merged = merged + gate_c * _dot(oc.astype(BF), wc_ref[...])
    o_ref[0] = h + _dot(merged.astype(BF), wo_ref[...])


def _merge(h3, oa, ob, ocs, lses, g_mix, w_gate, gbias, wa, wb, wc, wo, tm=512):
    b, s, d = h3.shape
    tm = min(tm, s)
    row = lambda n: pl.BlockSpec((1, tm, n), lambda bi, i: (bi, i, 0))
    views = [pl.BlockSpec((1, tm // r, r * DIL_WIDTH), lambda bi, i: (bi, i, 0)) for _, r in DIL_GROUPS]
    n_buf = 2 * sum(1 for _, r in DIL_GROUPS if r > 1)
    params = (g_mix, w_gate, gbias, wa, wb, wc, wo)
    return pl.pallas_call(
        _merge_body,
        grid=(b, s // tm),
        in_specs=[row(d), row(SB_WIDTH), row(SB_WIDTH)] + views + views + [_resident(p.shape) for p in params],
        out_specs=row(d),
        out_shape=jax.ShapeDtypeStruct((b, s, d), F32),
        scratch_shapes=[pltpu.VMEM((DIL_WIDTH // LANES, tm, LANES), F32)] * n_buf,
        compiler_params=_cparams(("parallel", "parallel")),
        name="merge",
    )(h3, oa, ob, *ocs, *lses, *params)


def _memkv_body(m_ref, g_ref, w_ref, o_ref):
    u = _rms(m_ref[0], g_ref[...]).astype(BF)
    o_ref[0] = _dot(u, w_ref[...]).astype(o_ref.dtype)


def _mem_kv(mem, g, w):
    b, m, d = mem.shape
    n = w.shape[1]
    return pl.pallas_call(
        _memkv_body,
        grid=(b,),
        in_specs=[pl.BlockSpec((1, m, d), lambda i: (i, 0, 0)),
                  pl.BlockSpec((1, d), lambda i: (0, 0)),
                  pl.BlockSpec((d, n), lambda i: (0, 0))],
        out_specs=pl.BlockSpec((1, m, n), lambda i: (i, 0, 0)),
        out_shape=jax.ShapeDtypeStruct((b, m, n), BF),
        compiler_params=_cparams(("parallel",)),
        name="mem_kv",
    )(mem, g, w)


def _xattn_body(h_ref, g_ref, kv_ref, wq_ref, wo_ref, o_ref):
    x = h_ref[0]
    u = _rms(x, g_ref[...]).astype(BF)
    q = _dot(u, wq_ref[...]).astype(BF)
    kv = kv_ref[0]
    width = X_HEADS * X_HEAD_DIM
    scale = X_HEAD_DIM ** -0.5
    outs = []
    for h in range(X_HEADS):
        cols = slice(h * X_HEAD_DIM, (h + 1) * X_HEAD_DIM)
        s = _dot_t(q[:, cols], kv[:, cols]) * scale
        m = jnp.max(s, axis=-1, keepdims=True)
        e = jnp.exp(s - m)
        p = e / jnp.sum(e, axis=-1, keepdims=True)
        outs.append(_dot(p.astype(BF), kv[:, width + h * X_HEAD_DIM:width + (h + 1) * X_HEAD_DIM]))
    o = jnp.concatenate(outs, axis=-1).astype(BF)
    o_ref[0] = x + _dot(o, wo_ref[...])


def _xattn(h3, g, kv, wq, wo, tm=512):
    b, s, d = h3.shape
    tm = min(tm, s)
    m = kv.shape[1]
    return pl.pallas_call(
        _xattn_body,
        grid=(b, s // tm),
        in_specs=[pl.BlockSpec((1, tm, d), lambda bi, i: (bi, i, 0)),
                  pl.BlockSpec((1, d), lambda bi, i: (0, 0)),
                  pl.BlockSpec((1, m, kv.shape[2]), lambda bi, i: (bi, 0, 0)),
                  pl.BlockSpec(wq.shape, lambda bi, i: (0, 0)),
                  pl.BlockSpec(wo.shape, lambda bi, i: (0, 0))],
        out_specs=pl.BlockSpec((1, tm, d), lambda bi, i: (bi, i, 0)),
        out_shape=jax.ShapeDtypeStruct((b, s, d), F32),
        compiler_params=_cparams(("parallel", "parallel")),
        name="xattn",
    )(h3, g, kv, wq, wo)


def _rel_bucket(dist):
    exact = REL_BUCKETS // 2
    d = jnp.maximum(dist, exact).astype(F32)
    large = exact + (jnp.log(d / exact) / math.log(REL_MAX_DIST / exact)
                     * (REL_BUCKETS - exact)).astype(jnp.int32)
    return jnp.where(dist < exact, dist, jnp.minimum(large, REL_BUCKETS - 1))


def _band_bias(rel_bias, group, dilation):
    heads = DIL_WIDTH // HEAD_DIM
    qi = jnp.arange(BAND)
    kj = jnp.arange(2 * BAND)
    steps = (qi[:, None] + BAND) - kj[None, :]
    table = rel_bias[:, group * heads:(group + 1) * heads].astype(F32)
    bucket = _rel_bucket(jnp.clip(steps, 0, BAND) * dilation)
    onehot = (bucket[None] == jnp.arange(REL_BUCKETS)[:, None, None]).astype(F32)
    bias = jnp.einsum('bh,bqk->hqk', table * LOG2E, onehot, precision=lax.Precision.HIGHEST)
    return jnp.where((steps >= 0) & (steps <= BAND), bias, NEG)


def _swap_halves(w, width):
    k, n = w.shape
    w = w.reshape(k, n // width, 2, width // 2)
    return w[:, :, ::-1, :].reshape(k, n)


def _pack_w_in(w_in):
    d = w_in.shape[0]
    qscale = HEAD_DIM ** -0.5 * LOG2E
    sb = jnp.concatenate([w_in[:, :SB_WIDTH] * qscale, w_in[:, SB_WIDTH:1536]], axis=1)
    cq = w_in[:, 1536:1920]
    ckv = w_in[:, 1920:2048]
    kr = w_in[:, 2048:2080]
    zeros = jnp.zeros((d, 64), w_in.dtype)
    kr_sw = _swap_halves(kr, MLA_ROPE)
    seg_a = jnp.concatenate([sb, cq, ckv, kr, kr, zeros, kr_sw, kr_sw, zeros], axis=1)
    group = 3 * DIL_WIDTH
    seg_d = []
    for g in range(len(DIL_GROUPS)):
        lo = 2080 + g * group
        seg_d.append(jnp.concatenate([w_in[:, lo:lo + DIL_WIDTH] * qscale, w_in[:, lo + DIL_WIDTH:lo + group]],
                                     axis=1).astype(BF))
    seg_g = w_in[:, 2080 + len(DIL_GROUPS) * group:]
    return seg_a.astype(BF), seg_d, seg_g.astype(BF)


def _pack_w_uq(w_uq):
    k = w_uq.shape[0]
    w = w_uq.reshape(k, MLA_HEADS, MLA_NOPE + MLA_ROPE)
    nope = w[:, :, :MLA_NOPE].reshape(k, MLA_HEADS * MLA_NOPE)
    rope = w[:, :, MLA_NOPE:]

    def pair_layout(rp):
        rp = rp.reshape(k, MLA_HEADS // 2, 2 * MLA_ROPE)
        pad = jnp.zeros((k, MLA_HEADS // 2, 128 - 2 * MLA_ROPE), rp.dtype)
        return jnp.concatenate([rp, pad], axis=-1).reshape(k, (MLA_HEADS // 2) * 128)

    rope_sw = rope.reshape(k, MLA_HEADS, 2, MLA_ROPE // 2)[:, :, ::-1, :].reshape(k, MLA_HEADS, MLA_ROPE)
    return jnp.concatenate([nope, pair_layout(rope), pair_layout(rope_sw)], axis=1).astype(BF)


def _pack_w_ukv(w_ukv):
    k = w_ukv.shape[0]
    w = w_ukv.reshape(k, MLA_HEADS, MLA_NOPE + MLA_V)
    return jnp.concatenate([w[:, :, :MLA_NOPE].reshape(k, -1), w[:, :, MLA_NOPE:].reshape(k, -1)],
                           axis=1).astype(BF)


def _rope_tables(seq):
    half = MLA_ROPE // 2
    freqs = ROPE_THETA ** (-jnp.arange(half, dtype=F32) / half)
    ang = jnp.arange(seq).astype(F32)[:, None] * freqs[None, :]
    cos, sin = jnp.cos(ang), jnp.sin(ang)
    pad = jnp.zeros((seq, 128 - 2 * MLA_ROPE), F32)
    cos_t = jnp.concatenate([cos, cos, cos, cos, pad], axis=1)
    sin_t = jnp.concatenate([-sin, sin, -sin, sin, pad], axis=1)
    return cos_t, sin_t


def kernel(x, mem, ffn1_norm, ffn1_w_gate, ffn1_w_up, ffn1_w_down, mix_norm, w_in, gate_bias, mla_q_norm, mla_w_uq, mla_kv_norm, mla_w_ukv, w_branch_a, w_branch_b, w_branch_c, w_mix_out, rel_bias, xattn_norm, mem_norm, xattn_w_q, xattn_w_kv, xattn_w_o, ffn2_norm, ffn2_w_gate, ffn2_w_up, ffn2_w_down, final_norm):
    b, s, d = x.shape
    t = b * s
    depth = w_in.shape[0]
    cos_t, sin_t = _rope_tables(s)
    biases = [_band_bias(rel_bias, g, dil) for g, (_, dil) in enumerate(DIL_GROUPS)]
    row = lambda v: v.reshape(1, -1)
    bf = lambda w: w.astype(BF)
    fg = row(final_norm)

    h = x.reshape(t, d)
    for l in range(depth):
        h = _ffn(h, row(ffn1_norm[l]), bf(ffn1_w_gate[l]), bf(ffn1_w_up[l]), bf(ffn1_w_down[l]), fg, False)

        w_a, w_d, w_g = _pack_w_in(w_in[l])
        g_mix = row(mix_norm[l])
        h3 = h.reshape(b, s, d)
        seg_a = _norm_proj(h, g_mix, w_a, tn=PROJ_CHUNK)

        a3 = seg_a.reshape(b, s, seg_a.shape[1])
        o_a = _sb_attention(a3)
        qp, kp, vt = _mla_prep(a3, cos_t, sin_t, row(mla_q_norm[l]), row(mla_kv_norm[l]),
                               _pack_w_uq(mla_w_uq[l]), _pack_w_ukv(mla_w_ukv[l]), tm=MLA_BLOCK)
        o_b = _mla_attention(qp, kp, vt)
        ocs, lses = [], []
        for g, (_, dil) in enumerate(DIL_GROUPS):
            o, lse = _dil_attention(_dil_proj(h3, g_mix, w_d[g], dil), biases[g], dil)
            ocs.append(o)
            lses.append(lse)
        h3 = _merge(h3, o_a, o_b, ocs, lses, g_mix, w_g, gate_bias[l],
                    bf(w_branch_a[l]), bf(w_branch_b[l]), bf(w_branch_c[l]), bf(w_mix_out[l]))

        kv = _mem_kv(mem, row(mem_norm[l]), bf(xattn_w_kv[l]))
        h = _xattn(h3, row(xattn_norm[l]), kv, bf(xattn_w_q[l]), bf(xattn_w_o[l])).reshape(t, d)

        h = _ffn(h, row(ffn2_norm[l]), bf(ffn2_w_gate[l]), bf(ffn2_w_up[l]), bf(ffn2_w_down[l]), fg,
                 l == depth - 1)
    return h.reshape(b, s, d)
```

```python
import functools
import math

import jax
import jax.numpy as jnp
import numpy as np
from jax import lax
from jax.experimental import pallas as pl
from jax.experimental.pallas import tpu as pltpu

BF = jnp.bfloat16
F32 = jnp.float32

EPS = 1e-6
NEG = -1e30
FFN_RESIDUAL = 0.5

SB_WIDTH = 512
MLA_HEADS = 8
MLA_Q_RANK = 384
MLA_KV_RANK = 128
MLA_NOPE = 64
MLA_ROPE = 32
MLA_V = 64
ROPE_THETA = 10000.0
DIL_GROUPS = ((128, 1), (512, 4), (2048, 16))
DIL_WIDTH = 512
REL_BUCKETS = 32
REL_MAX_DIST = 2048
X_HEADS = 4
X_HEAD_DIM = 128
HEAD_DIM = 64
BAND = 128
LANES = 128
MLA_PAIR = 2 * LANES

COL_CQ = 3 * SB_WIDTH
COL_CKV = COL_CQ + MLA_Q_RANK
COL_KR = COL_CKV + MLA_KV_RANK
COL_DIL = COL_KR + MLA_ROPE
SB_DEAD_LOG2 = -150.0
LOG2E = math.log2(math.e)
LN2 = math.log(2.0)

V7X_VMEM_LIMIT = 48 * 1024 * 1024
FF_CHUNK = 256
MLA_BLOCK = 512
DIL_RES_PER_STEP = 4
PROJ_CHUNK = 768


def _cparams(sem):
    return pltpu.CompilerParams(dimension_semantics=sem, vmem_limit_bytes=V7X_VMEM_LIMIT)


def _rms(x, g):
    ms = jnp.mean(x * x, axis=-1, keepdims=True)
    return x * lax.rsqrt(ms + EPS) * g


def _sigmoid(x):
    return 0.5 * jnp.tanh(0.5 * x) + 0.5


def _dot(a, b):
    return jnp.dot(a, b, preferred_element_type=F32)


def _dot_t(a, b):
    return lax.dot_general(a, b, (((1,), (1,)), ((), ())), preferred_element_type=F32)


def _resident(shape):
    nd = len(shape)
    return pl.BlockSpec(shape, lambda *_: (0,) * nd, pipeline_mode=pl.Buffered(1))


def _ffn_body(h_ref, g_ref, wg_ref, wu_ref, wd_ref, fg_ref, o_ref, act_ref, *, final):
    x = h_ref[...]
    u = _rms(x, g_ref[...]).astype(BF)
    d_ff = wg_ref.shape[1]
    for c in range(d_ff // FF_CHUNK):
        sl = slice(c * FF_CHUNK, (c + 1) * FF_CHUNK)
        a = _dot(u, wg_ref[:, sl])
        b = _dot(u, wu_ref[:, sl])
        act_ref[:, sl] = (a * jax.nn.sigmoid(a) * b).astype(BF)
    y = x + FFN_RESIDUAL * _dot(act_ref[...], wd_ref[...])
    if final:
        y = _rms(y, fg_ref[...])
    o_ref[...] = y


def _ffn(h, g, wg, wu, wd, fg, final, tm=512):
    t, d = h.shape
    d_ff = wg.shape[1]
    tm = min(tm, t)
    return pl.pallas_call(
        functools.partial(_ffn_body, final=final),
        grid=(t // tm,),
        in_specs=[
            pl.BlockSpec((tm, d), lambda i: (i, 0)),
            _resident((1, d)),
            _resident((d, d_ff)),
            _resident((d, d_ff)),
            _resident((d_ff, d)),
            _resident((1, d)),
        ],
        out_specs=pl.BlockSpec((tm, d), lambda i: (i, 0)),
        out_shape=jax.ShapeDtypeStruct((t, d), F32),
        scratch_shapes=[pltpu.VMEM((tm, d_ff), BF)],
        compiler_params=_cparams(("parallel",)),
        name="ffn",
    )(h, g, wg, wu, wd, fg)


def _proj_body(h_ref, g_ref, w_ref, o_ref, *, tn):
    u = _rms(h_ref[...], g_ref[...]).astype(BF)
    for c in range(w_ref.shape[1] // tn):
        sl = slice(c * tn, (c + 1) * tn)
        o_ref[:, sl] = _dot(u, w_ref[:, sl]).astype(o_ref.dtype)


def _norm_proj(h, g, w, tn, tm=512):
    t, d = h.shape
    n = w.shape[1]
    tm = min(tm, t)
    return pl.pallas_call(
        functools.partial(_proj_body, tn=tn),
        grid=(t // tm,),
        in_specs=[
            pl.BlockSpec((tm, d), lambda i: (i, 0)),
            _resident((1, d)),
            _resident((d, n)),
        ],
        out_specs=pl.BlockSpec((tm, n), lambda i: (i, 0)),
        out_shape=jax.ShapeDtypeStruct((t, n), BF),
        compiler_params=_cparams(("parallel",)),
        name="norm_proj",
    )(h, g, w)


def _dil_proj_body(h_ref, g_ref, w_ref, o_ref, *stage, r):
    rows = h_ref.shape[1] // r
    n = w_ref.shape[1]
    u = _rms(h_ref[0], g_ref[...])
    if r == 1:
        ub = u.astype(BF)
        for c in range(n // PROJ_CHUNK):
            sl = slice(c * PROJ_CHUNK, (c + 1) * PROJ_CHUNK)
            o_ref[0, :, sl] = _dot(ub, w_ref[:, sl]).astype(o_ref.dtype)
        return
    (stage,) = stage
    chunks = u.shape[1] // LANES
    for c in range(chunks):
        stage[c] = u[:, c * LANES:(c + 1) * LANES]
    for res in range(r):
        ur = jnp.concatenate([stage[c, pl.ds(res, rows, stride=r), :] for c in range(chunks)], axis=1)
        o_ref[0, :, res * n:(res + 1) * n] = _dot(ur.astype(BF), w_ref[...]).astype(o_ref.dtype)


def _dil_proj(h3, g, w, r):
    b, s, d = h3.shape
    n = w.shape[1]
    tm = min(s, max(1024, BAND * r) if r > 1 else 512)
    return pl.pallas_call(
        functools.partial(_dil_proj_body, r=r),
        grid=(b, s // tm),
        in_specs=[
            pl.BlockSpec((1, tm, d), lambda bi, i: (bi, i, 0)),
            pl.BlockSpec((1, d), lambda bi, i: (0, 0)),
            pl.BlockSpec((d, n), lambda bi, i: (0, 0)),
        ],
        out_specs=pl.BlockSpec((1, tm // r, r * n), lambda bi, i: (bi, i, 0)),
        out_shape=jax.ShapeDtypeStruct((b, s // r, r * n), BF),
        scratch_shapes=[pltpu.VMEM((d // LANES, tm, LANES), F32)] if r > 1 else [],
        compiler_params=_cparams(("parallel", "parallel")),
        name=f"dil_proj_r{r}",
    )(h3, g, w)


def _sb_body(q_ref, k_ref, v_ref, o_ref, *, tq, pairs):
    i = pl.program_id(2)
    pw = 2 * HEAD_DIM
    lane = lax.broadcasted_iota(jnp.int32, (1, pw), 1)
    first = lane < HEAD_DIM
    q_heads = []
    for p in range(pairs):
        q = q_ref[0, :, p * pw:(p + 1) * pw]
        zero = jnp.zeros_like(q)
        q_heads += [jnp.where(first, q, zero), jnp.where(first, zero, q)]
    row = lax.broadcasted_iota(jnp.int32, (tq, tq), 0)
    col = lax.broadcasted_iota(jnp.int32, (tq, tq), 1)
    strict = col < row
    later = jnp.where(row > col, 1.0, 0.0).astype(BF)

    def block(j, accs, rs, diag):
        start = pl.multiple_of(j * tq, tq)
        heads = range(2 * pairs)
        ks = [k_ref[0, pl.ds(start, tq), p * pw:(p + 1) * pw] for p in range(pairs)]
        vs = [v_ref[0, pl.ds(start, tq), p * pw:(p + 1) * pw] for p in range(pairs)]
        zs = [_dot_t(q_heads[h], ks[h // 2]) for h in heads]
        log_betas, log_keeps = [], []
        for z in zs:
            sp = jnp.log(1.0 + jnp.exp2(-jnp.abs(z))) * LOG2E
            log_beta = jnp.minimum(z, 0.0) - sp
            log_keep = log_beta - z
            if diag:
                log_keep = jnp.where(strict, log_keep, 0.0)
            log_betas.append(log_beta)
            log_keeps.append(log_keep)
        afters = [_dot(log_keeps[h].astype(BF), later) for h in heads]
        ws = []
        for h in heads:
            w = jnp.exp2(log_betas[h] + afters[h] + rs[h])
            if diag:
                w = jnp.where(strict, w, 0.0)
            ws.append(w.astype(BF))
        outs = [_dot(ws[h], vs[h // 2]) for h in heads]
        new_rs = [rs[h] + jnp.sum(log_keeps[h], axis=-1, keepdims=True) for h in heads]
        new_accs = [accs[p] + jnp.where(first, outs[2 * p], outs[2 * p + 1]) for p in range(pairs)]
        return tuple(new_accs), tuple(new_rs)

    def remaining(rs):
        return jnp.max(functools.reduce(jnp.maximum, rs))

    def cond(c):
        return (c[0] < i) & (c[1] > SB_DEAD_LOG2)

    def body(c):
        accs, rs = block(i - 1 - c[0], c[2], c[3], False)
        return c[0] + 1, remaining(rs), accs, rs

    accs = tuple(jnp.zeros((tq, pw), F32) for _ in range(pairs))
    rs = tuple(jnp.zeros((tq, 1), F32) for _ in range(2 * pairs))
    accs, rs = block(i, accs, rs, True)
    out = lax.while_loop(cond, body, (jnp.int32(0), remaining(rs), accs, rs))
    for p in range(pairs):
        o_ref[0, :, p * pw:(p + 1) * pw] = out[2][p].astype(o_ref.dtype)


def _sb_attention(a3, tq=256, pairs=4):
    b, s, _ = a3.shape
    tq = min(tq, s)
    w = 2 * HEAD_DIM * pairs
    groups = SB_WIDTH // w
    return pl.pallas_call(
        functools.partial(_sb_body, tq=tq, pairs=pairs),
        grid=(b, groups, s // tq),
        in_specs=[
            pl.BlockSpec((1, tq, w), lambda bi, p, i: (bi, i, p)),
            pl.BlockSpec((1, s, w), lambda bi, p, i: (bi, 0, groups + p)),
            pl.BlockSpec((1, s, w), lambda bi, p, i: (bi, 0, 2 * groups + p)),
        ],
        out_specs=pl.BlockSpec((1, tq, w), lambda bi, p, i: (bi, i, p)),
        out_shape=jax.ShapeDtypeStruct((b, s, SB_WIDTH), BF),
        compiler_params=_cparams(("parallel", "parallel", "arbitrary")),
        name="sb_attn",
    )(a3, a3, a3)


def _mla_prep_body(cq_ref, ckv_ref, kr_ref, cos_ref, sin_ref, qg_ref, kvg_ref, wq_ref, wkv_ref,
                   qp_ref, kp_ref, v_ref, *, scale):
    nq = _rms(cq_ref[0].astype(F32), qg_ref[...]).astype(BF)
    q = _dot(nq, wq_ref[...])
    nkv = _rms(ckv_ref[0].astype(F32), kvg_ref[...]).astype(BF)
    kv = _dot(nkv, wkv_ref[...])
    cos = cos_ref[...]
    sin = sin_ref[...]
    kr = kr_ref[0].astype(F32)
    k_pe = (kr[:, :128] * cos + kr[:, 128:] * sin).astype(BF)
    width = q.shape[1] // 3
    for p in range(width // 128):
        lo, hi = p * 128, (p + 1) * 128
        qp_ref[0, :, 2 * lo:2 * lo + 128] = (q[:, lo:hi] * scale).astype(BF)
        q_pe = q[:, width + lo:width + hi] * cos + q[:, 2 * width + lo:2 * width + hi] * sin
        qp_ref[0, :, 2 * lo + 128:2 * hi] = (q_pe * scale).astype(BF)
        kp_ref[0, :, 2 * lo:2 * lo + 128] = kv[:, lo:hi].astype(BF)
        kp_ref[0, :, 2 * lo + 128:2 * hi] = k_pe
    v_ref[0, 0] = kv[:, width:].T.astype(BF)


def _mla_prep(a3, cos, sin, qg, kvg, wq, wkv, tm):
    b, s, _ = a3.shape
    tm = min(tm, s)
    width = MLA_HEADS * MLA_NOPE
    scale = (MLA_NOPE + MLA_ROPE) ** -0.5 * LOG2E
    out_sds = lambda n: jax.ShapeDtypeStruct((b, s, n), BF)
    return pl.pallas_call(
        functools.partial(_mla_prep_body, scale=scale),
        grid=(b, s // tm),
        in_specs=[
            pl.BlockSpec((1, tm, MLA_Q_RANK), lambda bi, i: (bi, i, COL_CQ // MLA_Q_RANK)),
            pl.BlockSpec((1, tm, MLA_KV_RANK), lambda bi, i: (bi, i, COL_CKV // MLA_KV_RANK)),
            pl.BlockSpec((1, tm, 2 * LANES), lambda bi, i: (bi, i, COL_KR // (2 * LANES))),
            pl.BlockSpec((tm, 128), lambda bi, i: (i, 0)),
            pl.BlockSpec((tm, 128), lambda bi, i: (i, 0)),
            pl.BlockSpec((1, MLA_Q_RANK), lambda bi, i: (0, 0)),
            pl.BlockSpec((1, MLA_KV_RANK), lambda bi, i: (0, 0)),
            pl.BlockSpec(wq.shape, lambda bi, i: (0, 0)),
            pl.BlockSpec(wkv.shape, lambda bi, i: (0, 0)),
        ],
        out_specs=[
            pl.BlockSpec((1, tm, 2 * width), lambda bi, i: (bi, i, 0)),
            pl.BlockSpec((1, tm, 2 * width), lambda bi, i: (bi, i, 0)),
            pl.BlockSpec((1, 1, width, tm), lambda bi, i: (bi, i, 0, 0)),
        ],
        out_shape=[out_sds(2 * width), out_sds(2 * width),
                   jax.ShapeDtypeStruct((b, s // tm, width, tm), BF)],
        compiler_params=_cparams(("parallel", "parallel")),
        name="mla_prep",
    )(a3, a3, a3, cos, sin, qg, kvg, wq, wkv)


def _mla_body(q_ref, k_ref, v_ref, o_ref, *, tq, pairs):
    i = pl.program_id(2)
    lane = lax.broadcasted_iota(jnp.int32, (1, MLA_PAIR), 1)
    sel0 = (lane < 64) | ((lane >= 128) & (lane < 160))
    sel1 = ((lane >= 64) & (lane < 128)) | ((lane >= 160) & (lane < 192))
    q_heads = []
    for p in range(pairs):
        q = q_ref[0, :, p * MLA_PAIR:(p + 1) * MLA_PAIR]
        zero = jnp.zeros_like(q)
        q_heads += [jnp.where(sel0, q, zero), jnp.where(sel1, q, zero)]
    key = lax.broadcasted_iota(jnp.int32, (tq, tq), 0)
    qry = lax.broadcasted_iota(jnp.int32, (tq, tq), 1)
    causal = key <= qry
    first = lax.broadcasted_iota(jnp.int32, (128, 1), 0) < 64

    def block(j, carry, diag, nblk=1):
        start = pl.multiple_of(j * tq, tq)
        heads = range(2 * pairs)
        ks = [k_ref[0, pl.ds(start, nblk * tq), p * MLA_PAIR:(p + 1) * MLA_PAIR] for p in range(pairs)]
        v_heads = []
        for p in range(pairs):
            vt = jnp.concatenate([v_ref[0, j + n, p * 128:(p + 1) * 128, :] for n in range(nblk)], axis=1)
            ones = jnp.ones_like(vt)
            v_heads += [jnp.where(first, vt, ones), jnp.where(first, ones, vt)]
        ss = [_dot_t(ks[h // 2], q_heads[h]) for h in heads]
        if diag:
            ss = [jnp.where(causal, s, NEG) for s in ss]
        m_news = [jnp.maximum(carry[h][0], jnp.max(ss[h], axis=0, keepdims=True)) for h in heads]
        prs = [jnp.exp2(ss[h] - m_news[h]).astype(BF) for h in heads]
        pvs = [_dot(v_heads[h], prs[h]) for h in heads]
        return tuple((m_news[h], jnp.exp2(carry[h][0] - m_news[h]) * carry[h][1] + pvs[h]) for h in heads)

    one = (jnp.full((1, tq), NEG, F32), jnp.zeros((128, tq), F32))
    carry = lax.fori_loop(0, i // 2, lambda t, c: block(2 * t, c, False, 2), (one,) * (2 * pairs))
    carry = lax.cond(i % 2 == 1, lambda c: block(i - 1, c, False), lambda c: c, carry)
    carry = block(i, carry, True)
    for p in range(pairs):
        a0, a1 = carry[2 * p][1], carry[2 * p + 1][1]
        lsw = jnp.where(first, a1, a0)
        l = jnp.concatenate([lsw[64:], lsw[:64]], axis=0)
        out_t = jnp.where(first, a0, a1) / l
        o_ref[0, :, p * 128:(p + 1) * 128] = out_t.T.astype(o_ref.dtype)


def _mla_attention(qp, kp, vt, pairs=2):
    b, s, _ = qp.shape
    nblk, width, tq = vt.shape[1:]
    groups = MLA_HEADS // (2 * pairs)
    return pl.pallas_call(
        functools.partial(_mla_body, tq=tq, pairs=pairs),
        grid=(b, groups, s // tq),
        in_specs=[
            pl.BlockSpec((1, tq, MLA_PAIR * pairs), lambda bi, p, i: (bi, i, p)),
            pl.BlockSpec((1, s, MLA_PAIR * pairs), lambda bi, p, i: (bi, 0, p)),
            pl.BlockSpec((1, nblk, 128 * pairs, tq), lambda bi, p, i: (bi, 0, p, 0)),
        ],
        out_specs=pl.BlockSpec((1, tq, 128 * pairs), lambda bi, p, i: (bi, i, p)),
        out_shape=jax.ShapeDtypeStruct((b, s, MLA_HEADS * MLA_V), BF),
        compiler_params=_cparams(("parallel", "parallel", "arbitrary")),
        name="mla_attn",
    )(qp, kp, vt)


def _dil_body(cur_ref, prev_ref, bias_ref, o_ref, lse_ref, *, tl, nres):
    j = pl.program_id(2)
    lane = lax.broadcasted_iota(jnp.int32, (1, 2 * HEAD_DIM), 1)
    first = lane < HEAD_DIM
    kj = lax.broadcasted_iota(jnp.int32, (BAND, 2 * BAND), 1)
    has_prev = (kj >= BAND) | (j > 0)
    pw = 2 * HEAD_DIM
    pairs = DIL_WIDTH // pw
    heads = range(2 * pairs)
    for rr, sb in [(rr, sb) for rr in range(nres) for sb in range(tl // BAND)]:
        rows = slice(sb * BAND, (sb + 1) * BAND)
        prev = slice((sb - 1) * BAND, sb * BAND)
        q_cols, k_cols, v_cols = (slice((3 * rr + c) * DIL_WIDTH, (3 * rr + c + 1) * DIL_WIDTH) for c in range(3))
        out_base = rr * DIL_WIDTH
        k_prev = prev_ref[0, :, k_cols] if sb == 0 else cur_ref[0, prev, k_cols]
        v_prev = prev_ref[0, :, v_cols] if sb == 0 else cur_ref[0, prev, v_cols]
        kcat = jnp.concatenate([k_prev, cur_ref[0, rows, k_cols]], axis=0)
        vcat = jnp.concatenate([v_prev, cur_ref[0, rows, v_cols]], axis=0)
        qs = cur_ref[0, rows, q_cols]
        q_heads, v_heads = [], []
        for p in range(pairs):
            qb = qs[:, p * pw:(p + 1) * pw]
            vb = vcat[:, p * pw:(p + 1) * pw]
            zero, ones = jnp.zeros_like(qb), jnp.ones_like(vb)
            q_heads += [jnp.where(first, qb, zero), jnp.where(first, zero, qb)]
            v_heads += [jnp.where(first, vb, ones), jnp.where(first, ones, vb)]
        ss = [_dot_t(q_heads[h], kcat[:, (h // 2) * pw:(h // 2 + 1) * pw]) + bias_ref[h] for h in heads]
        if sb == 0:
            ss = [jnp.where(has_prev, s, NEG) for s in ss]
        ms = [jnp.max(s, axis=-1, keepdims=True) for s in ss]
        prs = [jnp.exp2(ss[h] - ms[h]).astype(BF) for h in heads]
        pvs = [_dot(prs[h], v_heads[h]) for h in heads]
        for p in range(pairs):
            a0, a1 = pvs[2 * p], pvs[2 * p + 1]
            cols = slice(out_base + p * pw, out_base + (p + 1) * pw)
            l = pltpu.roll(jnp.where(first, a1, a0), HEAD_DIM, 1)
            o_ref[0, rows, cols] = (jnp.where(first, a0, a1) / l).astype(o_ref.dtype)
            lse_ref[0, rows, cols] = jnp.where(first, ms[2 * p], ms[2 * p + 1]) * LN2 + jnp.log(l)


def _dil_attention(view, bias, dilation, tl=512):
    r = dilation
    batch, length, _ = view.shape
    tl = min(tl, length)
    sub = tl // BAND
    nres = min(r, DIL_RES_PER_STEP)
    wide = nres * 3 * DIL_WIDTH
    cur = pl.BlockSpec((1, tl, wide), lambda bi, rg, j: (bi, j, rg))
    prev = pl.BlockSpec((1, BAND, wide), lambda bi, rg, j: (bi, jnp.maximum(j * sub - 1, 0), rg))
    out_spec = pl.BlockSpec((1, tl, nres * DIL_WIDTH), lambda bi, rg, j: (bi, j, rg))
    return pl.pallas_call(
        functools.partial(_dil_body, tl=tl, nres=nres),
        grid=(batch, r // nres, length // tl),
        in_specs=[cur, prev, pl.BlockSpec(bias.shape, lambda bi, rg, j: (0, 0, 0))],
        out_specs=[out_spec, out_spec],
        out_shape=[jax.ShapeDtypeStruct((batch, length, r * DIL_WIDTH), BF),
                   jax.ShapeDtypeStruct((batch, length, r * DIL_WIDTH), F32)],
        compiler_params=_cparams(("parallel", "parallel", "arbitrary")),
        name=f"dil_attn_r{r}",
    )(view, view, bias)


def _merge_body(h_ref, oa_ref, ob_ref, oc0_ref, oc1_ref, oc2_ref, l0_ref, l1_ref, l2_ref,
                gn_ref, wg_ref, gbias_ref, wa_ref, wb_ref, wc_ref, wo_ref, o_ref, *scratch):
    tm, d = h_ref.shape[1:]
    scratch = list(scratch)
    h = h_ref[0]
    u = _rms(h, gn_ref[...]).astype(BF)

    def gate(c):
        return _sigmoid(_dot(u, wg_ref[:, c * d:(c + 1) * d]) + gbias_ref[c:c + 1, :])

    def token_major(ref, r):
        if r == 1:
            return ref[0].astype(F32)
        buf = scratch.pop(0)
        chunks = DIL_WIDTH // LANES
        for res in range(r):
            for c in range(chunks):
                lo = res * DIL_WIDTH + c * LANES
                buf[c, pl.ds(res, tm // r, stride=r), :] = ref[0, :, lo:lo + LANES].astype(F32)
        return jnp.concatenate([buf[c] for c in range(chunks)], axis=1)

    dils = [dil for _, dil in DIL_GROUPS]
    merged = gate(0) * _dot(oa_ref[0], wa_ref[...]) + gate(1) * _dot(ob_ref[0], wb_ref[...])
    gate_c = gate(2)
    l0, l1, l2 = (token_major(ref, r) for ref, r in zip((l0_ref, l1_ref, l2_ref), dils))
    mx = jnp.maximum(jnp.maximum(l0, l1), l2)
    e0, e1, e2 = jnp.exp(l0 - mx), jnp.exp(l1 - mx), jnp.exp(l2 - mx)
    o0, o1, o2 = (token_major(ref, r) for ref, r in zip((oc0_ref, oc1_ref, oc2_ref), dils))
    oc = (e0 * o0 + e1 * o1 + e2 * o2) / (e0 + e1 + e2)
    merged = merged + gate_c * _dot(oc.astype(BF), wc_ref[...])
    o_ref[0] = h + _dot(merged.astype(BF), wo_ref[...])


def _merge(h3, oa, ob, ocs, lses, g_mix, w_gate, gbias, wa, wb, wc, wo, tm=512):
    b, s, d = h3.shape
    tm = min(tm, s)
    row = lambda n: pl.BlockSpec((1, tm, n), lambda bi, i: (bi, i, 0))
    views = [pl.BlockSpec((1, tm // r, r * DIL_WIDTH), lambda bi, i: (bi, i, 0)) for _, r in DIL_GROUPS]
    n_buf = 2 * sum(1 for _, r in DIL_GROUPS if r > 1)
    params = (g_mix, w_gate, gbias, wa, wb, wc, wo)
    return pl.pallas_call(
        _merge_body,
        grid=(b, s // tm),
        in_specs=[row(d), row(SB_WIDTH), row(SB_WIDTH)] + views + views + [_resident(p.shape) for p in params],
        out_specs=row(d),
        out_shape=jax.ShapeDtypeStruct((b, s, d), F32),
        scratch_shapes=[pltpu.VMEM((DIL_WIDTH // LANES, tm, LANES), F32)] * n_buf,
        compiler_params=_cparams(("parallel", "parallel")),
        name="merge",
    )(h3, oa, ob, *ocs, *lses, *params)


def _memkv_body(m_ref, g_ref, w_ref, o_ref):
    u = _rms(m_ref[0], g_ref[...]).astype(BF)
    o_ref[0] = _dot(u, w_ref[...]).astype(o_ref.dtype)


def _mem_kv(mem, g, w):
    b, m, d = mem.shape
    n = w.shape[1]
    return pl.pallas_call(
        _memkv_body,
        grid=(b,),
        in_specs=[pl.BlockSpec((1, m, d), lambda i: (i, 0, 0)),
                  pl.BlockSpec((1, d), lambda i: (0, 0)),
                  pl.BlockSpec((d, n), lambda i: (0, 0))],
        out_specs=pl.BlockSpec((1, m, n), lambda i: (i, 0, 0)),
        out_shape=jax.ShapeDtypeStruct((b, m, n), BF),
        compiler_params=_cparams(("parallel",)),
        name="mem_kv",
    )(mem, g, w)


def _xattn_body(h_ref, g_ref, kv_ref, wq_ref, wo_ref, o_ref):
    x = h_ref[0]
    u = _rms(x, g_ref[...]).astype(BF)
    q = _dot(u, wq_ref[...]).astype(BF)
    kv = kv_ref[0]
    width = X_HEADS * X_HEAD_DIM
    scale = X_HEAD_DIM ** -0.5
    outs = []
    for h in range(X_HEADS):
        cols = slice(h * X_HEAD_DIM, (h + 1) * X_HEAD_DIM)
        s = _dot_t(q[:, cols], kv[:, cols]) * scale
        m = jnp.max(s, axis=-1, keepdims=True)
        e = jnp.exp(s - m)
        p = e / jnp.sum(e, axis=-1, keepdims=True)
        outs.append(_dot(p.astype(BF), kv[:, width + h * X_HEAD_DIM:width + (h + 1) * X_HEAD_DIM]))
    o = jnp.concatenate(outs, axis=-1).astype(BF)
    o_ref[0] = x + _dot(o, wo_ref[...])


def _xattn(h3, g, kv, wq, wo, tm=512):
    b, s, d = h3.shape
    tm = min(tm, s)
    m = kv.shape[1]
    return pl.pallas_call(
        _xattn_body,
        grid=(b, s // tm),
        in_specs=[pl.BlockSpec((1, tm, d), lambda bi, i: (bi, i, 0)),
                  pl.BlockSpec((1, d), lambda bi, i: (0, 0)),
                  pl.BlockSpec((1, m, kv.shape[2]), lambda bi, i: (bi, 0, 0)),
                  pl.BlockSpec(wq.shape, lambda bi, i: (0, 0)),
                  pl.BlockSpec(wo.shape, lambda bi, i: (0, 0))],
        out_specs=pl.BlockSpec((1, tm, d), lambda bi, i: (bi, i, 0)),
        out_shape=jax.ShapeDtypeStruct((b, s, d), F32),
        compiler_params=_cparams(("parallel", "parallel")),
        name="xattn",
    )(h3, g, kv, wq, wo)


def _rel_bucket(dist):
    exact = REL_BUCKETS // 2
    d = jnp.maximum(dist, exact).astype(F32)
    large = exact + (jnp.log(d / exact) / math.log(REL_MAX_DIST / exact)
                     * (REL_BUCKETS - exact)).astype(jnp.int32)
    return jnp.where(dist < exact, dist, jnp.minimum(large, REL_BUCKETS - 1))


def _band_bias(rel_bias, group, dilation):
    heads = DIL_WIDTH // HEAD_DIM
    qi = jnp.arange(BAND)
    kj = jnp.arange(2 * BAND)
    steps = (qi[:, None] + BAND) - kj[None, :]
    table = rel_bias[:, group * heads:(group + 1) * heads].astype(F32)
    bucket = _rel_bucket(jnp.clip(steps, 0, BAND) * dilation)
    onehot = (bucket[None] == jnp.arange(REL_BUCKETS)[:, None, None]).astype(F32)
    bias = jnp.einsum('bh,bqk->hqk', table * LOG2E, onehot, precision=lax.Precision.HIGHEST)
    return jnp.where((steps >= 0) & (steps <= BAND), bias, NEG)


def _swap_halves(w, width):
    k, n = w.shape
    w = w.reshape(k, n // width, 2, width // 2)
    return w[:, :, ::-1, :].reshape(k, n)


def _pack_w_in(w_in):
    d = w_in.shape[0]
    qscale = HEAD_DIM ** -0.5 * LOG2E
    sb = jnp.concatenate([w_in[:, :SB_WIDTH] * qscale, w_in[:, SB_WIDTH:COL_CQ]], axis=1)
    cq = w_in[:, COL_CQ:COL_CKV]
    ckv = w_in[:, COL_CKV:COL_KR]
    kr = w_in[:, COL_KR:COL_DIL]
    zeros = jnp.zeros((d, 64), w_in.dtype)
    kr_sw = _swap_halves(kr, MLA_ROPE)
    seg_a = jnp.concatenate([sb, cq, ckv, kr, kr, zeros, kr_sw, kr_sw, zeros], axis=1)
    group = 3 * DIL_WIDTH
    seg_d = []
    for g in range(len(DIL_GROUPS)):
        lo = COL_DIL + g * group
        seg_d.append(jnp.concatenate([w_in[:, lo:lo + DIL_WIDTH] * qscale, w_in[:, lo + DIL_WIDTH:lo + group]],
                                     axis=1).astype(BF))
    seg_g = w_in[:, COL_DIL + len(DIL_GROUPS) * group:]
    return seg_a.astype(BF), seg_d, seg_g.astype(BF)


def _pack_w_uq(w_uq):
    k = w_uq.shape[0]
    w = w_uq.reshape(k, MLA_HEADS, MLA_NOPE + MLA_ROPE)
    nope = w[:, :, :MLA_NOPE].reshape(k, MLA_HEADS * MLA_NOPE)
    rope = w[:, :, MLA_NOPE:]

    def pair_layout(rp):
        rp = rp.reshape(k, MLA_HEADS // 2, 2 * MLA_ROPE)
        pad = jnp.zeros((k, MLA_HEADS // 2, 128 - 2 * MLA_ROPE), rp.dtype)
        return jnp.concatenate([rp, pad], axis=-1).reshape(k, (MLA_HEADS // 2) * 128)

    rope_sw = rope.reshape(k, MLA_HEADS, 2, MLA_ROPE // 2)[:, :, ::-1, :].reshape(k, MLA_HEADS, MLA_ROPE)
    return jnp.concatenate([nope, pair_layout(rope), pair_layout(rope_sw)], axis=1).astype(BF)


def _pack_w_ukv(w_ukv):
    k = w_ukv.shape[0]
    w = w_ukv.reshape(k, MLA_HEADS, MLA_NOPE + MLA_V)
    return jnp.concatenate([w[:, :, :MLA_NOPE].reshape(k, -1), w[:, :, MLA_NOPE:].reshape(k, -1)],
                           axis=1).astype(BF)


def _rope_tables(seq):
    half = MLA_ROPE // 2
    freqs = ROPE_THETA ** (-jnp.arange(half, dtype=F32) / half)
    ang = jnp.arange(seq).astype(F32)[:, None] * freqs[None, :]
    cos, sin = jnp.cos(ang), jnp.sin(ang)
    pad = jnp.zeros((seq, 128 - 2 * MLA_ROPE), F32)
    cos_t = jnp.concatenate([cos, cos, cos, cos, pad], axis=1)
    sin_t = jnp.concatenate([-sin, sin, -sin, sin, pad], axis=1)
    return cos_t, sin_t


def kernel(x, mem, ffn1_norm, ffn1_w_gate, ffn1_w_up, ffn1_w_down, mix_norm, w_in, gate_bias, mla_q_norm, mla_w_uq, mla_kv_norm, mla_w_ukv, w_branch_a, w_branch_b, w_branch_c, w_mix_out, rel_bias, xattn_norm, mem_norm, xattn_w_q, xattn_w_kv, xattn_w_o, ffn2_norm, ffn2_w_gate, ffn2_w_up, ffn2_w_down, final_norm):
    b, s, d = x.shape
    t = b * s
    depth = w_in.shape[0]
    cos_t, sin_t = _rope_tables(s)
    biases = [_band_bias(rel_bias, g, dil) for g, (_, dil) in enumerate(DIL_GROUPS)]
    row = lambda v: v.reshape(1, -1)
    bf = lambda w: w.astype(BF)
    fg = row(final_norm)

    h = x.reshape(t, d)
    for l in range(depth):
        h = _ffn(h, row(ffn1_norm[l]), bf(ffn1_w_gate[l]), bf(ffn1_w_up[l]), bf(ffn1_w_down[l]), fg, False)

        w_a, w_d, w_g = _pack_w_in(w_in[l])
        g_mix = row(mix_norm[l])
        h3 = h.reshape(b, s, d)
        seg_a = _norm_proj(h, g_mix, w_a, tn=PROJ_CHUNK)

        a3 = seg_a.reshape(b, s, seg_a.shape[1])
        o_a = _sb_attention(a3)
        qp, kp, vt = _mla_prep(a3, cos_t, sin_t, row(mla_q_norm[l]), row(mla_kv_norm[l]),
                               _pack_w_uq(mla_w_uq[l]), _pack_w_ukv(mla_w_ukv[l]), tm=MLA_BLOCK)
        o_b = _mla_attention(qp, kp, vt)
        ocs, lses = [], []
        for g, (_, dil) in enumerate(DIL_GROUPS):
            o, lse = _dil_attention(_dil_proj(h3, g_mix, w_d[g], dil), biases[g], dil)
            ocs.append(o)
            lses.append(lse)
        h3 = _merge(h3, o_a, o_b, ocs, lses, g_mix, w_g, gate_bias[l],
                    bf(w_branch_a[l]), bf(w_branch_b[l]), bf(w_branch_c[l]), bf(w_mix_out[l]))

        kv = _mem_kv(mem, row(mem_norm[l]), bf(xattn_w_kv[l]))
        h = _xattn(h3, row(xattn_norm[l]), kv, bf(xattn_w_q[l]), bf(xattn_w_o[l])).reshape(t, d)

        h = _ffn(h, row(ffn2_norm[l]), bf(ffn2_w_gate[l]), bf(ffn2_w_up[l]), bf(ffn2_w_down[l]), fg,
                 l == depth - 1)
    return h.reshape(b, s, d)
```

```python
import functools
import math

import jax
import jax.numpy as jnp
import numpy as np
from jax import lax
from jax.experimental import pallas as pl
from jax.experimental.pallas import tpu as pltpu

BF = jnp.bfloat16
F32 = jnp.float32

EPS = 1e-6
NEG = -1e30
FFN_RESIDUAL = 0.5

SB_WIDTH = 512
MLA_HEADS = 8
MLA_Q_RANK = 384
MLA_KV_RANK = 128
MLA_NOPE = 64
MLA_ROPE = 32
MLA_V = 64
ROPE_THETA = 10000.0
DIL_GROUPS = ((128, 1), (512, 4), (2048, 16))
DIL_WIDTH = 512
REL_BUCKETS = 32
REL_MAX_DIST = 2048
X_HEADS = 4
X_HEAD_DIM = 128
HEAD_DIM = 64
BAND = 128
LANES = 128
MLA_PAIR = 2 * LANES

COL_CQ = 3 * SB_WIDTH
COL_CKV = COL_CQ + MLA_Q_RANK
COL_KR = COL_CKV + MLA_KV_RANK
COL_DIL = COL_KR + MLA_ROPE
SB_DEAD_LOG2 = -150.0
LOG2E = math.log2(math.e)
LN2 = math.log(2.0)

V7X_VMEM_LIMIT = 48 * 1024 * 1024
FF_CHUNK = 256
MLA_BLOCK = 512
DIL_RES_PER_STEP = 4
PROJ_CHUNK = 768


def _cparams(sem):
    return pltpu.CompilerParams(dimension_semantics=sem, vmem_limit_bytes=V7X_VMEM_LIMIT)


def _rms(x, g):
    ms = jnp.mean(x * x, axis=-1, keepdims=True)
    return x * lax.rsqrt(ms + EPS) * g


def _sigmoid(x):
    return 0.5 * jnp.tanh(0.5 * x) + 0.5


def _dot(a, b):
    return jnp.dot(a, b, preferred_element_type=F32)


def _dot_t(a, b):
    return lax.dot_general(a, b, (((1,), (1,)), ((), ())), preferred_element_type=F32)


def _resident(shape):
    nd = len(shape)
    return pl.BlockSpec(shape, lambda *_: (0,) * nd, pipeline_mode=pl.Buffered(1))


def _ffn_body(h_ref, g_ref, wg_ref, wu_ref, wd_ref, fg_ref, o_ref, act_ref, *, final):
    x = h_ref[...]
    u = _rms(x, g_ref[...]).astype(BF)
    d_ff = wg_ref.shape[1]
    for c in range(d_ff // FF_CHUNK):
        sl = slice(c * FF_CHUNK, (c + 1) * FF_CHUNK)
        a = _dot(u, wg_ref[:, sl])
        b = _dot(u, wu_ref[:, sl])
        act_ref[:, sl] = (a * jax.nn.sigmoid(a) * b).astype(BF)
    y = x + FFN_RESIDUAL * _dot(act_ref[...], wd_ref[...])
    if final:
        y = _rms(y, fg_ref[...])
    o_ref[...] = y


def _ffn(h, g, wg, wu, wd, fg, final, tm=512):
    t, d = h.shape
    d_ff = wg.shape[1]
    tm = min(tm, t)
    return pl.pallas_call(
        functools.partial(_ffn_body, final=final),
        grid=(t // tm,),
        in_specs=[
            pl.BlockSpec((tm, d), lambda i: (i, 0)),
            _resident((1, d)),
            _resident((d, d_ff)),
            _resident((d, d_ff)),
            _resident((d_ff, d)),
            _resident((1, d)),
        ],
        out_specs=pl.BlockSpec((tm, d), lambda i: (i, 0)),
        out_shape=jax.ShapeDtypeStruct((t, d), F32),
        scratch_shapes=[pltpu.VMEM((tm, d_ff), BF)],
        compiler_params=_cparams(("parallel",)),
        name="ffn",
    )(h, g, wg, wu, wd, fg)


def _proj_a_body(h_ref, g_ref, w_ref, cos_ref, sin_ref, qg_ref, kvg_ref, wq_ref, wkv_ref,
                 sb_ref, qp_ref, kp_ref, vt_ref, *, scale):
    u = _rms(h_ref[...], g_ref[...]).astype(BF)
    for c in range(COL_CQ // PROJ_CHUNK):
        sl = slice(c * PROJ_CHUNK, (c + 1) * PROJ_CHUNK)
        sb_ref[:, sl] = _dot(u, w_ref[:, sl]).astype(sb_ref.dtype)
    lat = _dot(u, w_ref[:, COL_CQ:])
    c_kv0 = COL_CKV - COL_CQ
    kr0 = COL_KR - COL_CQ
    nq = _rms(lat[:, :c_kv0], qg_ref[...]).astype(BF)
    q = _dot(nq, wq_ref[...])
    nkv = _rms(lat[:, c_kv0:kr0], kvg_ref[...]).astype(BF)
    kv = _dot(nkv, wkv_ref[...])
    cos = cos_ref[...]
    sin = sin_ref[...]
    k_pe = (lat[:, kr0:kr0 + LANES] * cos + lat[:, kr0 + LANES:] * sin).astype(BF)
    width = q.shape[1] // 3
    for p in range(width // LANES):
        lo, hi = p * LANES, (p + 1) * LANES
        qp_ref[:, 2 * lo:2 * lo + LANES] = (q[:, lo:hi] * scale).astype(BF)
        q_pe = q[:, width + lo:width + hi] * cos + q[:, 2 * width + lo:2 * width + hi] * sin
        qp_ref[:, 2 * lo + LANES:2 * hi] = (q_pe * scale).astype(BF)
        kp_ref[:, 2 * lo:2 * lo + LANES] = kv[:, lo:hi].astype(BF)
        kp_ref[:, 2 * lo + LANES:2 * hi] = k_pe
    vt_ref[0] = kv[:, width:].T.astype(BF)


def _proj_a(h, g, w, cos, sin, qg, kvg, wq, wkv, seq, tm):
    t, d = h.shape
    width = MLA_HEADS * MLA_NOPE
    scale = (MLA_NOPE + MLA_ROPE) ** -0.5 * LOG2E
    per_seq = seq // tm
    params = (g, w)
    tables = [pl.BlockSpec((tm, LANES), lambda i: (i % per_seq, 0))] * 2
    mla_params = (qg, kvg, wq, wkv)
    row = lambda n: pl.BlockSpec((tm, n), lambda i: (i, 0))
    return pl.pallas_call(
        functools.partial(_proj_a_body, scale=scale),
        grid=(t // tm,),
        in_specs=[row(d)] + [_resident(p.shape) for p in params] + tables + [_resident(p.shape) for p in mla_params],
        out_specs=[row(COL_CQ), row(2 * width), row(2 * width),
                   pl.BlockSpec((1, width, tm), lambda i: (i, 0, 0))],
        out_shape=[jax.ShapeDtypeStruct((t, COL_CQ), BF), jax.ShapeDtypeStruct((t, 2 * width), BF),
                   jax.ShapeDtypeStruct((t, 2 * width), BF), jax.ShapeDtypeStruct((t // tm, width, tm), BF)],
        compiler_params=_cparams(("parallel",)),
        name="proj_a",
    )(h, g, w, cos, sin, qg, kvg, wq, wkv)


def _dil_proj_body(h_ref, g_ref, w_ref, o_ref, *stage, r):
    rows = h_ref.shape[1] // r
    n = w_ref.shape[1]
    u = _rms(h_ref[0], g_ref[...])
    if r == 1:
        ub = u.astype(BF)
        for c in range(n // PROJ_CHUNK):
            sl = slice(c * PROJ_CHUNK, (c + 1) * PROJ_CHUNK)
            o_ref[0, :, sl] = _dot(ub, w_ref[:, sl]).astype(o_ref.dtype)
        return
    (stage,) = stage
    chunks = u.shape[1] // LANES
    for c in range(chunks):
        stage[c] = u[:, c * LANES:(c + 1) * LANES]
    for res in range(r):
        ur = jnp.concatenate([stage[c, pl.ds(res, rows, stride=r), :] for c in range(chunks)], axis=1)
        o_ref[0, :, res * n:(res + 1) * n] = _dot(ur.astype(BF), w_ref[...]).astype(o_ref.dtype)


def _dil_proj(h3, g, w, r):
    b, s, d = h3.shape
    n = w.shape[1]
    tm = min(s, max(1024, BAND * r) if r > 1 else 512)
    return pl.pallas_call(
        functools.partial(_dil_proj_body, r=r),
        grid=(b, s // tm),
        in_specs=[
            pl.BlockSpec((1, tm, d), lambda bi, i: (bi, i, 0)),
            pl.BlockSpec((1, d), lambda bi, i: (0, 0)),
            pl.BlockSpec((d, n), lambda bi, i: (0, 0)),
        ],
        out_specs=pl.BlockSpec((1, tm // r, r * n), lambda bi, i: (bi, i, 0)),
        out_shape=jax.ShapeDtypeStruct((b, s // r, r * n), BF),
        scratch_shapes=[pltpu.VMEM((d // LANES, tm, LANES), F32)] if r > 1 else [],
        compiler_params=_cparams(("parallel", "parallel")),
        name=f"dil_proj_r{r}",
    )(h3, g, w)


def _sb_body(q_ref, k_ref, v_ref, o_ref, *, tq, pairs):
    i = pl.program_id(2)
    pw = 2 * HEAD_DIM
    lane = lax.broadcasted_iota(jnp.int32, (1, pw), 1)
    first = lane < HEAD_DIM
    q_heads = []
    for p in range(pairs):
        q = q_ref[0, :, p * pw:(p + 1) * pw]
        zero = jnp.zeros_like(q)
        q_heads += [jnp.where(first, q, zero), jnp.where(first, zero, q)]
    row = lax.broadcasted_iota(jnp.int32, (tq, tq), 0)
    col = lax.broadcasted_iota(jnp.int32, (tq, tq), 1)
    strict = col < row
    later = jnp.where(row > col, 1.0, 0.0).astype(BF)

    def block(j, accs, rs, diag):
        start = pl.multiple_of(j * tq, tq)
        heads = range(2 * pairs)
        ks = [k_ref[0, pl.ds(start, tq), p * pw:(p + 1) * pw] for p in range(pairs)]
        vs = [v_ref[0, pl.ds(start, tq), p * pw:(p + 1) * pw] for p in range(pairs)]
        zs = [_dot_t(q_heads[h], ks[h // 2]) for h in heads]
        log_betas, log_keeps = [], []
        for z in zs:
            sp = jnp.log(1.0 + jnp.exp2(-jnp.abs(z))) * LOG2E
            log_beta = jnp.minimum(z, 0.0) - sp
            log_keep = log_beta - z
            if diag:
                log_keep = jnp.where(strict, log_keep, 0.0)
            log_betas.append(log_beta)
            log_keeps.append(log_keep)
        afters = [_dot(log_keeps[h].astype(BF), later) for h in heads]
        ws = []
        for h in heads:
            w = jnp.exp2(log_betas[h] + afters[h] + rs[h])
            if diag:
                w = jnp.where(strict, w, 0.0)
            ws.append(w.astype(BF))
        outs = [_dot(ws[h], vs[h // 2]) for h in heads]
        new_rs = [rs[h] + jnp.sum(log_keeps[h], axis=-1, keepdims=True) for h in heads]
        new_accs = [accs[p] + jnp.where(first, outs[2 * p], outs[2 * p + 1]) for p in range(pairs)]
        return tuple(new_accs), tuple(new_rs)

    def remaining(rs):
        return jnp.max(functools.reduce(jnp.maximum, rs))

    def cond(c):
        return (c[0] < i) & (c[1] > SB_DEAD_LOG2)

    def body(c):
        accs, rs = block(i - 1 - c[0], c[2], c[3], False)
        return c[0] + 1, remaining(rs), accs, rs

    accs = tuple(jnp.zeros((tq, pw), F32) for _ in range(pairs))
    rs = tuple(jnp.zeros((tq, 1), F32) for _ in range(2 * pairs))
    accs, rs = block(i, accs, rs, True)
    out = lax.while_loop(cond, body, (jnp.int32(0), remaining(rs), accs, rs))
    for p in range(pairs):
        o_ref[0, :, p * pw:(p + 1) * pw] = out[2][p].astype(o_ref.dtype)


def _sb_attention(a3, tq=256, pairs=4):
    b, s, _ = a3.shape
    tq = min(tq, s)
    w = 2 * HEAD_DIM * pairs
    groups = SB_WIDTH // w
    return pl.pallas_call(
        functools.partial(_sb_body, tq=tq, pairs=pairs),
        grid=(b, groups, s // tq),
        in_specs=[
            pl.BlockSpec((1, tq, w), lambda bi, p, i: (bi, i, p)),
            pl.BlockSpec((1, s, w), lambda bi, p, i: (bi, 0, groups + p)),
            pl.BlockSpec((1, s, w), lambda bi, p, i: (bi, 0, 2 * groups + p)),
        ],
        out_specs=pl.BlockSpec((1, tq, w), lambda bi, p, i: (bi, i, p)),
        out_shape=jax.ShapeDtypeStruct((b, s, SB_WIDTH), BF),
        compiler_params=_cparams(("parallel", "parallel", "arbitrary")),
        name="sb_attn",
    )(a3, a3, a3)


def _mla_body(q_ref, k_ref, v_ref, o_ref, *, tq, pairs):
    i = pl.program_id(2)
    lane = lax.broadcasted_iota(jnp.int32, (1, MLA_PAIR), 1)
    sel0 = (lane < 64) | ((lane >= 128) & (lane < 160))
    sel1 = ((lane >= 64) & (lane < 128)) | ((lane >= 160) & (lane < 192))
    q_heads = []
    for p in range(pairs):
        q = q_ref[0, :, p * MLA_PAIR:(p + 1) * MLA_PAIR]
        zero = jnp.zeros_like(q)
        q_heads += [jnp.where(sel0, q, zero), jnp.where(sel1, q, zero)]
    key = lax.broadcasted_iota(jnp.int32, (tq, tq), 0)
    qry = lax.broadcasted_iota(jnp.int32, (tq, tq), 1)
    causal = key <= qry
    first = lax.broadcasted_iota(jnp.int32, (128, 1), 0) < 64

    def block(j, carry, diag, nblk=1):
        start = pl.multiple_of(j * tq, tq)
        heads = range(2 * pairs)
        ks = [k_ref[0, pl.ds(start, nblk * tq), p * MLA_PAIR:(p + 1) * MLA_PAIR] for p in range(pairs)]
        v_heads = []
        for p in range(pairs):
            vt = jnp.concatenate([v_ref[0, j + n, p * 128:(p + 1) * 128, :] for n in range(nblk)], axis=1)
            ones = jnp.ones_like(vt)
            v_heads += [jnp.where(first, vt, ones), jnp.where(first, ones, vt)]
        ss = [_dot_t(ks[h // 2], q_heads[h]) for h in heads]
        if diag:
            ss = [jnp.where(causal, s, NEG) for s in ss]
        m_news = [jnp.maximum(carry[h][0], jnp.max(ss[h], axis=0, keepdims=True)) for h in heads]
        prs = [jnp.exp2(ss[h] - m_news[h]).astype(BF) for h in heads]
        pvs = [_dot(v_heads[h], prs[h]) for h in heads]
        return tuple((m_news[h], jnp.exp2(carry[h][0] - m_news[h]) * carry[h][1] + pvs[h]) for h in heads)

    one = (jnp.full((1, tq), NEG, F32), jnp.zeros((128, tq), F32))
    carry = lax.fori_loop(0, i // 2, lambda t, c: block(2 * t, c, False, 2), (one,) * (2 * pairs))
    carry = lax.cond(i % 2 == 1, lambda c: block(i - 1, c, False), lambda c: c, carry)
    carry = block(i, carry, True)
    for p in range(pairs):
        a0, a1 = carry[2 * p][1], carry[2 * p + 1][1]
        lsw = jnp.where(first, a1, a0)
        l = jnp.concatenate([lsw[64:], lsw[:64]], axis=0)
        out_t = jnp.where(first, a0, a1) / l
        o_ref[0, :, p * 128:(p + 1) * 128] = out_t.T.astype(o_ref.dtype)


def _mla_attention(qp, kp, vt, pairs=2):
    b, s, _ = qp.shape
    nblk, width, tq = vt.shape[1:]
    groups = MLA_HEADS // (2 * pairs)
    return pl.pallas_call(
        functools.partial(_mla_body, tq=tq, pairs=pairs),
        grid=(b, groups, s // tq),
        in_specs=[
            pl.BlockSpec((1, tq, MLA_PAIR * pairs), lambda bi, p, i: (bi, i, p)),
            pl.BlockSpec((1, s, MLA_PAIR * pairs), lambda bi, p, i: (bi, 0, p)),
            pl.BlockSpec((1, nblk, 128 * pairs, tq), lambda bi, p, i: (bi, 0, p, 0)),
        ],
        out_specs=pl.BlockSpec((1, tq, 128 * pairs), lambda bi, p, i: (bi, i, p)),
        out_shape=jax.ShapeDtypeStruct((b, s, MLA_HEADS * MLA_V), BF),
        compiler_params=_cparams(("parallel", "parallel", "arbitrary")),
        name="mla_attn",
    )(qp, kp, vt)


def _dil_body(cur_ref, prev_ref, bias_ref, o_ref, lse_ref, *, tl, nres):
    j = pl.program_id(2)
    lane = lax.broadcasted_iota(jnp.int32, (1, 2 * HEAD_DIM), 1)
    first = lane < HEAD_DIM
    kj = lax.broadcasted_iota(jnp.int32, (BAND, 2 * BAND), 1)
    has_prev = (kj >= BAND) | (j > 0)
    pw = 2 * HEAD_DIM
    pairs = DIL_WIDTH // pw
    heads = range(2 * pairs)
    for rr, sb in [(rr, sb) for rr in range(nres) for sb in range(tl // BAND)]:
        rows = slice(sb * BAND, (sb + 1) * BAND)
        prev = slice((sb - 1) * BAND, sb * BAND)
        q_cols, k_cols, v_cols = (slice((3 * rr + c) * DIL_WIDTH, (3 * rr + c + 1) * DIL_WIDTH) for c in range(3))
        out_base = rr * DIL_WIDTH
        k_prev = prev_ref[0, :, k_cols] if sb == 0 else cur_ref[0, prev, k_cols]
        v_prev = prev_ref[0, :, v_cols] if sb == 0 else cur_ref[0, prev, v_cols]
        kcat = jnp.concatenate([k_prev, cur_ref[0, rows, k_cols]], axis=0)
        vcat = jnp.concatenate([v_prev, cur_ref[0, rows, v_cols]], axis=0)
        qs = cur_ref[0, rows, q_cols]
        q_heads, v_heads = [], []
        for p in range(pairs):
            qb = qs[:, p * pw:(p + 1) * pw]
            vb = vcat[:, p * pw:(p + 1) * pw]
            zero, ones = jnp.zeros_like(qb), jnp.ones_like(vb)
            q_heads += [jnp.where(first, qb, zero), jnp.where(first, zero, qb)]
            v_heads += [jnp.where(first, vb, ones), jnp.where(first, ones, vb)]
        ss = [_dot_t(q_heads[h], kcat[:, (h // 2) * pw:(h // 2 + 1) * pw]) + bias_ref[h] for h in heads]
        if sb == 0:
            ss = [jnp.where(has_prev, s, NEG) for s in ss]
        ms = [jnp.max(s, axis=-1, keepdims=True) for s in ss]
        prs = [jnp.exp2(ss[h] - ms[h]).astype(BF) for h in heads]
        pvs = [_dot(prs[h], v_heads[h]) for h in heads]
        for p in range(pairs):
            a0, a1 = pvs[2 * p], pvs[2 * p + 1]
            cols = slice(out_base + p * pw, out_base + (p + 1) * pw)
            l = pltpu.roll(jnp.where(first, a1, a0), HEAD_DIM, 1)
            o_ref[0, rows, cols] = (jnp.where(first, a0, a1) / l).astype(o_ref.dtype)
            lse_ref[0, rows, cols] = jnp.where(first, ms[2 * p], ms[2 * p + 1]) * LN2 + jnp.log(l)


def _dil_attention(view, bias, dilation, tl=512):
    r = dilation
    batch, length, _ = view.shape
    tl = min(tl, length)
    sub = tl // BAND
    nres = min(r, DIL_RES_PER_STEP)
    wide = nres * 3 * DIL_WIDTH
    cur = pl.BlockSpec((1, tl, wide), lambda bi, rg, j: (bi, j, rg))
    prev = pl.BlockSpec((1, BAND, wide), lambda bi, rg, j: (bi, jnp.maximum(j * sub - 1, 0), rg))
    out_spec = pl.BlockSpec((1, tl, nres * DIL_WIDTH), lambda bi, rg, j: (bi, j, rg))
    return pl.pallas_call(
        functools.partial(_dil_body, tl=tl, nres=nres),
        grid=(batch, r // nres, length // tl),
        in_specs=[cur, prev, pl.BlockSpec(bias.shape, lambda bi, rg, j: (0, 0, 0))],
        out_specs=[out_spec, out_spec],
        out_shape=[jax.ShapeDtypeStruct((batch, length, r * DIL_WIDTH), BF),
                   jax.ShapeDtypeStruct((batch, length, r * DIL_WIDTH), F32)],
        compiler_params=_cparams(("parallel", "parallel", "arbitrary")),
        name=f"dil_attn_r{r}",
    )(view, view, bias)


def _merge_body(h_ref, oa_ref, ob_ref, oc0_ref, oc1_ref, oc2_ref, l0_ref, l1_ref, l2_ref,
                gn_ref, wg_ref, gbias_ref, wa_ref, wb_ref, wc_ref, wo_ref, o_ref, *scratch):
    tm, d = h_ref.shape[1:]
    scratch = list(scratch)
    h = h_ref[0]
    u = _rms(h, gn_ref[...]).astype(BF)

    def gate(c):
        return _sigmoid(_dot(u, wg_ref[:, c * d:(c + 1) * d]) + gbias_ref[c:c + 1, :])

    def token_major(ref, r):
        if r == 1:
            return ref[0].astype(F32)
        buf = scratch.pop(0)
        chunks = DIL_WIDTH // LANES
        for res in range(r):
            for c in range(chunks):
                lo = res * DIL_WIDTH + c * LANES
                buf[c, pl.ds(res, tm // r, stride=r), :] = ref[0, :, lo:lo + LANES].astype(F32)
        return jnp.concatenate([buf[c] for c in range(chunks)], axis=1)

    dils = [dil for _, dil in DIL_GROUPS]
    merged = gate(0) * _dot(oa_ref[0], wa_ref[...]) + gate(1) * _dot(ob_ref[0], wb_ref[...])
    gate_c = gate(2)
    l0, l1, l2 = (token_major(ref, r) for ref, r in zip((l0_ref, l1_ref, l2_ref), dils))
    mx = jnp.maximum(jnp.maximum(l0, l1), l2)
    e0, e1, e2 = jnp.exp(l0 - mx), jnp.exp(l1 - mx), jnp.exp(l2 - mx)
    o0, o1, o2 = (token_major(ref, r) for ref, r in zip((oc0_ref, oc1_ref, oc2_ref), dils))
    oc = (e0 * o0 + e1 * o1 + e2 * o2) / (e0 + e1 + e2)
    merged = merged + gate_c * _dot(oc.astype(BF), wc_ref[...])
    o_ref[0] = h + _dot(merged.astype(BF), wo_ref[...])


def _merge(h3, oa, ob, ocs, lses, g_mix, w_gate, gbias, wa, wb, wc, wo, tm=512):
    b, s, d = h3.shape
    tm = min(tm, s)
    row = lambda n: pl.BlockSpec((1, tm, n), lambda bi, i: (bi, i, 0))
    views = [pl.BlockSpec((1, tm // r, r * DIL_WIDTH), lambda bi, i: (bi, i, 0)) for _, r in DIL_GROUPS]
    n_buf = 2 * sum(1 for _, r in DIL_GROUPS if r > 1)
    params = (g_mix, w_gate, gbias, wa, wb, wc, wo)
    return pl.pallas_call(
        _merge_body,
        grid=(b, s // tm),
        in_specs=[row(d), row(SB_WIDTH), row(SB_WIDTH)] + views + views + [_resident(p.shape) for p in params],
        out_specs=row(d),
        out_shape=jax.ShapeDtypeStruct((b, s, d), F32),
        scratch_shapes=[pltpu.VMEM((DIL_WIDTH // LANES, tm, LANES), F32)] * n_buf,
        compiler_params=_cparams(("parallel", "parallel")),
        name="merge",
    )(h3, oa, ob, *ocs, *lses, *params)


def _memkv_body(m_ref, g_ref, w_ref, o_ref):
    u = _rms(m_ref[0], g_ref[...]).astype(BF)
    o_ref[0] = _dot(u, w_ref[...]).astype(o_ref.dtype)


def _mem_kv(mem, g, w):
    b, m, d = mem.shape
    n = w.shape[1]
    return pl.pallas_call(
        _memkv_body,
        grid=(b,),
        in_specs=[pl.BlockSpec((1, m, d), lambda i: (i, 0, 0)),
                  pl.BlockSpec((1, d), lambda i: (0, 0)),
                  pl.BlockSpec((d, n), lambda i: (0, 0))],
        out_specs=pl.BlockSpec((1, m, n), lambda i: (i, 0, 0)),
        out_shape=jax.ShapeDtypeStruct((b, m, n), BF),
        compiler_params=_cparams(("parallel",)),
        name="mem_kv",
    )(mem, g, w)


def _xattn_body(h_ref, g_ref, kv_ref, wq_ref, wo_ref, o_ref):
    x = h_ref[0]
    u = _rms(x, g_ref[...]).astype(BF)
    q = _dot(u, wq_ref[...]).astype(BF)
    kv = kv_ref[0]
    width = X_HEADS * X_HEAD_DIM
    scale = X_HEAD_DIM ** -0.5
    outs = []
    for h in range(X_HEADS):
        cols = slice(h * X_HEAD_DIM, (h + 1) * X_HEAD_DIM)
        s = _dot_t(q[:, cols], kv[:, cols]) * scale
        m = jnp.max(s, axis=-1, keepdims=True)
        e = jnp.exp(s - m)
        p = e / jnp.sum(e, axis=-1, keepdims=True)
        outs.append(_dot(p.astype(BF), kv[:, width + h * X_HEAD_DIM:width + (h + 1) * X_HEAD_DIM]))
    o = jnp.concatenate(outs, axis=-1).astype(BF)
    o_ref[0] = x + _dot(o, wo_ref[...])


def _xattn(h3, g, kv, wq, wo, tm=512):
    b, s, d = h3.shape
    tm = min(tm, s)
    m = kv.shape[1]
    return pl.pallas_call(
        _xattn_body,
        grid=(b, s // tm),
        in_specs=[pl.BlockSpec((1, tm, d), lambda bi, i: (bi, i, 0)),
                  pl.BlockSpec((1, d), lambda bi, i: (0, 0)),
                  pl.BlockSpec((1, m, kv.shape[2]), lambda bi, i: (bi, 0, 0)),
                  pl.BlockSpec(wq.shape, lambda bi, i: (0, 0)),
                  pl.BlockSpec(wo.shape, lambda bi, i: (0, 0))],
        out_specs=pl.BlockSpec((1, tm, d), lambda bi, i: (bi, i, 0)),
        out_shape=jax.ShapeDtypeStruct((b, s, d), F32),
        compiler_params=_cparams(("parallel", "parallel")),
        name="xattn",
    )(h3, g, kv, wq, wo)


def _rel_bucket(dist):
    exact = REL_BUCKETS // 2
    d = jnp.maximum(dist, exact).astype(F32)
    large = exact + (jnp.log(d / exact) / math.log(REL_MAX_DIST / exact)
                     * (REL_BUCKETS - exact)).astype(jnp.int32)
    return jnp.where(dist < exact, dist, jnp.minimum(large, REL_BUCKETS - 1))


def _band_bias(rel_bias, group, dilation):
    heads = DIL_WIDTH // HEAD_DIM
    qi = jnp.arange(BAND)
    kj = jnp.arange(2 * BAND)
    steps = (qi[:, None] + BAND) - kj[None, :]
    table = rel_bias[:, group * heads:(group + 1) * heads].astype(F32)
    bucket = _rel_bucket(jnp.clip(steps, 0, BAND) * dilation)
    onehot = (bucket[None] == jnp.arange(REL_BUCKETS)[:, None, None]).astype(F32)
    bias = jnp.einsum('bh,bqk->hqk', table * LOG2E, onehot, precision=lax.Precision.HIGHEST)
    return jnp.where((steps >= 0) & (steps <= BAND), bias, NEG)


def _swap_halves(w, width):
    k, n = w.shape
    w = w.reshape(k, n // width, 2, width // 2)
    return w[:, :, ::-1, :].reshape(k, n)


def _pack_w_in(w_in):
    d = w_in.shape[0]
    qscale = HEAD_DIM ** -0.5 * LOG2E
    sb = jnp.concatenate([w_in[:, :SB_WIDTH] * qscale, w_in[:, SB_WIDTH:COL_CQ]], axis=1)
    cq = w_in[:, COL_CQ:COL_CKV]
    ckv = w_in[:, COL_CKV:COL_KR]
    kr = w_in[:, COL_KR:COL_DIL]
    zeros = jnp.zeros((d, 64), w_in.dtype)
    kr_sw = _swap_halves(kr, MLA_ROPE)
    seg_a = jnp.concatenate([sb, cq, ckv, kr, kr, zeros, kr_sw, kr_sw, zeros], axis=1)
    group = 3 * DIL_WIDTH
    seg_d = []
    for g in range(len(DIL_GROUPS)):
        lo = COL_DIL + g * group
        seg_d.append(jnp.concatenate([w_in[:, lo:lo + DIL_WIDTH] * qscale, w_in[:, lo + DIL_WIDTH:lo + group]],
                                     axis=1).astype(BF))
    seg_g = w_in[:, COL_DIL + len(DIL_GROUPS) * group:]
    return seg_a.astype(BF), seg_d, seg_g.astype(BF)


def _pack_w_uq(w_uq):
    k = w_uq.shape[0]
    w = w_uq.reshape(k, MLA_HEADS, MLA_NOPE + MLA_ROPE)
    nope = w[:, :, :MLA_NOPE].reshape(k, MLA_HEADS * MLA_NOPE)
    rope = w[:, :, MLA_NOPE:]

    def pair_layout(rp):
        rp = rp.reshape(k, MLA_HEADS // 2, 2 * MLA_ROPE)
        pad = jnp.zeros((k, MLA_HEADS // 2, 128 - 2 * MLA_ROPE), rp.dtype)
        return jnp.concatenate([rp, pad], axis=-1).reshape(k, (MLA_HEADS // 2) * 128)

    rope_sw = rope.reshape(k, MLA_HEADS, 2, MLA_ROPE // 2)[:, :, ::-1, :].reshape(k, MLA_HEADS, MLA_ROPE)
    return jnp.concatenate([nope, pair_layout(rope), pair_layout(rope_sw)], axis=1).astype(BF)


def _pack_w_ukv(w_ukv):
    k = w_ukv.shape[0]
    w = w_ukv.reshape(k, MLA_HEADS, MLA_NOPE + MLA_V)
    return jnp.concatenate([w[:, :, :MLA_NOPE].reshape(k, -1), w[:, :, MLA_NOPE:].reshape(k, -1)],
                           axis=1).astype(BF)


def _rope_tables(seq):
    half = MLA_ROPE // 2
    freqs = ROPE_THETA ** (-jnp.arange(half, dtype=F32) / half)
    ang = jnp.arange(seq).astype(F32)[:, None] * freqs[None, :]
    cos, sin = jnp.cos(ang), jnp.sin(ang)
    pad = jnp.zeros((seq, 128 - 2 * MLA_ROPE), F32)
    cos_t = jnp.concatenate([cos, cos, cos, cos, pad], axis=1)
    sin_t = jnp.concatenate([-sin, sin, -sin, sin, pad], axis=1)
    return cos_t, sin_t


def kernel(x, mem, ffn1_norm, ffn1_w_gate, ffn1_w_up, ffn1_w_down, mix_norm, w_in, gate_bias, mla_q_norm, mla_w_uq, mla_kv_norm, mla_w_ukv, w_branch_a, w_branch_b, w_branch_c, w_mix_out, rel_bias, xattn_norm, mem_norm, xattn_w_q, xattn_w_kv, xattn_w_o, ffn2_norm, ffn2_w_gate, ffn2_w_up, ffn2_w_down, final_norm):
    b, s, d = x.shape
    t = b * s
    depth = w_in.shape[0]
    cos_t, sin_t = _rope_tables(s)
    biases = [_band_bias(rel_bias, g, dil) for g, (_, dil) in enumerate(DIL_GROUPS)]
    row = lambda v: v.reshape(1, -1)
    bf = lambda w: w.astype(BF)
    fg = row(final_norm)

    h = x.reshape(t, d)
    for l in range(depth):
        h = _ffn(h, row(ffn1_norm[l]), bf(ffn1_w_gate[l]), bf(ffn1_w_up[l]), bf(ffn1_w_down[l]), fg, False)

        w_a, w_d, w_g = _pack_w_in(w_in[l])
        g_mix = row(mix_norm[l])
        h3 = h.reshape(b, s, d)
        tm_a = min(MLA_BLOCK, s)
        sb, qp, kp, vt = _proj_a(h, g_mix, w_a, cos_t, sin_t, row(mla_q_norm[l]), row(mla_kv_norm[l]),
                                 _pack_w_uq(mla_w_uq[l]), _pack_w_ukv(mla_w_ukv[l]), s, tm_a)
        o_a = _sb_attention(sb.reshape(b, s, sb.shape[1]))
        o_b = _mla_attention(qp.reshape(b, s, qp.shape[1]), kp.reshape(b, s, kp.shape[1]),
                             vt.reshape(b, s // tm_a, vt.shape[1], tm_a))
        ocs, lses = [], []
        for g, (_, dil) in enumerate(DIL_GROUPS):
            o, lse = _dil_attention(_dil_proj(h3, g_mix, w_d[g], dil), biases[g], dil)
            ocs.append(o)
            lses.append(lse)
        h3 = _merge(h3, o_a, o_b, ocs, lses, g_mix, w_g, gate_bias[l],
                    bf(w_branch_a[l]), bf(w_branch_b[l]), bf(w_branch_c[l]), bf(w_mix_out[l]))

        kv = _mem_kv(mem, row(mem_norm[l]), bf(xattn_w_kv[l]))
        h = _xattn(h3, row(xattn_norm[l]), kv, bf(xattn_w_q[l]), bf(xattn_w_o[l])).reshape(t, d)

        h = _ffn(h, row(ffn2_norm[l]), bf(ffn2_w_gate[l]), bf(ffn2_w_up[l]), bf(ffn2_w_down[l]), fg,
                 l == depth - 1)
    return h.reshape(b, s, d)
```

```python
import functools
import math

import jax
import jax.numpy as jnp
import numpy as np
from jax import lax
from jax.experimental import pallas as pl
from jax.experimental.pallas import tpu as pltpu

BF = jnp.bfloat16
F32 = jnp.float32

EPS = 1e-6
NEG = -1e30
FFN_RESIDUAL = 0.5

SB_WIDTH = 512
MLA_HEADS = 8
MLA_Q_RANK = 384
MLA_KV_RANK = 128
MLA_NOPE = 64
MLA_ROPE = 32
MLA_V = 64
ROPE_THETA = 10000.0
DIL_GROUPS = ((128, 1), (512, 4), (2048, 16))
DIL_WIDTH = 512
REL_BUCKETS = 32
REL_MAX_DIST = 2048
X_HEADS = 4
X_HEAD_DIM = 128
HEAD_DIM = 64
BAND = 128
LANES = 128
MLA_PAIR = 2 * LANES

COL_CQ = 3 * SB_WIDTH
COL_CKV = COL_CQ + MLA_Q_RANK
COL_KR = COL_CKV + MLA_KV_RANK
COL_DIL = COL_KR + MLA_ROPE
SB_DEAD_LOG2 = -150.0
LOG2E = math.log2(math.e)
LN2 = math.log(2.0)

V7X_VMEM_LIMIT = 48 * 1024 * 1024
FF_CHUNK = 256
MLA_BLOCK = 512
DIL_RES_PER_STEP = 4
QSCALE = HEAD_DIM ** -0.5 * LOG2E


def _cparams(sem):
    return pltpu.CompilerParams(dimension_semantics=sem, vmem_limit_bytes=V7X_VMEM_LIMIT)


def _rms(x, g):
    ms = jnp.mean(x * x, axis=-1, keepdims=True)
    return x * lax.rsqrt(ms + EPS) * g


def _sigmoid(x):
    return 0.5 * jnp.tanh(0.5 * x) + 0.5


def _dot(a, b):
    return jnp.dot(a, b, preferred_element_type=F32)


def _dot_t(a, b):
    return lax.dot_general(a, b, (((1,), (1,)), ((), ())), preferred_element_type=F32)


def _resident(shape):
    nd = len(shape)
    return pl.BlockSpec(shape, lambda *_: (0,) * nd, pipeline_mode=pl.Buffered(1))


def _ffn_body(h_ref, g_ref, wg_ref, wu_ref, wd_ref, fg_ref, o_ref, act_ref, *, final):
    x = h_ref[...]
    u = _rms(x, g_ref[...]).astype(BF)
    d_ff = wg_ref.shape[1]
    for c in range(d_ff // FF_CHUNK):
        sl = slice(c * FF_CHUNK, (c + 1) * FF_CHUNK)
        a = _dot(u, wg_ref[:, sl])
        b = _dot(u, wu_ref[:, sl])
        act_ref[:, sl] = (a * jax.nn.sigmoid(a) * b).astype(BF)
    y = x + FFN_RESIDUAL * _dot(act_ref[...], wd_ref[...])
    if final:
        y = _rms(y, fg_ref[...])
    o_ref[...] = y


def _ffn(h, g, wg, wu, wd, fg, final, tm=512):
    t, d = h.shape
    d_ff = wg.shape[1]
    tm = min(tm, t)
    return pl.pallas_call(
        functools.partial(_ffn_body, final=final),
        grid=(t // tm,),
        in_specs=[
            pl.BlockSpec((tm, d), lambda i: (i, 0)),
            _resident((1, d)),
            _resident((d, d_ff)),
            _resident((d, d_ff)),
            _resident((d_ff, d)),
            _resident((1, d)),
        ],
        out_specs=pl.BlockSpec((tm, d), lambda i: (i, 0)),
        out_shape=jax.ShapeDtypeStruct((t, d), F32),
        scratch_shapes=[pltpu.VMEM((tm, d_ff), BF)],
        compiler_params=_cparams(("parallel",)),
        name="ffn",
    )(h, g, wg, wu, wd, fg)


def _proj_a_body(h_ref, g_ref, w_ref, wr_ref, cos_ref, sin_ref, qg_ref, kvg_ref, wq_ref, wkv_ref,
                 sb_ref, qp_ref, kp_ref, vt_ref, *, scale):
    u = _rms(h_ref[...], g_ref[...]).astype(BF)
    for c, factor in enumerate((QSCALE, None, None)):
        sl = slice(c * SB_WIDTH, (c + 1) * SB_WIDTH)
        y = _dot(u, w_ref[:, sl])
        sb_ref[:, sl] = (y if factor is None else y * factor).astype(sb_ref.dtype)
    lat = _dot(u, w_ref[:, COL_CQ:COL_KR])
    kr = _dot(u, wr_ref[...])
    c_kv0 = COL_CKV - COL_CQ
    nq = _rms(lat[:, :c_kv0], qg_ref[...]).astype(BF)
    q = _dot(nq, wq_ref[...])
    nkv = _rms(lat[:, c_kv0:], kvg_ref[...]).astype(BF)
    kv = _dot(nkv, wkv_ref[...])
    cos = cos_ref[...]
    sin = sin_ref[...]
    k_pe = (kr[:, :LANES] * cos + kr[:, LANES:] * sin).astype(BF)
    width = q.shape[1] // 3
    for p in range(width // LANES):
        lo, hi = p * LANES, (p + 1) * LANES
        qp_ref[:, 2 * lo:2 * lo + LANES] = (q[:, lo:hi] * scale).astype(BF)
        q_pe = q[:, width + lo:width + hi] * cos + q[:, 2 * width + lo:2 * width + hi] * sin
        qp_ref[:, 2 * lo + LANES:2 * hi] = (q_pe * scale).astype(BF)
        kp_ref[:, 2 * lo:2 * lo + LANES] = kv[:, lo:hi].astype(BF)
        kp_ref[:, 2 * lo + LANES:2 * hi] = k_pe
    vt_ref[0] = kv[:, width:].T.astype(BF)


def _proj_a(h, g, w, w_rope, cos, sin, qg, kvg, wq, wkv, seq, tm):
    t, d = h.shape
    width = MLA_HEADS * MLA_NOPE
    scale = (MLA_NOPE + MLA_ROPE) ** -0.5 * LOG2E
    per_seq = seq // tm
    params = (g, w, w_rope)
    tables = [pl.BlockSpec((tm, LANES), lambda i: (i % per_seq, 0))] * 2
    mla_params = (qg, kvg, wq, wkv)
    row = lambda n: pl.BlockSpec((tm, n), lambda i: (i, 0))
    return pl.pallas_call(
        functools.partial(_proj_a_body, scale=scale),
        grid=(t // tm,),
        in_specs=[row(d)] + [_resident(p.shape) for p in params] + tables + [_resident(p.shape) for p in mla_params],
        out_specs=[row(COL_CQ), row(2 * width), row(2 * width),
                   pl.BlockSpec((1, width, tm), lambda i: (i, 0, 0))],
        out_shape=[jax.ShapeDtypeStruct((t, COL_CQ), BF), jax.ShapeDtypeStruct((t, 2 * width), BF),
                   jax.ShapeDtypeStruct((t, 2 * width), BF), jax.ShapeDtypeStruct((t // tm, width, tm), BF)],
        compiler_params=_cparams(("parallel",)),
        name="proj_a",
    )(h, g, w, w_rope, cos, sin, qg, kvg, wq, wkv)


def _dil_proj_body(h_ref, g_ref, w_ref, o_ref, *stage, r):
    rows = h_ref.shape[1] // r
    n = w_ref.shape[1]
    u = _rms(h_ref[0], g_ref[...])

    def project(ub, base):
        for c, factor in enumerate((QSCALE, None, None)):
            sl = slice(c * DIL_WIDTH, (c + 1) * DIL_WIDTH)
            y = _dot(ub, w_ref[:, sl])
            o_ref[0, :, base + c * DIL_WIDTH:base + (c + 1) * DIL_WIDTH] = (
                y if factor is None else y * factor).astype(o_ref.dtype)

    if r == 1:
        project(u.astype(BF), 0)
        return
    (stage,) = stage
    chunks = u.shape[1] // LANES
    for c in range(chunks):
        stage[c] = u[:, c * LANES:(c + 1) * LANES]
    for res in range(r):
        ur = jnp.concatenate([stage[c, pl.ds(res, rows, stride=r), :] for c in range(chunks)], axis=1)
        project(ur.astype(BF), res * n)


def _dil_proj(h3, g, w, r):
    b, s, d = h3.shape
    n = w.shape[1]
    tm = min(s, max(1024, BAND * r) if r > 1 else 512)
    return pl.pallas_call(
        functools.partial(_dil_proj_body, r=r),
        grid=(b, s // tm),
        in_specs=[
            pl.BlockSpec((1, tm, d), lambda bi, i: (bi, i, 0)),
            pl.BlockSpec((1, d), lambda bi, i: (0, 0)),
            pl.BlockSpec((d, n), lambda bi, i: (0, 0)),
        ],
        out_specs=pl.BlockSpec((1, tm // r, r * n), lambda bi, i: (bi, i, 0)),
        out_shape=jax.ShapeDtypeStruct((b, s // r, r * n), BF),
        scratch_shapes=[pltpu.VMEM((d // LANES, tm, LANES), F32)] if r > 1 else [],
        compiler_params=_cparams(("parallel", "parallel")),
        name=f"dil_proj_r{r}",
    )(h3, g, w)


def _sb_body(q_ref, k_ref, v_ref, o_ref, *, tq, pairs):
    i = pl.program_id(2)
    pw = 2 * HEAD_DIM
    lane = lax.broadcasted_iota(jnp.int32, (1, pw), 1)
    first = lane < HEAD_DIM
    q_heads = []
    for p in range(pairs):
        q = q_ref[0, :, p * pw:(p + 1) * pw]
        zero = jnp.zeros_like(q)
        q_heads += [jnp.where(first, q, zero), jnp.where(first, zero, q)]
    row = lax.broadcasted_iota(jnp.int32, (tq, tq), 0)
    col = lax.broadcasted_iota(jnp.int32, (tq, tq), 1)
    strict = col < row
    later = jnp.where(row > col, 1.0, 0.0).astype(BF)

    def block(j, accs, rs, diag):
        start = pl.multiple_of(j * tq, tq)
        heads = range(2 * pairs)
        ks = [k_ref[0, pl.ds(start, tq), p * pw:(p + 1) * pw] for p in range(pairs)]
        vs = [v_ref[0, pl.ds(start, tq), p * pw:(p + 1) * pw] for p in range(pairs)]
        zs = [_dot_t(q_heads[h], ks[h // 2]) for h in heads]
        log_betas, log_keeps = [], []
        for z in zs:
            sp = jnp.log(1.0 + jnp.exp2(-jnp.abs(z))) * LOG2E
            log_beta = jnp.minimum(z, 0.0) - sp
            log_keep = log_beta - z
            if diag:
                log_keep = jnp.where(strict, log_keep, 0.0)
            log_betas.append(log_beta)
            log_keeps.append(log_keep)
        afters = [_dot(log_keeps[h].astype(BF), later) for h in heads]
        ws = []
        for h in heads:
            w = jnp.exp2(log_betas[h] + afters[h] + rs[h])
            if diag:
                w = jnp.where(strict, w, 0.0)
            ws.append(w.astype(BF))
        outs = [_dot(ws[h], vs[h // 2]) for h in heads]
        new_rs = [rs[h] + jnp.sum(log_keeps[h], axis=-1, keepdims=True) for h in heads]
        new_accs = [accs[p] + jnp.where(first, outs[2 * p], outs[2 * p + 1]) for p in range(pairs)]
        return tuple(new_accs), tuple(new_rs)

    def remaining(rs):
        return jnp.max(functools.reduce(jnp.maximum, rs))

    def cond(c):
        return (c[0] < i) & (c[1] > SB_DEAD_LOG2)

    def body(c):
        accs, rs = block(i - 1 - c[0], c[2], c[3], False)
        return c[0] + 1, remaining(rs), accs, rs

    accs = tuple(jnp.zeros((tq, pw), F32) for _ in range(pairs))
    rs = tuple(jnp.zeros((tq, 1), F32) for _ in range(2 * pairs))
    accs, rs = block(i, accs, rs, True)
    out = lax.while_loop(cond, body, (jnp.int32(0), remaining(rs), accs, rs))
    for p in range(pairs):
        o_ref[0, :, p * pw:(p + 1) * pw] = out[2][p].astype(o_ref.dtype)


def _sb_attention(a3, tq=256, pairs=4):
    b, s, _ = a3.shape
    tq = min(tq, s)
    w = 2 * HEAD_DIM * pairs
    groups = SB_WIDTH // w
    return pl.pallas_call(
        functools.partial(_sb_body, tq=tq, pairs=pairs),
        grid=(b, groups, s // tq),
        in_specs=[
            pl.BlockSpec((1, tq, w), lambda bi, p, i: (bi, i, p)),
            pl.BlockSpec((1, s, w), lambda bi, p, i: (bi, 0, groups + p)),
            pl.BlockSpec((1, s, w), lambda bi, p, i: (bi, 0, 2 * groups + p)),
        ],
        out_specs=pl.BlockSpec((1, tq, w), lambda bi, p, i: (bi, i, p)),
        out_shape=jax.ShapeDtypeStruct((b, s, SB_WIDTH), BF),
        compiler_params=_cparams(("parallel", "parallel", "arbitrary")),
        name="sb_attn",
    )(a3, a3, a3)


def _mla_body(q_ref, k_ref, v_ref, o_ref, *, tq, pairs):
    i = pl.program_id(2)
    lane = lax.broadcasted_iota(jnp.int32, (1, MLA_PAIR), 1)
    sel0 = (lane < 64) | ((lane >= 128) & (lane < 160))
    sel1 = ((lane >= 64) & (lane < 128)) | ((lane >= 160) & (lane < 192))
    q_heads = []
    for p in range(pairs):
        q = q_ref[0, :, p * MLA_PAIR:(p + 1) * MLA_PAIR]
        zero = jnp.zeros_like(q)
        q_heads += [jnp.where(sel0, q, zero), jnp.where(sel1, q, zero)]
    key = lax.broadcasted_iota(jnp.int32, (tq, tq), 0)
    qry = lax.broadcasted_iota(jnp.int32, (tq, tq), 1)
    causal = key <= qry
    first = lax.broadcasted_iota(jnp.int32, (128, 1), 0) < 64

    def block(j, carry, diag, nblk=1):
        start = pl.multiple_of(j * tq, tq)
        heads = range(2 * pairs)
        ks = [k_ref[0, pl.ds(start, nblk * tq), p * MLA_PAIR:(p + 1) * MLA_PAIR] for p in range(pairs)]
        v_heads = []
        for p in range(pairs):
            vt = jnp.concatenate([v_ref[0, j + n, p * 128:(p + 1) * 128, :] for n in range(nblk)], axis=1)
            ones = jnp.ones_like(vt)
            v_heads += [jnp.where(first, vt, ones), jnp.where(first, ones, vt)]
        ss = [_dot_t(ks[h // 2], q_heads[h]) for h in heads]
        if diag:
            ss = [jnp.where(causal, s, NEG) for s in ss]
        m_news = [jnp.maximum(carry[h][0], jnp.max(ss[h], axis=0, keepdims=True)) for h in heads]
        prs = [jnp.exp2(ss[h] - m_news[h]).astype(BF) for h in heads]
        pvs = [_dot(v_heads[h], prs[h]) for h in heads]
        return tuple((m_news[h], jnp.exp2(carry[h][0] - m_news[h]) * carry[h][1] + pvs[h]) for h in heads)

    one = (jnp.full((1, tq), NEG, F32), jnp.zeros((128, tq), F32))
    carry = lax.fori_loop(0, i // 2, lambda t, c: block(2 * t, c, False, 2), (one,) * (2 * pairs))
    carry = lax.cond(i % 2 == 1, lambda c: block(i - 1, c, False), lambda c: c, carry)
    carry = block(i, carry, True)
    for p in range(pairs):
        a0, a1 = carry[2 * p][1], carry[2 * p + 1][1]
        lsw = jnp.where(first, a1, a0)
        l = jnp.concatenate([lsw[64:], lsw[:64]], axis=0)
        out_t = jnp.where(first, a0, a1) / l
        o_ref[0, :, p * 128:(p + 1) * 128] = out_t.T.astype(o_ref.dtype)


def _mla_attention(qp, kp, vt, pairs=2):
    b, s, _ = qp.shape
    nblk, width, tq = vt.shape[1:]
    groups = MLA_HEADS // (2 * pairs)
    return pl.pallas_call(
        functools.partial(_mla_body, tq=tq, pairs=pairs),
        grid=(b, groups, s // tq),
        in_specs=[
            pl.BlockSpec((1, tq, MLA_PAIR * pairs), lambda bi, p, i: (bi, i, p)),
            pl.BlockSpec((1, s, MLA_PAIR * pairs), lambda bi, p, i: (bi, 0, p)),
            pl.BlockSpec((1, nblk, 128 * pairs, tq), lambda bi, p, i: (bi, 0, p, 0)),
        ],
        out_specs=pl.BlockSpec((1, tq, 128 * pairs), lambda bi, p, i: (bi, i, p)),
        out_shape=jax.ShapeDtypeStruct((b, s, MLA_HEADS * MLA_V), BF),
        compiler_params=_cparams(("parallel", "parallel", "arbitrary")),
        name="mla_attn",
    )(qp, kp, vt)


def _dil_body(cur_ref, prev_ref, bias_ref, o_ref, lse_ref, *, tl, nres):
    j = pl.program_id(2)
    lane = lax.broadcasted_iota(jnp.int32, (1, 2 * HEAD_DIM), 1)
    first = lane < HEAD_DIM
    kj = lax.broadcasted_iota(jnp.int32, (BAND, 2 * BAND), 1)
    has_prev = (kj >= BAND) | (j > 0)
    pw = 2 * HEAD_DIM
    pairs = DIL_WIDTH // pw
    heads = range(2 * pairs)
    for rr, sb in [(rr, sb) for rr in range(nres) for sb in range(tl // BAND)]:
        rows = slice(sb * BAND, (sb + 1) * BAND)
        prev = slice((sb - 1) * BAND, sb * BAND)
        q_cols, k_cols, v_cols = (slice((3 * rr + c) * DIL_WIDTH, (3 * rr + c + 1) * DIL_WIDTH) for c in range(3))
        out_base = rr * DIL_WIDTH
        k_prev = prev_ref[0, :, k_cols] if sb == 0 else cur_ref[0, prev, k_cols]
        v_prev = prev_ref[0, :, v_cols] if sb == 0 else cur_ref[0, prev, v_cols]
        kcat = jnp.concatenate([k_prev, cur_ref[0, rows, k_cols]], axis=0)
        vcat = jnp.concatenate([v_prev, cur_ref[0, rows, v_cols]], axis=0)
        qs = cur_ref[0, rows, q_cols]
        q_heads, v_heads = [], []
        for p in range(pairs):
            qb = qs[:, p * pw:(p + 1) * pw]
            vb = vcat[:, p * pw:(p + 1) * pw]
            zero, ones = jnp.zeros_like(qb), jnp.ones_like(vb)
            q_heads += [jnp.where(first, qb, zero), jnp.where(first, zero, qb)]
            v_heads += [jnp.where(first, vb, ones), jnp.where(first, ones, vb)]
        ss = [_dot_t(q_heads[h], kcat[:, (h // 2) * pw:(h // 2 + 1) * pw]) + bias_ref[h] for h in heads]
        if sb == 0:
            ss = [jnp.where(has_prev, s, NEG) for s in ss]
        ms = [jnp.max(s, axis=-1, keepdims=True) for s in ss]
        prs = [jnp.exp2(ss[h] - ms[h]).astype(BF) for h in heads]
        pvs = [_dot(prs[h], v_heads[h]) for h in heads]
        for p in range(pairs):
            a0, a1 = pvs[2 * p], pvs[2 * p + 1]
            cols = slice(out_base + p * pw, out_base + (p + 1) * pw)
            l = pltpu.roll(jnp.where(first, a1, a0), HEAD_DIM, 1)
            o_ref[0, rows, cols] = (jnp.where(first, a0, a1) / l).astype(o_ref.dtype)
            lse_ref[0, rows, cols] = jnp.where(first, ms[2 * p], ms[2 * p + 1]) * LN2 + jnp.log(l)


def _dil_attention(view, bias, dilation, tl=512):
    r = dilation
    batch, length, _ = view.shape
    tl = min(tl, length)
    sub = tl // BAND
    nres = min(r, DIL_RES_PER_STEP)
    wide = nres * 3 * DIL_WIDTH
    cur = pl.BlockSpec((1, tl, wide), lambda bi, rg, j: (bi, j, rg))
    prev = pl.BlockSpec((1, BAND, wide), lambda bi, rg, j: (bi, jnp.maximum(j * sub - 1, 0), rg))
    out_spec = pl.BlockSpec((1, tl, nres * DIL_WIDTH), lambda bi, rg, j: (bi, j, rg))
    return pl.pallas_call(
        functools.partial(_dil_body, tl=tl, nres=nres),
        grid=(batch, r // nres, length // tl),
        in_specs=[cur, prev, pl.BlockSpec(bias.shape, lambda bi, rg, j: (0, 0, 0))],
        out_specs=[out_spec, out_spec],
        out_shape=[jax.ShapeDtypeStruct((batch, length, r * DIL_WIDTH), BF),
                   jax.ShapeDtypeStruct((batch, length, r * DIL_WIDTH), F32)],
        compiler_params=_cparams(("parallel", "parallel", "arbitrary")),
        name=f"dil_attn_r{r}",
    )(view, view, bias)


def _merge_body(h_ref, oa_ref, ob_ref, oc0_ref, oc1_ref, oc2_ref, l0_ref, l1_ref, l2_ref,
                gn_ref, wg_ref, gbias_ref, wa_ref, wb_ref, wc_ref, wo_ref, o_ref, *scratch):
    tm, d = h_ref.shape[1:]
    scratch = list(scratch)
    h = h_ref[0]
    u = _rms(h, gn_ref[...]).astype(BF)

    def gate(c):
        return _sigmoid(_dot(u, wg_ref[:, c * d:(c + 1) * d]) + gbias_ref[c:c + 1, :])

    def token_major(ref, r):
        if r == 1:
            return ref[0].astype(F32)
        buf = scratch.pop(0)
        chunks = DIL_WIDTH // LANES
        for res in range(r):
            for c in range(chunks):
                lo = res * DIL_WIDTH + c * LANES
                buf[c, pl.ds(res, tm // r, stride=r), :] = ref[0, :, lo:lo + LANES].astype(F32)
        return jnp.concatenate([buf[c] for c in range(chunks)], axis=1)

    dils = [dil for _, dil in DIL_GROUPS]
    merged = gate(0) * _dot(oa_ref[0], wa_ref[...]) + gate(1) * _dot(ob_ref[0], wb_ref[...])
    gate_c = gate(2)
    l0, l1, l2 = (token_major(ref, r) for ref, r in zip((l0_ref, l1_ref, l2_ref), dils))
    mx = jnp.maximum(jnp.maximum(l0, l1), l2)
    e0, e1, e2 = jnp.exp(l0 - mx), jnp.exp(l1 - mx), jnp.exp(l2 - mx)
    o0, o1, o2 = (token_major(ref, r) for ref, r in zip((oc0_ref, oc1_ref, oc2_ref), dils))
    oc = (e0 * o0 + e1 * o1 + e2 * o2) / (e0 + e1 + e2)
    merged = merged + gate_c * _dot(oc.astype(BF), wc_ref[...])
    o_ref[0] = h + _dot(merged.astype(BF), wo_ref[...])


def _merge(h3, oa, ob, ocs, lses, g_mix, w_gate, gbias, wa, wb, wc, wo, tm=512):
    b, s, d = h3.shape
    tm = min(tm, s)
    row = lambda n: pl.BlockSpec((1, tm, n), lambda bi, i: (bi, i, 0))
    views = [pl.BlockSpec((1, tm // r, r * DIL_WIDTH), lambda bi, i: (bi, i, 0)) for _, r in DIL_GROUPS]
    n_buf = 2 * sum(1 for _, r in DIL_GROUPS if r > 1)
    params = (g_mix, w_gate, gbias, wa, wb, wc, wo)
    return pl.pallas_call(
        _merge_body,
        grid=(b, s // tm),
        in_specs=[row(d), row(SB_WIDTH), row(SB_WIDTH)] + views + views + [_resident(p.shape) for p in params],
        out_specs=row(d),
        out_shape=jax.ShapeDtypeStruct((b, s, d), F32),
        scratch_shapes=[pltpu.VMEM((DIL_WIDTH // LANES, tm, LANES), F32)] * n_buf,
        compiler_params=_cparams(("parallel", "parallel")),
        name="merge",
    )(h3, oa, ob, *ocs, *lses, *params)


def _memkv_body(m_ref, g_ref, w_ref, o_ref):
    u = _rms(m_ref[0], g_ref[...]).astype(BF)
    o_ref[0] = _dot(u, w_ref[...]).astype(o_ref.dtype)


def _mem_kv(mem, g, w):
    b, m, d = mem.shape
    n = w.shape[1]
    return pl.pallas_call(
        _memkv_body,
        grid=(b,),
        in_specs=[pl.BlockSpec((1, m, d), lambda i: (i, 0, 0)),
                  pl.BlockSpec((1, d), lambda i: (0, 0)),
                  pl.BlockSpec((d, n), lambda i: (0, 0))],
        out_specs=pl.BlockSpec((1, m, n), lambda i: (i, 0, 0)),
        out_shape=jax.ShapeDtypeStruct((b, m, n), BF),
        compiler_params=_cparams(("parallel",)),
        name="mem_kv",
    )(mem, g, w)


def _xattn_body(h_ref, g_ref, kv_ref, wq_ref, wo_ref, o_ref):
    x = h_ref[0]
    u = _rms(x, g_ref[...]).astype(BF)
    q = _dot(u, wq_ref[...]).astype(BF)
    kv = kv_ref[0]
    width = X_HEADS * X_HEAD_DIM
    scale = X_HEAD_DIM ** -0.5
    outs = []
    for h in range(X_HEADS):
        cols = slice(h * X_HEAD_DIM, (h + 1) * X_HEAD_DIM)
        s = _dot_t(q[:, cols], kv[:, cols]) * scale
        m = jnp.max(s, axis=-1, keepdims=True)
        e = jnp.exp(s - m)
        p = e / jnp.sum(e, axis=-1, keepdims=True)
        outs.append(_dot(p.astype(BF), kv[:, width + h * X_HEAD_DIM:width + (h + 1) * X_HEAD_DIM]))
    o = jnp.concatenate(outs, axis=-1).astype(BF)
    o_ref[0] = x + _dot(o, wo_ref[...])


def _xattn(h3, g, kv, wq, wo, tm=512):
    b, s, d = h3.shape
    tm = min(tm, s)
    m = kv.shape[1]
    return pl.pallas_call(
        _xattn_body,
        grid=(b, s // tm),
        in_specs=[pl.BlockSpec((1, tm, d), lambda bi, i: (bi, i, 0)),
                  pl.BlockSpec((1, d), lambda bi, i: (0, 0)),
                  pl.BlockSpec((1, m, kv.shape[2]), lambda bi, i: (bi, 0, 0)),
                  pl.BlockSpec(wq.shape, lambda bi, i: (0, 0)),
                  pl.BlockSpec(wo.shape, lambda bi, i: (0, 0))],
        out_specs=pl.BlockSpec((1, tm, d), lambda bi, i: (bi, i, 0)),
        out_shape=jax.ShapeDtypeStruct((b, s, d), F32),
        compiler_params=_cparams(("parallel", "parallel")),
        name="xattn",
    )(h3, g, kv, wq, wo)


def _rel_bucket(dist):
    exact = REL_BUCKETS // 2
    d = jnp.maximum(dist, exact).astype(F32)
    large = exact + (jnp.log(d / exact) / math.log(REL_MAX_DIST / exact)
                     * (REL_BUCKETS - exact)).astype(jnp.int32)
    return jnp.where(dist < exact, dist, jnp.minimum(large, REL_BUCKETS - 1))


def _band_bias(rel_bias, group, dilation):
    heads = DIL_WIDTH // HEAD_DIM
    qi = jnp.arange(BAND)
    kj = jnp.arange(2 * BAND)
    steps = (qi[:, None] + BAND) - kj[None, :]
    table = rel_bias[:, group * heads:(group + 1) * heads].astype(F32)
    bucket = _rel_bucket(jnp.clip(steps, 0, BAND) * dilation)
    onehot = (bucket[None] == jnp.arange(REL_BUCKETS)[:, None, None]).astype(F32)
    bias = jnp.einsum('bh,bqk->hqk', table * LOG2E, onehot, precision=lax.Precision.HIGHEST)
    return jnp.where((steps >= 0) & (steps <= BAND), bias, NEG)


def _swap_halves(w, width):
    k, n = w.shape
    w = w.reshape(k, n // width, 2, width // 2)
    return w[:, :, ::-1, :].reshape(k, n)


def _pack_w_in(w_in):
    d = w_in.shape[0]
    seg_a = w_in[:, :COL_KR].astype(BF)
    kr = w_in[:, COL_KR:COL_DIL]
    zeros = jnp.zeros((d, LANES - 2 * MLA_ROPE), w_in.dtype)
    kr_sw = _swap_halves(kr, MLA_ROPE)
    w_rope = jnp.concatenate([kr, kr, zeros, kr_sw, kr_sw, zeros], axis=1).astype(BF)
    group = 3 * DIL_WIDTH
    seg_d = [w_in[:, COL_DIL + g * group:COL_DIL + (g + 1) * group].astype(BF) for g in range(len(DIL_GROUPS))]
    seg_g = w_in[:, COL_DIL + len(DIL_GROUPS) * group:].astype(BF)
    return seg_a, w_rope, seg_d, seg_g


def _pack_w_uq(w_uq):
    k = w_uq.shape[0]
    w = w_uq.reshape(k, MLA_HEADS, MLA_NOPE + MLA_ROPE)
    nope = w[:, :, :MLA_NOPE].reshape(k, MLA_HEADS * MLA_NOPE)
    rope = w[:, :, MLA_NOPE:]

    def pair_layout(rp):
        rp = rp.reshape(k, MLA_HEADS // 2, 2 * MLA_ROPE)
        pad = jnp.zeros((k, MLA_HEADS // 2, 128 - 2 * MLA_ROPE), rp.dtype)
        return jnp.concatenate([rp, pad], axis=-1).reshape(k, (MLA_HEADS // 2) * 128)

    rope_sw = rope.reshape(k, MLA_HEADS, 2, MLA_ROPE // 2)[:, :, ::-1, :].reshape(k, MLA_HEADS, MLA_ROPE)
    return jnp.concatenate([nope, pair_layout(rope), pair_layout(rope_sw)], axis=1).astype(BF)


def _pack_w_ukv(w_ukv):
    k = w_ukv.shape[0]
    w = w_ukv.reshape(k, MLA_HEADS, MLA_NOPE + MLA_V)
    return jnp.concatenate([w[:, :, :MLA_NOPE].reshape(k, -1), w[:, :, MLA_NOPE:].reshape(k, -1)],
                           axis=1).astype(BF)


def _rope_tables(seq):
    half = MLA_ROPE // 2
    freqs = ROPE_THETA ** (-jnp.arange(half, dtype=F32) / half)
    ang = jnp.arange(seq).astype(F32)[:, None] * freqs[None, :]
    cos, sin = jnp.cos(ang), jnp.sin(ang)
    pad = jnp.zeros((seq, 128 - 2 * MLA_ROPE), F32)
    cos_t = jnp.concatenate([cos, cos, cos, cos, pad], axis=1)
    sin_t = jnp.concatenate([-sin, sin, -sin, sin, pad], axis=1)
    return cos_t, sin_t


def kernel(x, mem, ffn1_norm, ffn1_w_gate, ffn1_w_up, ffn1_w_down, mix_norm, w_in, gate_bias, mla_q_norm, mla_w_uq, mla_kv_norm, mla_w_ukv, w_branch_a, w_branch_b, w_branch_c, w_mix_out, rel_bias, xattn_norm, mem_norm, xattn_w_q, xattn_w_kv, xattn_w_o, ffn2_norm, ffn2_w_gate, ffn2_w_up, ffn2_w_down, final_norm):
    b, s, d = x.shape
    t = b * s
    depth = w_in.shape[0]
    cos_t, sin_t = _rope_tables(s)
    biases = [_band_bias(rel_bias, g, dil) for g, (_, dil) in enumerate(DIL_GROUPS)]
    row = lambda v: v.reshape(1, -1)
    bf = lambda w: w.astype(BF)
    fg = row(final_norm)

    h = x.reshape(t, d)
    for l in range(depth):
        h = _ffn(h, row(ffn1_norm[l]), bf(ffn1_w_gate[l]), bf(ffn1_w_up[l]), bf(ffn1_w_down[l]), fg, False)

        w_a, w_rope, w_d, w_g = _pack_w_in(w_in[l])
        g_mix = row(mix_norm[l])
        h3 = h.reshape(b, s, d)
        tm_a = min(MLA_BLOCK, s)
        sb, qp, kp, vt = _proj_a(h, g_mix, w_a, w_rope, cos_t, sin_t, row(mla_q_norm[l]), row(mla_kv_norm[l]),
                                 _pack_w_uq(mla_w_uq[l]), _pack_w_ukv(mla_w_ukv[l]), s, tm_a)
        o_a = _sb_attention(sb.reshape(b, s, sb.shape[1]))
        o_b = _mla_attention(qp.reshape(b, s, qp.shape[1]), kp.reshape(b, s, kp.shape[1]),
                             vt.reshape(b, s // tm_a, vt.shape[1], tm_a))
        ocs, lses = [], []
        for g, (_, dil) in enumerate(DIL_GROUPS):
            o, lse = _dil_attention(_dil_proj(h3, g_mix, w_d[g], dil), biases[g], dil)
            ocs.append(o)
            lses.append(lse)
        h3 = _merge(h3, o_a, o_b, ocs, lses, g_mix, w_g, gate_bias[l],
                    bf(w_branch_a[l]), bf(w_branch_b[l]), bf(w_branch_c[l]), bf(w_mix_out[l]))

        kv = _mem_kv(mem, row(mem_norm[l]), bf(xattn_w_kv[l]))
        h = _xattn(h3, row(xattn_norm[l]), kv, bf(xattn_w_q[l]), bf(xattn_w_o[l])).reshape(t, d)

        h = _ffn(h, row(ffn2_norm[l]), bf(ffn2_w_gate[l]), bf(ffn2_w_up[l]), bf(ffn2_w_down[l]), fg,
                 l == depth - 1)
    return h.reshape(b, s, d)
```

```python
import functools
import math

import jax
import jax.numpy as jnp
import numpy as np
from jax import lax
from jax.experimental import pallas as pl
from jax.experimental.pallas import tpu as pltpu

BF = jnp.bfloat16
F32 = jnp.float32

EPS = 1e-6
NEG = -1e30
FFN_RESIDUAL = 0.5

SB_WIDTH = 512
MLA_HEADS = 8
MLA_Q_RANK = 384
MLA_KV_RANK = 128
MLA_NOPE = 64
MLA_ROPE = 32
MLA_V = 64
ROPE_THETA = 10000.0
DIL_GROUPS = ((128, 1), (512, 4), (2048, 16))
DIL_WIDTH = 512
REL_BUCKETS = 32
REL_MAX_DIST = 2048
X_HEADS = 4
X_HEAD_DIM = 128
HEAD_DIM = 64
BAND = 128
LANES = 128
MLA_PAIR = 2 * LANES

COL_CQ = 3 * SB_WIDTH
COL_CKV = COL_CQ + MLA_Q_RANK
COL_KR = COL_CKV + MLA_KV_RANK
COL_DIL = COL_KR + MLA_ROPE
SB_DEAD_LOG2 = -150.0
LOG2E = math.log2(math.e)
LN2 = math.log(2.0)

V7X_VMEM_LIMIT = 48 * 1024 * 1024
FF_CHUNK = 256
MLA_BLOCK = 512
DIL_RES_PER_STEP = 4
QSCALE = HEAD_DIM ** -0.5 * LOG2E


def _cparams(sem):
    return pltpu.CompilerParams(dimension_semantics=sem, vmem_limit_bytes=V7X_VMEM_LIMIT)


def _rms(x, g):
    ms = jnp.mean(x * x, axis=-1, keepdims=True)
    return x * lax.rsqrt(ms + EPS) * g


def _sigmoid(x):
    return 0.5 * jnp.tanh(0.5 * x) + 0.5


def _dot(a, b):
    return jnp.dot(a, b, preferred_element_type=F32)


def _dot_t(a, b):
    return lax.dot_general(a, b, (((1,), (1,)), ((), ())), preferred_element_type=F32)


def _resident(shape):
    nd = len(shape)
    return pl.BlockSpec(shape, lambda *_: (0,) * nd, pipeline_mode=pl.Buffered(1))


def _ffn_body(h_ref, g_ref, wg_ref, wu_ref, wd_ref, fg_ref, o_ref, act_ref, *, final):
    x = h_ref[...]
    u = _rms(x, g_ref[...]).astype(BF)
    d_ff = wg_ref.shape[1]
    for c in range(d_ff // FF_CHUNK):
        sl = slice(c * FF_CHUNK, (c + 1) * FF_CHUNK)
        a = _dot(u, wg_ref[:, sl])
        b = _dot(u, wu_ref[:, sl])
        act_ref[:, sl] = (a * jax.nn.sigmoid(a) * b).astype(BF)
    y = x + FFN_RESIDUAL * _dot(act_ref[...], wd_ref[...])
    if final:
        y = _rms(y, fg_ref[...])
    o_ref[...] = y


def _ffn(h, g, wg, wu, wd, fg, final, tm=512):
    t, d = h.shape
    d_ff = wg.shape[1]
    tm = min(tm, t)
    return pl.pallas_call(
        functools.partial(_ffn_body, final=final),
        grid=(t // tm,),
        in_specs=[
            pl.BlockSpec((tm, d), lambda i: (i, 0)),
            _resident((1, d)),
            _resident((d, d_ff)),
            _resident((d, d_ff)),
            _resident((d_ff, d)),
            _resident((1, d)),
        ],
        out_specs=pl.BlockSpec((tm, d), lambda i: (i, 0)),
        out_shape=jax.ShapeDtypeStruct((t, d), F32),
        scratch_shapes=[pltpu.VMEM((tm, d_ff), BF)],
        compiler_params=_cparams(("parallel",)),
        name="ffn",
    )(h, g, wg, wu, wd, fg)


def _proj_a_body(h_ref, g_ref, w_ref, wr_ref, cos_ref, sin_ref, qg_ref, kvg_ref, wq_ref, wkv_ref,
                 sb_ref, qp_ref, kp_ref, vt_ref, *, scale):
    u = _rms(h_ref[...], g_ref[...]).astype(BF)
    for c, factor in enumerate((QSCALE, None, None)):
        sl = slice(c * SB_WIDTH, (c + 1) * SB_WIDTH)
        y = _dot(u, w_ref[:, sl])
        sb_ref[:, sl] = (y if factor is None else y * factor).astype(sb_ref.dtype)
    lat = _dot(u, w_ref[:, COL_CQ:COL_KR])
    kr = _dot(u, wr_ref[...])
    c_kv0 = COL_CKV - COL_CQ
    nq = _rms(lat[:, :c_kv0], qg_ref[...]).astype(BF)
    q = _dot(nq, wq_ref[...])
    nkv = _rms(lat[:, c_kv0:], kvg_ref[...]).astype(BF)
    kv = _dot(nkv, wkv_ref[...])
    cos = cos_ref[...]
    sin = sin_ref[...]
    k_pe = (kr[:, :LANES] * cos + kr[:, LANES:] * sin).astype(BF)
    width = q.shape[1] // 3
    for p in range(width // LANES):
        lo, hi = p * LANES, (p + 1) * LANES
        qp_ref[:, 2 * lo:2 * lo + LANES] = (q[:, lo:hi] * scale).astype(BF)
        q_pe = q[:, width + lo:width + hi] * cos + q[:, 2 * width + lo:2 * width + hi] * sin
        qp_ref[:, 2 * lo + LANES:2 * hi] = (q_pe * scale).astype(BF)
        kp_ref[:, 2 * lo:2 * lo + LANES] = kv[:, lo:hi].astype(BF)
        kp_ref[:, 2 * lo + LANES:2 * hi] = k_pe
    vt_ref[0] = kv[:, width:].T.astype(BF)


def _proj_a(h, g, w, w_rope, cos, sin, qg, kvg, wq, wkv, seq, tm):
    t, d = h.shape
    width = MLA_HEADS * MLA_NOPE
    scale = (MLA_NOPE + MLA_ROPE) ** -0.5 * LOG2E
    per_seq = seq // tm
    params = (g, w, w_rope)
    tables = [pl.BlockSpec((tm, LANES), lambda i: (i % per_seq, 0))] * 2
    mla_params = (qg, kvg, wq, wkv)
    row = lambda n: pl.BlockSpec((tm, n), lambda i: (i, 0))
    return pl.pallas_call(
        functools.partial(_proj_a_body, scale=scale),
        grid=(t // tm,),
        in_specs=[row(d)] + [_resident(p.shape) for p in params] + tables + [_resident(p.shape) for p in mla_params],
        out_specs=[row(COL_CQ), row(2 * width), row(2 * width),
                   pl.BlockSpec((1, width, tm), lambda i: (i, 0, 0))],
        out_shape=[jax.ShapeDtypeStruct((t, COL_CQ), BF), jax.ShapeDtypeStruct((t, 2 * width), BF),
                   jax.ShapeDtypeStruct((t, 2 * width), BF), jax.ShapeDtypeStruct((t // tm, width, tm), BF)],
        compiler_params=_cparams(("parallel",)),
        name="proj_a",
    )(h, g, w, w_rope, cos, sin, qg, kvg, wq, wkv)


def _dil_proj_body(h_ref, g_ref, w_ref, o_ref, *stage, r):
    rows = h_ref.shape[1] // r
    n = w_ref.shape[1]
    u = _rms(h_ref[0], g_ref[...])

    def project(ub, base):
        for c, factor in enumerate((QSCALE, None, None)):
            sl = slice(c * DIL_WIDTH, (c + 1) * DIL_WIDTH)
            y = _dot(ub, w_ref[:, sl])
            o_ref[0, :, base + c * DIL_WIDTH:base + (c + 1) * DIL_WIDTH] = (
                y if factor is None else y * factor).astype(o_ref.dtype)

    if r == 1:
        project(u.astype(BF), 0)
        return
    (stage,) = stage
    chunks = u.shape[1] // LANES
    for c in range(chunks):
        stage[c] = u[:, c * LANES:(c + 1) * LANES]
    for res in range(r):
        ur = jnp.concatenate([stage[c, pl.ds(res, rows, stride=r), :] for c in range(chunks)], axis=1)
        project(ur.astype(BF), res * n)


def _dil_proj(h3, g, w, r):
    b, s, d = h3.shape
    n = w.shape[1]
    tm = min(s, max(1024, BAND * r) if r > 1 else 512)
    return pl.pallas_call(
        functools.partial(_dil_proj_body, r=r),
        grid=(b, s // tm),
        in_specs=[
            pl.BlockSpec((1, tm, d), lambda bi, i: (bi, i, 0)),
            pl.BlockSpec((1, d), lambda bi, i: (0, 0)),
            pl.BlockSpec((d, n), lambda bi, i: (0, 0)),
        ],
        out_specs=pl.BlockSpec((1, tm // r, r * n), lambda bi, i: (bi, i, 0)),
        out_shape=jax.ShapeDtypeStruct((b, s // r, r * n), BF),
        scratch_shapes=[pltpu.VMEM((d // LANES, tm, LANES), F32)] if r > 1 else [],
        compiler_params=_cparams(("parallel", "parallel")),
        name=f"dil_proj_r{r}",
    )(h3, g, w)


def _sb_body(q_ref, k_ref, v_ref, o_ref, *, tq, pairs):
    i = pl.program_id(2)
    pw = 2 * HEAD_DIM
    lane = lax.broadcasted_iota(jnp.int32, (1, pw), 1)
    first = lane < HEAD_DIM
    q_heads = []
    for p in range(pairs):
        q = q_ref[0, :, p * pw:(p + 1) * pw]
        zero = jnp.zeros_like(q)
        q_heads += [jnp.where(first, q, zero), jnp.where(first, zero, q)]
    row = lax.broadcasted_iota(jnp.int32, (tq, tq), 0)
    col = lax.broadcasted_iota(jnp.int32, (tq, tq), 1)
    strict = col < row
    later = jnp.where(row > col, 1.0, 0.0).astype(BF)

    def block(j, accs, rs, diag):
        start = pl.multiple_of(j * tq, tq)
        heads = range(2 * pairs)
        ks = [k_ref[0, pl.ds(start, tq), p * pw:(p + 1) * pw] for p in range(pairs)]
        vs = [v_ref[0, pl.ds(start, tq), p * pw:(p + 1) * pw] for p in range(pairs)]
        zs = [_dot_t(q_heads[h], ks[h // 2]) for h in heads]
        log_betas, log_keeps = [], []
        for z in zs:
            sp = jnp.log(1.0 + jnp.exp2(-jnp.abs(z))) * LOG2E
            log_beta = jnp.minimum(z, 0.0) - sp
            log_keep = log_beta - z
            if diag:
                log_keep = jnp.where(strict, log_keep, 0.0)
            log_betas.append(log_beta)
            log_keeps.append(log_keep)
        afters = [_dot(log_keeps[h].astype(BF), later) for h in heads]
        ws = []
        for h in heads:
            w = jnp.exp2(log_betas[h] + afters[h] + rs[h])
            if diag:
                w = jnp.where(strict, w, 0.0)
            ws.append(w.astype(BF))
        outs = [_dot(ws[h], vs[h // 2]) for h in heads]
        new_rs = [rs[h] + jnp.sum(log_keeps[h], axis=-1, keepdims=True) for h in heads]
        new_accs = [accs[p] + jnp.where(first, outs[2 * p], outs[2 * p + 1]) for p in range(pairs)]
        return tuple(new_accs), tuple(new_rs)

    def remaining(rs):
        return jnp.max(functools.reduce(jnp.maximum, rs))

    def cond(c):
        return (c[0] < i) & (c[1] > SB_DEAD_LOG2)

    def body(c):
        accs, rs = block(i - 1 - c[0], c[2], c[3], False)
        return c[0] + 1, remaining(rs), accs, rs

    accs = tuple(jnp.zeros((tq, pw), F32) for _ in range(pairs))
    rs = tuple(jnp.zeros((tq, 1), F32) for _ in range(2 * pairs))
    accs, rs = block(i, accs, rs, True)
    out = lax.while_loop(cond, body, (jnp.int32(0), remaining(rs), accs, rs))
    for p in range(pairs):
        o_ref[0, :, p * pw:(p + 1) * pw] = out[2][p].astype(o_ref.dtype)


def _sb_attention(a3, tq=256, pairs=4):
    b, s, _ = a3.shape
    tq = min(tq, s)
    w = 2 * HEAD_DIM * pairs
    groups = SB_WIDTH // w
    return pl.pallas_call(
        functools.partial(_sb_body, tq=tq, pairs=pairs),
        grid=(b, groups, s // tq),
        in_specs=[
            pl.BlockSpec((1, tq, w), lambda bi, p, i: (bi, i, p)),
            pl.BlockSpec((1, s, w), lambda bi, p, i: (bi, 0, groups + p)),
            pl.BlockSpec((1, s, w), lambda bi, p, i: (bi, 0, 2 * groups + p)),
        ],
        out_specs=pl.BlockSpec((1, tq, w), lambda bi, p, i: (bi, i, p)),
        out_shape=jax.ShapeDtypeStruct((b, s, SB_WIDTH), BF),
        compiler_params=_cparams(("parallel", "parallel", "arbitrary")),
        name="sb_attn",
    )(a3, a3, a3)


def _mla_body(q_ref, k_ref, v_ref, o_ref, *, tq, pairs):
    i = pl.program_id(2)
    lane = lax.broadcasted_iota(jnp.int32, (1, MLA_PAIR), 1)
    sel0 = (lane < 64) | ((lane >= 128) & (lane < 160))
    sel1 = ((lane >= 64) & (lane < 128)) | ((lane >= 160) & (lane < 192))
    q_heads = []
    for p in range(pairs):
        q = q_ref[0, :, p * MLA_PAIR:(p + 1) * MLA_PAIR]
        zero = jnp.zeros_like(q)
        q_heads += [jnp.where(sel0, q, zero), jnp.where(sel1, q, zero)]
    key = lax.broadcasted_iota(jnp.int32, (tq, tq), 0)
    qry = lax.broadcasted_iota(jnp.int32, (tq, tq), 1)
    causal = key <= qry
    first = lax.broadcasted_iota(jnp.int32, (128, 1), 0) < 64

    def block(j, carry, diag, nblk=1):
        start = pl.multiple_of(j * tq, tq)
        heads = range(2 * pairs)
        ks = [k_ref[0, pl.ds(start, nblk * tq), p * MLA_PAIR:(p + 1) * MLA_PAIR] for p in range(pairs)]
        v_heads = []
        for p in range(pairs):
            vt = jnp.concatenate([v_ref[0, j + n, p * 128:(p + 1) * 128, :] for n in range(nblk)], axis=1)
            ones = jnp.ones_like(vt)
            v_heads += [jnp.where(first, vt, ones), jnp.where(first, ones, vt)]
        ss = [_dot_t(ks[h // 2], q_heads[h]) for h in heads]
        if diag:
            ss = [jnp.where(causal, s, NEG) for s in ss]
        m_news = [jnp.maximum(carry[h][0], jnp.max(ss[h], axis=0, keepdims=True)) for h in heads]
        prs = [jnp.exp2(ss[h] - m_news[h]).astype(BF) for h in heads]
        pvs = [_dot(v_heads[h], prs[h]) for h in heads]
        return tuple((m_news[h], jnp.exp2(carry[h][0] - m_news[h]) * carry[h][1] + pvs[h]) for h in heads)

    one = (jnp.full((1, tq), NEG, F32), jnp.zeros((128, tq), F32))
    carry = lax.fori_loop(0, i // 2, lambda t, c: block(2 * t, c, False, 2), (one,) * (2 * pairs))
    carry = lax.cond(i % 2 == 1, lambda c: block(i - 1, c, False), lambda c: c, carry)
    carry = block(i, carry, True)
    for p in range(pairs):
        a0, a1 = carry[2 * p][1], carry[2 * p + 1][1]
        lsw = jnp.where(first, a1, a0)
        l = jnp.concatenate([lsw[64:], lsw[:64]], axis=0)
        out_t = jnp.where(first, a0, a1) / l
        o_ref[0, :, p * 128:(p + 1) * 128] = out_t.T.astype(o_ref.dtype)


def _mla_attention(qp, kp, vt, pairs=2):
    b, s, _ = qp.shape
    nblk, width, tq = vt.shape[1:]
    groups = MLA_HEADS // (2 * pairs)
    return pl.pallas_call(
        functools.partial(_mla_body, tq=tq, pairs=pairs),
        grid=(b, groups, s // tq),
        in_specs=[
            pl.BlockSpec((1, tq, MLA_PAIR * pairs), lambda bi, p, i: (bi, i, p)),
            pl.BlockSpec((1, s, MLA_PAIR * pairs), lambda bi, p, i: (bi, 0, p)),
            pl.BlockSpec((1, nblk, 128 * pairs, tq), lambda bi, p, i: (bi, 0, p, 0)),
        ],
        out_specs=pl.BlockSpec((1, tq, 128 * pairs), lambda bi, p, i: (bi, i, p)),
        out_shape=jax.ShapeDtypeStruct((b, s, MLA_HEADS * MLA_V), BF),
        compiler_params=_cparams(("parallel", "parallel", "arbitrary")),
        name="mla_attn",
    )(qp, kp, vt)


def _dil_body(cur_ref, prev_ref, bias_ref, o_ref, lse_ref, *, tl, nres):
    j = pl.program_id(2)
    lane = lax.broadcasted_iota(jnp.int32, (1, 2 * HEAD_DIM), 1)
    first = lane < HEAD_DIM
    kj = lax.broadcasted_iota(jnp.int32, (BAND, 2 * BAND), 1)
    has_prev = (kj >= BAND) | (j > 0)
    pw = 2 * HEAD_DIM
    pairs = DIL_WIDTH // pw
    heads = range(2 * pairs)
    for rr, sb in [(rr, sb) for rr in range(nres) for sb in range(tl // BAND)]:
        rows = slice(sb * BAND, (sb + 1) * BAND)
        prev = slice((sb - 1) * BAND, sb * BAND)
        q_cols, k_cols, v_cols = (slice((3 * rr + c) * DIL_WIDTH, (3 * rr + c + 1) * DIL_WIDTH) for c in range(3))
        out_base = rr * DIL_WIDTH
        k_prev = prev_ref[0, :, k_cols] if sb == 0 else cur_ref[0, prev, k_cols]
        v_prev = prev_ref[0, :, v_cols] if sb == 0 else cur_ref[0, prev, v_cols]
        kcat = jnp.concatenate([k_prev, cur_ref[0, rows, k_cols]], axis=0)
        vcat = jnp.concatenate([v_prev, cur_ref[0, rows, v_cols]], axis=0)
        qs = cur_ref[0, rows, q_cols]
        q_heads, v_heads = [], []
        for p in range(pairs):
            qb = qs[:, p * pw:(p + 1) * pw]
            vb = vcat[:, p * pw:(p + 1) * pw]
            zero, ones = jnp.zeros_like(qb), jnp.ones_like(vb)
            q_heads += [jnp.where(first, qb, zero), jnp.where(first, zero, qb)]
            v_heads += [jnp.where(first, vb, ones), jnp.where(first, ones, vb)]
        ss = [_dot_t(q_heads[h], kcat[:, (h // 2) * pw:(h // 2 + 1) * pw]) + bias_ref[h] for h in heads]
        if sb == 0:
            ss = [jnp.where(has_prev, s, NEG) for s in ss]
        ms = [jnp.max(s, axis=-1, keepdims=True) for s in ss]
        prs = [jnp.exp2(ss[h] - ms[h]).astype(BF) for h in heads]
        pvs = [_dot(prs[h], v_heads[h]) for h in heads]
        for p in range(pairs):
            a0, a1 = pvs[2 * p], pvs[2 * p + 1]
            cols = slice(out_base + p * pw, out_base + (p + 1) * pw)
            l = pltpu.roll(jnp.where(first, a1, a0), HEAD_DIM, 1)
            o_ref[0, rows, cols] = (jnp.where(first, a0, a1) / l).astype(o_ref.dtype)
            lse_ref[0, rows, cols] = jnp.where(first, ms[2 * p], ms[2 * p + 1]) * LN2 + jnp.log(l)


def _dil_attention(view, bias, dilation, tl=512):
    r = dilation
    batch, length, _ = view.shape
    tl = min(tl, length)
    sub = tl // BAND
    nres = min(r, DIL_RES_PER_STEP)
    wide = nres * 3 * DIL_WIDTH
    cur = pl.BlockSpec((1, tl, wide), lambda bi, rg, j: (bi, j, rg))
    prev = pl.BlockSpec((1, BAND, wide), lambda bi, rg, j: (bi, jnp.maximum(j * sub - 1, 0), rg))
    out_spec = pl.BlockSpec((1, tl, nres * DIL_WIDTH), lambda bi, rg, j: (bi, j, rg))
    return pl.pallas_call(
        functools.partial(_dil_body, tl=tl, nres=nres),
        grid=(batch, r // nres, length // tl),
        in_specs=[cur, prev, pl.BlockSpec(bias.shape, lambda bi, rg, j: (0, 0, 0))],
        out_specs=[out_spec, out_spec],
        out_shape=[jax.ShapeDtypeStruct((batch, length, r * DIL_WIDTH), BF),
                   jax.ShapeDtypeStruct((batch, length, r * DIL_WIDTH), F32)],
        compiler_params=_cparams(("parallel", "parallel", "arbitrary")),
        name=f"dil_attn_r{r}",
    )(view, view, bias)


def _merge_body(h_ref, oa_ref, ob_ref, oc0_ref, oc1_ref, oc2_ref, l0_ref, l1_ref, l2_ref,
                gn_ref, wg_ref, gbias_ref, wa_ref, wb_ref, wc_ref, wo_ref, o_ref, *scratch):
    tm, d = h_ref.shape[1:]
    scratch = list(scratch)
    h = h_ref[0]
    u = _rms(h, gn_ref[...]).astype(BF)

    def gate(c):
        return _sigmoid(_dot(u, wg_ref[:, c * d:(c + 1) * d]) + gbias_ref[c:c + 1, :])

    def token_major(ref, r):
        if r == 1:
            return ref[0].astype(F32)
        buf = scratch.pop(0)
        chunks = DIL_WIDTH // LANES
        for res in range(r):
            for c in range(chunks):
                lo = res * DIL_WIDTH + c * LANES
                buf[c, pl.ds(res, tm // r, stride=r), :] = ref[0, :, lo:lo + LANES].astype(F32)
        return jnp.concatenate([buf[c] for c in range(chunks)], axis=1)

    dils = [dil for _, dil in DIL_GROUPS]
    merged = gate(0) * _dot(oa_ref[0], wa_ref[...]) + gate(1) * _dot(ob_ref[0], wb_ref[...])
    gate_c = gate(2)
    l0, l1, l2 = (token_major(ref, r) for ref, r in zip((l0_ref, l1_ref, l2_ref), dils))
    mx = jnp.maximum(jnp.maximum(l0, l1), l2)
    e0, e1, e2 = jnp.exp(l0 - mx), jnp.exp(l1 - mx), jnp.exp(l2 - mx)
    o0, o1, o2 = (token_major(ref, r) for ref, r in zip((oc0_ref, oc1_ref, oc2_ref), dils))
    oc = (e0 * o0 + e1 * o1 + e2 * o2) / (e0 + e1 + e2)
    merged = merged + gate_c * _dot(oc.astype(BF), wc_ref[...])
    o_ref[0] = h + _dot(merged.astype(BF), wo_ref[...])


def _merge(h3, oa, ob, ocs, lses, g_mix, w_gate, gbias, wa, wb, wc, wo, tm=512):
    b, s, d = h3.shape
    tm = min(tm, s)
    row = lambda n: pl.BlockSpec((1, tm, n), lambda bi, i: (bi, i, 0))
    views = [pl.BlockSpec((1, tm // r, r * DIL_WIDTH), lambda bi, i: (bi, i, 0)) for _, r in DIL_GROUPS]
    n_buf = 2 * sum(1 for _, r in DIL_GROUPS if r > 1)
    params = (g_mix, w_gate, gbias, wa, wb, wc, wo)
    return pl.pallas_call(
        _merge_body,
        grid=(b, s // tm),
        in_specs=[row(d), row(SB_WIDTH), row(SB_WIDTH)] + views + views + [_resident(p.shape) for p in params],
        out_specs=row(d),
        out_shape=jax.ShapeDtypeStruct((b, s, d), F32),
        scratch_shapes=[pltpu.VMEM((DIL_WIDTH // LANES, tm, LANES), F32)] * n_buf,
        compiler_params=_cparams(("parallel", "parallel")),
        name="merge",
    )(h3, oa, ob, *ocs, *lses, *params)


def _memkv_body(m_ref, g_ref, w_ref, o_ref):
    u = _rms(m_ref[0], g_ref[...]).astype(BF)
    o_ref[0] = _dot(u, w_ref[...]).astype(o_ref.dtype)


def _mem_kv(mem, g, w):
    b, m, d = mem.shape
    n = w.shape[1]
    return pl.pallas_call(
        _memkv_body,
        grid=(b,),
        in_specs=[pl.BlockSpec((1, m, d), lambda i: (i, 0, 0)),
                  pl.BlockSpec((1, d), lambda i: (0, 0)),
                  pl.BlockSpec((d, n), lambda i: (0, 0))],
        out_specs=pl.BlockSpec((1, m, n), lambda i: (i, 0, 0)),
        out_shape=jax.ShapeDtypeStruct((b, m, n), BF),
        compiler_params=_cparams(("parallel",)),
        name="mem_kv",
    )(mem, g, w)


def _xattn_body(h_ref, g_ref, kv_ref, wq_ref, wo_ref, o_ref):
    x = h_ref[0]
    u = _rms(x, g_ref[...]).astype(BF)
    q = (_dot(u, wq_ref[...]) * (X_HEAD_DIM ** -0.5 * LOG2E)).astype(BF)
    kv = kv_ref[0]
    width = X_HEADS * X_HEAD_DIM
    heads = range(X_HEADS)
    cols = [slice(h * X_HEAD_DIM, (h + 1) * X_HEAD_DIM) for h in heads]
    ss = [_dot_t(q[:, cols[h]], kv[:, cols[h]]) for h in heads]
    es = [jnp.exp2(s - jnp.max(s, axis=-1, keepdims=True)) for s in ss]
    ps = [(e / jnp.sum(e, axis=-1, keepdims=True)).astype(BF) for e in es]
    outs = [_dot(ps[h], kv[:, width + h * X_HEAD_DIM:width + (h + 1) * X_HEAD_DIM]) for h in heads]
    o = jnp.concatenate(outs, axis=-1).astype(BF)
    o_ref[0] = x + _dot(o, wo_ref[...])


def _xattn(h3, g, kv, wq, wo, tm=512):
    b, s, d = h3.shape
    tm = min(tm, s)
    m = kv.shape[1]
    return pl.pallas_call(
        _xattn_body,
        grid=(b, s // tm),
        in_specs=[pl.BlockSpec((1, tm, d), lambda bi, i: (bi, i, 0)),
                  pl.BlockSpec((1, d), lambda bi, i: (0, 0)),
                  pl.BlockSpec((1, m, kv.shape[2]), lambda bi, i: (bi, 0, 0)),
                  pl.BlockSpec(wq.shape, lambda bi, i: (0, 0)),
                  pl.BlockSpec(wo.shape, lambda bi, i: (0, 0))],
        out_specs=pl.BlockSpec((1, tm, d), lambda bi, i: (bi, i, 0)),
        out_shape=jax.ShapeDtypeStruct((b, s, d), F32),
        compiler_params=_cparams(("parallel", "parallel")),
        name="xattn",
    )(h3, g, kv, wq, wo)


def _rel_bucket(dist):
    exact = REL_BUCKETS // 2
    d = jnp.maximum(dist, exact).astype(F32)
    large = exact + (jnp.log(d / exact) / math.log(REL_MAX_DIST / exact)
                     * (REL_BUCKETS - exact)).astype(jnp.int32)
    return jnp.where(dist < exact, dist, jnp.minimum(large, REL_BUCKETS - 1))


def _band_bias(rel_bias, group, dilation):
    heads = DIL_WIDTH // HEAD_DIM
    qi = jnp.arange(BAND)
    kj = jnp.arange(2 * BAND)
    steps = (qi[:, None] + BAND) - kj[None, :]
    table = rel_bias[:, group * heads:(group + 1) * heads].astype(F32)
    bucket = _rel_bucket(jnp.clip(steps, 0, BAND) * dilation)
    onehot = (bucket[None] == jnp.arange(REL_BUCKETS)[:, None, None]).astype(F32)
    bias = jnp.einsum('bh,bqk->hqk', table * LOG2E, onehot, precision=lax.Precision.HIGHEST)
    return jnp.where((steps >= 0) & (steps <= BAND), bias, NEG)


def _swap_halves(w, width):
    k, n = w.shape
    w = w.reshape(k, n // width, 2, width // 2)
    return w[:, :, ::-1, :].reshape(k, n)


def _pack_w_in(w_in):
    d = w_in.shape[0]
    seg_a = w_in[:, :COL_KR].astype(BF)
    kr = w_in[:, COL_KR:COL_DIL]
    zeros = jnp.zeros((d, LANES - 2 * MLA_ROPE), w_in.dtype)
    kr_sw = _swap_halves(kr, MLA_ROPE)
    w_rope = jnp.concatenate([kr, kr, zeros, kr_sw, kr_sw, zeros], axis=1).astype(BF)
    group = 3 * DIL_WIDTH
    seg_d = [w_in[:, COL_DIL + g * group:COL_DIL + (g + 1) * group].astype(BF) for g in range(len(DIL_GROUPS))]
    seg_g = w_in[:, COL_DIL + len(DIL_GROUPS) * group:].astype(BF)
    return seg_a, w_rope, seg_d, seg_g


def _pack_w_uq(w_uq):
    k = w_uq.shape[0]
    w = w_uq.reshape(k, MLA_HEADS, MLA_NOPE + MLA_ROPE)
    nope = w[:, :, :MLA_NOPE].reshape(k, MLA_HEADS * MLA_NOPE)
    rope = w[:, :, MLA_NOPE:]

    def pair_layout(rp):
        rp = rp.reshape(k, MLA_HEADS // 2, 2 * MLA_ROPE)
        pad = jnp.zeros((k, MLA_HEADS // 2, 128 - 2 * MLA_ROPE), rp.dtype)
        return jnp.concatenate([rp, pad], axis=-1).reshape(k, (MLA_HEADS // 2) * 128)

    rope_sw = rope.reshape(k, MLA_HEADS, 2, MLA_ROPE // 2)[:, :, ::-1, :].reshape(k, MLA_HEADS, MLA_ROPE)
    return jnp.concatenate([nope, pair_layout(rope), pair_layout(rope_sw)], axis=1).astype(BF)


def _pack_w_ukv(w_ukv):
    k = w_ukv.shape[0]
    w = w_ukv.reshape(k, MLA_HEADS, MLA_NOPE + MLA_V)
    return jnp.concatenate([w[:, :, :MLA_NOPE].reshape(k, -1), w[:, :, MLA_NOPE:].reshape(k, -1)],
                           axis=1).astype(BF)


def _rope_tables(seq):
    half = MLA_ROPE // 2
    freqs = ROPE_THETA ** (-jnp.arange(half, dtype=F32) / half)
    ang = jnp.arange(seq).astype(F32)[:, None] * freqs[None, :]
    cos, sin = jnp.cos(ang), jnp.sin(ang)
    pad = jnp.zeros((seq, 128 - 2 * MLA_ROPE), F32)
    cos_t = jnp.concatenate([cos, cos, cos, cos, pad], axis=1)
    sin_t = jnp.concatenate([-sin, sin, -sin, sin, pad], axis=1)
    return cos_t, sin_t


def kernel(x, mem, ffn1_norm, ffn1_w_gate, ffn1_w_up, ffn1_w_down, mix_norm, w_in, gate_bias, mla_q_norm, mla_w_uq, mla_kv_norm, mla_w_ukv, w_branch_a, w_branch_b, w_branch_c, w_mix_out, rel_bias, xattn_norm, mem_norm, xattn_w_q, xattn_w_kv, xattn_w_o, ffn2_norm, ffn2_w_gate, ffn2_w_up, ffn2_w_down, final_norm):
    b, s, d = x.shape
    t = b * s
    depth = w_in.shape[0]
    cos_t, sin_t = _rope_tables(s)
    biases = [_band_bias(rel_bias, g, dil) for g, (_, dil) in enumerate(DIL_GROUPS)]
    row = lambda v: v.reshape(1, -1)
    bf = lambda w: w.astype(BF)
    fg = row(final_norm)

    h = x.reshape(t, d)
    for l in range(depth):
        h = _ffn(h, row(ffn1_norm[l]), bf(ffn1_w_gate[l]), bf(ffn1_w_up[l]), bf(ffn1_w_down[l]), fg, False)

        w_a, w_rope, w_d, w_g = _pack_w_in(w_in[l])
        g_mix = row(mix_norm[l])
        h3 = h.reshape(b, s, d)
        tm_a = min(MLA_BLOCK, s)
        sb, qp, kp, vt = _proj_a(h, g_mix, w_a, w_rope, cos_t, sin_t, row(mla_q_norm[l]), row(mla_kv_norm[l]),
                                 _pack_w_uq(mla_w_uq[l]), _pack_w_ukv(mla_w_ukv[l]), s, tm_a)
        o_a = _sb_attention(sb.reshape(b, s, sb.shape[1]))
        o_b = _mla_attention(qp.reshape(b, s, qp.shape[1]), kp.reshape(b, s, kp.shape[1]),
                             vt.reshape(b, s // tm_a, vt.shape[1], tm_a))
        ocs, lses = [], []
        for g, (_, dil) in enumerate(DIL_GROUPS):
            o, lse = _dil_attention(_dil_proj(h3, g_mix, w_d[g], dil), biases[g], dil)
            ocs.append(o)
            lses.append(lse)
        h3 = _merge(h3, o_a, o_b, ocs, lses, g_mix, w_g, gate_bias[l],
                    bf(w_branch_a[l]), bf(w_branch_b[l]), bf(w_branch_c[l]), bf(w_mix_out[l]))

        kv = _mem_kv(mem, row(mem_norm[l]), bf(xattn_w_kv[l]))
        h = _xattn(h3, row(xattn_norm[l]), kv, bf(xattn_w_q[l]), bf(xattn_w_o[l])).reshape(t, d)

        h = _ffn(h, row(ffn2_norm[l]), bf(ffn2_w_gate[l]), bf(ffn2_w_up[l]), bf(ffn2_w_down[l]), fg,
                 l == depth - 1)
    return h.reshape(b, s, d)
```

```python
import functools
import math

import jax
import jax.numpy as jnp
from jax import lax
from jax.experimental import pallas as pl
from jax.experimental.pallas import tpu as pltpu

BF = jnp.bfloat16
F32 = jnp.float32

EPS = 1e-6
NEG = -1e30
FFN_RESIDUAL = 0.5

SB_WIDTH = 512
MLA_HEADS = 8
MLA_Q_RANK = 384
MLA_KV_RANK = 128
MLA_NOPE = 64
MLA_ROPE = 32
MLA_V = 64
ROPE_THETA = 10000.0
DIL_GROUPS = ((128, 1), (512, 4), (2048, 16))
DIL_WIDTH = 512
REL_BUCKETS = 32
REL_MAX_DIST = 2048
X_HEADS = 4
X_HEAD_DIM = 128
HEAD_DIM = 64
BAND = 128
LANES = 128
MLA_PAIR = 2 * LANES

COL_CQ = 3 * SB_WIDTH
COL_CKV = COL_CQ + MLA_Q_RANK
COL_KR = COL_CKV + MLA_KV_RANK
COL_DIL = COL_KR + MLA_ROPE
SB_DEAD_LOG2 = -150.0
LOG2E = math.log2(math.e)
LN2 = math.log(2.0)

V7X_VMEM_LIMIT = 48 * 1024 * 1024
FF_CHUNK = 256
MLA_BLOCK = 512
DIL_RES_PER_STEP = 4
QSCALE = HEAD_DIM ** -0.5 * LOG2E


def _cparams(sem):
    return pltpu.CompilerParams(dimension_semantics=sem, vmem_limit_bytes=V7X_VMEM_LIMIT)


def _rms(x, g):
    ms = jnp.mean(x * x, axis=-1, keepdims=True)
    return x * lax.rsqrt(ms + EPS) * g


def _sigmoid(x):
    return 0.5 * jnp.tanh(0.5 * x) + 0.5


def _dot(a, b):
    return jnp.dot(a, b, preferred_element_type=F32)


def _dot_t(a, b):
    return lax.dot_general(a, b, (((1,), (1,)), ((), ())), preferred_element_type=F32)


def _resident(shape):
    nd = len(shape)
    return pl.BlockSpec(shape, lambda *_: (0,) * nd, pipeline_mode=pl.Buffered(1))


def _ffn_body(h_ref, g_ref, wg_ref, wu_ref, wd_ref, fg_ref, o_ref, act_ref, *, final):
    x = h_ref[...]
    u = _rms(x, g_ref[...]).astype(BF)
    d_ff = wg_ref.shape[1]
    for c in range(d_ff // FF_CHUNK):
        sl = slice(c * FF_CHUNK, (c + 1) * FF_CHUNK)
        a = _dot(u, wg_ref[:, sl])
        b = _dot(u, wu_ref[:, sl])
        act_ref[:, sl] = (a * jax.nn.sigmoid(a) * b).astype(BF)
    y = x + FFN_RESIDUAL * _dot(act_ref[...], wd_ref[...])
    if final:
        y = _rms(y, fg_ref[...])
    o_ref[...] = y


def _ffn(h, g, wg, wu, wd, fg, final, tm=512):
    t, d = h.shape
    d_ff = wg.shape[1]
    tm = min(tm, t)
    return pl.pallas_call(
        functools.partial(_ffn_body, final=final),
        grid=(t // tm,),
        in_specs=[
            pl.BlockSpec((tm, d), lambda i: (i, 0)),
            _resident((1, d)),
            _resident((d, d_ff)),
            _resident((d, d_ff)),
            _resident((d_ff, d)),
            _resident((1, d)),
        ],
        out_specs=pl.BlockSpec((tm, d), lambda i: (i, 0)),
        out_shape=jax.ShapeDtypeStruct((t, d), F32),
        scratch_shapes=[pltpu.VMEM((tm, d_ff), BF)],
        compiler_params=_cparams(("parallel",)),
        name="ffn",
    )(h, g, wg, wu, wd, fg)


def _proj_a_body(h_ref, g_ref, w_ref, wr_ref, wd_ref, cos_ref, sin_ref, qg_ref, kvg_ref, wq_ref, wkv_ref,
                 sb_ref, d1_ref, qp_ref, kp_ref, vt_ref, *, scale):
    u = _rms(h_ref[...], g_ref[...]).astype(BF)
    for wt_ref, out_ref in ((w_ref, sb_ref), (wd_ref, d1_ref)):
        for c, factor in enumerate((QSCALE, None, None)):
            sl = slice(c * SB_WIDTH, (c + 1) * SB_WIDTH)
            y = _dot(u, wt_ref[:, sl])
            out_ref[:, sl] = (y if factor is None else y * factor).astype(out_ref.dtype)
    lat = _dot(u, w_ref[:, COL_CQ:COL_KR])
    kr = _dot(u, wr_ref[...])
    c_kv0 = COL_CKV - COL_CQ
    nq = _rms(lat[:, :c_kv0], qg_ref[...]).astype(BF)
    q = _dot(nq, wq_ref[...])
    nkv = _rms(lat[:, c_kv0:], kvg_ref[...]).astype(BF)
    kv = _dot(nkv, wkv_ref[...])
    cos = cos_ref[...]
    sin = sin_ref[...]
    k_pe = (kr[:, :LANES] * cos + kr[:, LANES:] * sin).astype(BF)
    width = q.shape[1] // 3
    for p in range(width // LANES):
        lo, hi = p * LANES, (p + 1) * LANES
        qp_ref[:, 2 * lo:2 * lo + LANES] = (q[:, lo:hi] * scale).astype(BF)
        q_pe = q[:, width + lo:width + hi] * cos + q[:, 2 * width + lo:2 * width + hi] * sin
        qp_ref[:, 2 * lo + LANES:2 * hi] = (q_pe * scale).astype(BF)
        kp_ref[:, 2 * lo:2 * lo + LANES] = kv[:, lo:hi].astype(BF)
        kp_ref[:, 2 * lo + LANES:2 * hi] = k_pe
    vt_ref[0] = kv[:, width:].T.astype(BF)


def _proj_a(h, g, w, w_rope, w_dil1, cos, sin, qg, kvg, wq, wkv, seq, tm):
    t, d = h.shape
    width = MLA_HEADS * MLA_NOPE
    scale = (MLA_NOPE + MLA_ROPE) ** -0.5 * LOG2E
    per_seq = seq // tm
    params = (g, w, w_rope, w_dil1)
    tables = [pl.BlockSpec((tm, LANES), lambda i: (i % per_seq, 0))] * 2
    mla_params = (qg, kvg, wq, wkv)
    row = lambda n: pl.BlockSpec((tm, n), lambda i: (i, 0))
    return pl.pallas_call(
        functools.partial(_proj_a_body, scale=scale),
        grid=(t // tm,),
        in_specs=[row(d)] + [_resident(p.shape) for p in params] + tables + [_resident(p.shape) for p in mla_params],
        out_specs=[row(COL_CQ), row(3 * DIL_WIDTH), row(2 * width), row(2 * width),
                   pl.BlockSpec((1, width, tm), lambda i: (i, 0, 0))],
        out_shape=[jax.ShapeDtypeStruct((t, COL_CQ), BF), jax.ShapeDtypeStruct((t, 3 * DIL_WIDTH), BF),
                   jax.ShapeDtypeStruct((t, 2 * width), BF), jax.ShapeDtypeStruct((t, 2 * width), BF),
                   jax.ShapeDtypeStruct((t // tm, width, tm), BF)],
        compiler_params=_cparams(("parallel",)),
        name="proj_a",
    )(h, g, w, w_rope, w_dil1, cos, sin, qg, kvg, wq, wkv)


def _dil_proj_body(h_ref, g_ref, w_ref, o_ref, stage, *, r):
    rows = h_ref.shape[1] // r
    n = w_ref.shape[1]
    u = _rms(h_ref[0], g_ref[...])
    chunks = u.shape[1] // LANES
    for c in range(chunks):
        stage[c] = u[:, c * LANES:(c + 1) * LANES]
    for res in range(r):
        ur = jnp.concatenate([stage[c, pl.ds(res, rows, stride=r), :] for c in range(chunks)], axis=1).astype(BF)
        for c, factor in enumerate((QSCALE, None, None)):
            y = _dot(ur, w_ref[:, c * DIL_WIDTH:(c + 1) * DIL_WIDTH])
            lo = res * n + c * DIL_WIDTH
            o_ref[0, :, lo:lo + DIL_WIDTH] = (y if factor is None else y * factor).astype(o_ref.dtype)


def _dil_proj(h3, g, w, r):
    b, s, d = h3.shape
    n = w.shape[1]
    tm = min(s, max(1024, BAND * r))
    return pl.pallas_call(
        functools.partial(_dil_proj_body, r=r),
        grid=(b, s // tm),
        in_specs=[
            pl.BlockSpec((1, tm, d), lambda bi, i: (bi, i, 0)),
            pl.BlockSpec((1, d), lambda bi, i: (0, 0)),
            pl.BlockSpec((d, n), lambda bi, i: (0, 0)),
        ],
        out_specs=pl.BlockSpec((1, tm // r, r * n), lambda bi, i: (bi, i, 0)),
        out_shape=jax.ShapeDtypeStruct((b, s // r, r * n), BF),
        scratch_shapes=[pltpu.VMEM((d // LANES, tm, LANES), F32)],
        compiler_params=_cparams(("parallel", "parallel")),
        name=f"dil_proj_r{r}",
    )(h3, g, w)


def _sb_body(q_ref, k_ref, v_ref, o_ref, *, tq, pairs):
    i = pl.program_id(2)
    pw = 2 * HEAD_DIM
    lane = lax.broadcasted_iota(jnp.int32, (1, pw), 1)
    first = lane < HEAD_DIM
    q_heads = []
    for p in range(pairs):
        q = q_ref[0, :, p * pw:(p + 1) * pw]
        zero = jnp.zeros_like(q)
        q_heads += [jnp.where(first, q, zero), jnp.where(first, zero, q)]
    row = lax.broadcasted_iota(jnp.int32, (tq, tq), 0)
    col = lax.broadcasted_iota(jnp.int32, (tq, tq), 1)
    strict = col < row
    later = jnp.where(row > col, 1.0, 0.0).astype(BF)

    def block(j, accs, rs, diag):
        start = pl.multiple_of(j * tq, tq)
        heads = range(2 * pairs)
        ks = [k_ref[0, pl.ds(start, tq), p * pw:(p + 1) * pw] for p in range(pairs)]
        vs = [v_ref[0, pl.ds(start, tq), p * pw:(p + 1) * pw] for p in range(pairs)]
        zs = [_dot_t(q_heads[h], ks[h // 2]) for h in heads]
        log_betas, log_keeps = [], []
        for z in zs:
            sp = jnp.log(1.0 + jnp.exp2(-jnp.abs(z))) * LOG2E
            log_beta = jnp.minimum(z, 0.0) - sp
            log_keep = log_beta - z
            if diag:
                log_keep = jnp.where(strict, log_keep, 0.0)
            log_betas.append(log_beta)
            log_keeps.append(log_keep)
        afters = [_dot(log_keeps[h].astype(BF), later) for h in heads]
        ws = []
        for h in heads:
            w = jnp.exp2(log_betas[h] + afters[h] + rs[h])
            if diag:
                w = jnp.where(strict, w, 0.0)
            ws.append(w.astype(BF))
        outs = [_dot(ws[h], vs[h // 2]) for h in heads]
        new_rs = [rs[h] + jnp.sum(log_keeps[h], axis=-1, keepdims=True) for h in heads]
        new_accs = [accs[p] + jnp.where(first, outs[2 * p], outs[2 * p + 1]) for p in range(pairs)]
        return tuple(new_accs), tuple(new_rs)

    def remaining(rs):
        return jnp.max(functools.reduce(jnp.maximum, rs))

    def cond(c):
        return (c[0] < i) & (c[1] > SB_DEAD_LOG2)

    def body(c):
        accs, rs = block(i - 1 - c[0], c[2], c[3], False)
        return c[0] + 1, remaining(rs), accs, rs

    accs = tuple(jnp.zeros((tq, pw), F32) for _ in range(pairs))
    rs = tuple(jnp.zeros((tq, 1), F32) for _ in range(2 * pairs))
    accs, rs = block(i, accs, rs, True)
    out = lax.while_loop(cond, body, (jnp.int32(0), remaining(rs), accs, rs))
    for p in range(pairs):
        o_ref[0, :, p * pw:(p + 1) * pw] = out[2][p].astype(o_ref.dtype)


def _sb_attention(a3, tq=256, pairs=4):
    b, s, _ = a3.shape
    tq = min(tq, s)
    w = 2 * HEAD_DIM * pairs
    groups = SB_WIDTH // w
    return pl.pallas_call(
        functools.partial(_sb_body, tq=tq, pairs=pairs),
        grid=(b, groups, s // tq),
        in_specs=[
            pl.BlockSpec((1, tq, w), lambda bi, p, i: (bi, i, p)),
            pl.BlockSpec((1, s, w), lambda bi, p, i: (bi, 0, groups + p)),
            pl.BlockSpec((1, s, w), lambda bi, p, i: (bi, 0, 2 * groups + p)),
        ],
        out_specs=pl.BlockSpec((1, tq, w), lambda bi, p, i: (bi, i, p)),
        out_shape=jax.ShapeDtypeStruct((b, s, SB_WIDTH), BF),
        compiler_params=_cparams(("parallel", "parallel", "arbitrary")),
        name="sb_attn",
    )(a3, a3, a3)


def _mla_body(q_ref, k_ref, v_ref, o_ref, *, tq, pairs):
    i = pl.program_id(2)
    lane = lax.broadcasted_iota(jnp.int32, (1, MLA_PAIR), 1)
    sel0 = (lane < 64) | ((lane >= 128) & (lane < 160))
    sel1 = ((lane >= 64) & (lane < 128)) | ((lane >= 160) & (lane < 192))
    q_heads = []
    for p in range(pairs):
        q = q_ref[0, :, p * MLA_PAIR:(p + 1) * MLA_PAIR]
        zero = jnp.zeros_like(q)
        q_heads += [jnp.where(sel0, q, zero), jnp.where(sel1, q, zero)]
    key = lax.broadcasted_iota(jnp.int32, (tq, tq), 0)
    qry = lax.broadcasted_iota(jnp.int32, (tq, tq), 1)
    causal = key <= qry
    first = lax.broadcasted_iota(jnp.int32, (128, 1), 0) < 64

    def block(j, carry, diag, nblk=1):
        start = pl.multiple_of(j * tq, tq)
        heads = range(2 * pairs)
        ks = [k_ref[0, pl.ds(start, nblk * tq), p * MLA_PAIR:(p + 1) * MLA_PAIR] for p in range(pairs)]
        v_heads = []
        for p in range(pairs):
            vt = jnp.concatenate([v_ref[0, j + n, p * 128:(p + 1) * 128, :] for n in range(nblk)], axis=1)
            ones = jnp.ones_like(vt)
            v_heads += [jnp.where(first, vt, ones), jnp.where(first, ones, vt)]
        ss = [_dot_t(ks[h // 2], q_heads[h]) for h in heads]
        if diag:
            ss = [jnp.where(causal, s, NEG) for s in ss]
        m_news = [jnp.maximum(carry[h][0], jnp.max(ss[h], axis=0, keepdims=True)) for h in heads]
        prs = [jnp.exp2(ss[h] - m_news[h]).astype(BF) for h in heads]
        pvs = [_dot(v_heads[h], prs[h]) for h in heads]
        return tuple((m_news[h], jnp.exp2(carry[h][0] - m_news[h]) * carry[h][1] + pvs[h]) for h in heads)

    one = (jnp.full((1, tq), NEG, F32), jnp.zeros((128, tq), F32))
    carry = lax.fori_loop(0, i // 2, lambda t, c: block(2 * t, c, False, 2), (one,) * (2 * pairs))
    carry = lax.cond(i % 2 == 1, lambda c: block(i - 1, c, False), lambda c: c, carry)
    carry = block(i, carry, True)
    for p in range(pairs):
        a0, a1 = carry[2 * p][1], carry[2 * p + 1][1]
        lsw = jnp.where(first, a1, a0)
        l = jnp.concatenate([lsw[64:], lsw[:64]], axis=0)
        out_t = jnp.where(first, a0, a1) / l
        o_ref[0, :, p * 128:(p + 1) * 128] = out_t.T.astype(o_ref.dtype)


def _mla_attention(qp, kp, vt, pairs=2):
    b, s, _ = qp.shape
    nblk, width, tq = vt.shape[1:]
    groups = MLA_HEADS // (2 * pairs)
    return pl.pallas_call(
        functools.partial(_mla_body, tq=tq, pairs=pairs),
        grid=(b, groups, s // tq),
        in_specs=[
            pl.BlockSpec((1, tq, MLA_PAIR * pairs), lambda bi, p, i: (bi, i, p)),
            pl.BlockSpec((1, s, MLA_PAIR * pairs), lambda bi, p, i: (bi, 0, p)),
            pl.BlockSpec((1, nblk, 128 * pairs, tq), lambda bi, p, i: (bi, 0, p, 0)),
        ],
        out_specs=pl.BlockSpec((1, tq, 128 * pairs), lambda bi, p, i: (bi, i, p)),
        out_shape=jax.ShapeDtypeStruct((b, s, MLA_HEADS * MLA_V), BF),
        compiler_params=_cparams(("parallel", "parallel", "arbitrary")),
        name="mla_attn",
    )(qp, kp, vt)


def _dil_body(cur_ref, prev_ref, bias_ref, o_ref, lse_ref, *, tl, nres):
    j = pl.program_id(2)
    lane = lax.broadcasted_iota(jnp.int32, (1, 2 * HEAD_DIM), 1)
    first = lane < HEAD_DIM
    kj = lax.broadcasted_iota(jnp.int32, (BAND, 2 * BAND), 1)
    has_prev = (kj >= BAND) | (j > 0)
    pw = 2 * HEAD_DIM
    pairs = DIL_WIDTH // pw
    heads = range(2 * pairs)
    for rr, sb in [(rr, sb) for rr in range(nres) for sb in range(tl // BAND)]:
        rows = slice(sb * BAND, (sb + 1) * BAND)
        prev = slice((sb - 1) * BAND, sb * BAND)
        q_cols, k_cols, v_cols = (slice((3 * rr + c) * DIL_WIDTH, (3 * rr + c + 1) * DIL_WIDTH) for c in range(3))
        out_base = rr * DIL_WIDTH
        k_prev = prev_ref[0, :, k_cols] if sb == 0 else cur_ref[0, prev, k_cols]
        v_prev = prev_ref[0, :, v_cols] if sb == 0 else cur_ref[0, prev, v_cols]
        kcat = jnp.concatenate([k_prev, cur_ref[0, rows, k_cols]], axis=0)
        vcat = jnp.concatenate([v_prev, cur_ref[0, rows, v_cols]], axis=0)
        qs = cur_ref[0, rows, q_cols]
        q_heads, v_heads = [], []
        for p in range(pairs):
            qb = qs[:, p * pw:(p + 1) * pw]
            vb = vcat[:, p * pw:(p + 1) * pw]
            zero, ones = jnp.zeros_like(qb), jnp.ones_like(vb)
            q_heads += [jnp.where(first, qb, zero), jnp.where(first, zero, qb)]
            v_heads += [jnp.where(first, vb, ones), jnp.where(first, ones, vb)]
        ss = [_dot_t(q_heads[h], kcat[:, (h // 2) * pw:(h // 2 + 1) * pw]) + bias_ref[h] for h in heads]
        if sb == 0:
            ss = [jnp.where(has_prev, s, NEG) for s in ss]
        ms = [jnp.max(s, axis=-1, keepdims=True) for s in ss]
        prs = [jnp.exp2(ss[h] - ms[h]).astype(BF) for h in heads]
        pvs = [_dot(prs[h], v_heads[h]) for h in heads]
        for p in range(pairs):
            a0, a1 = pvs[2 * p], pvs[2 * p + 1]
            cols = slice(out_base + p * pw, out_base + (p + 1) * pw)
            l = pltpu.roll(jnp.where(first, a1, a0), HEAD_DIM, 1)
            o_ref[0, rows, cols] = (jnp.where(first, a0, a1) / l).astype(o_ref.dtype)
            lse_ref[0, rows, cols] = jnp.where(first, ms[2 * p], ms[2 * p + 1]) * LN2 + jnp.log(l)


def _dil_attention(view, bias, dilation, tl=512):
    r = dilation
    batch, length, _ = view.shape
    tl = min(tl, length)
    sub = tl // BAND
    nres = min(r, DIL_RES_PER_STEP)
    wide = nres * 3 * DIL_WIDTH
    cur = pl.BlockSpec((1, tl, wide), lambda bi, rg, j: (bi, j, rg))
    prev = pl.BlockSpec((1, BAND, wide), lambda bi, rg, j: (bi, jnp.maximum(j * sub - 1, 0), rg))
    out_spec = pl.BlockSpec((1, tl, nres * DIL_WIDTH), lambda bi, rg, j: (bi, j, rg))
    return pl.pallas_call(
        functools.partial(_dil_body, tl=tl, nres=nres),
        grid=(batch, r // nres, length // tl),
        in_specs=[cur, prev, pl.BlockSpec(bias.shape, lambda bi, rg, j: (0, 0, 0))],
        out_specs=[out_spec, out_spec],
        out_shape=[jax.ShapeDtypeStruct((batch, length, r * DIL_WIDTH), BF),
                   jax.ShapeDtypeStruct((batch, length, r * DIL_WIDTH), F32)],
        compiler_params=_cparams(("parallel", "parallel", "arbitrary")),
        name=f"dil_attn_r{r}",
    )(view, view, bias)


def _merge_body(h_ref, oa_ref, ob_ref, oc0_ref, oc1_ref, oc2_ref, l0_ref, l1_ref, l2_ref,
                gn_ref, wg_ref, gbias_ref, wa_ref, wb_ref, wc_ref, wo_ref, o_ref, *scratch):
    tm, d = h_ref.shape[1:]
    scratch = list(scratch)
    h = h_ref[0]
    u = _rms(h, gn_ref[...]).astype(BF)

    def gate(c):
        return _sigmoid(_dot(u, wg_ref[:, c * d:(c + 1) * d]) + gbias_ref[c:c + 1, :])

    def token_major(ref, r):
        if r == 1:
            return ref[0].astype(F32)
        buf = scratch.pop(0)
        chunks = DIL_WIDTH // LANES
        for res in range(r):
            for c in range(chunks):
                lo = res * DIL_WIDTH + c * LANES
                buf[c, pl.ds(res, tm // r, stride=r), :] = ref[0, :, lo:lo + LANES].astype(F32)
        return jnp.concatenate([buf[c] for c in range(chunks)], axis=1)

    dils = [dil for _, dil in DIL_GROUPS]
    merged = gate(0) * _dot(oa_ref[0], wa_ref[...]) + gate(1) * _dot(ob_ref[0], wb_ref[...])
    gate_c = gate(2)
    l0, l1, l2 = (token_major(ref, r) for ref, r in zip((l0_ref, l1_ref, l2_ref), dils))
    mx = jnp.maximum(jnp.maximum(l0, l1), l2)
    e0, e1, e2 = jnp.exp(l0 - mx), jnp.exp(l1 - mx), jnp.exp(l2 - mx)
    o0, o1, o2 = (token_major(ref, r) for ref, r in zip((oc0_ref, oc1_ref, oc2_ref), dils))
    oc = (e0 * o0 + e1 * o1 + e2 * o2) / (e0 + e1 + e2)
    merged = merged + gate_c * _dot(oc.astype(BF), wc_ref[...])
    o_ref[0] = h + _dot(merged.astype(BF), wo_ref[...])


def _merge(h3, oa, ob, ocs, lses, g_mix, w_gate, gbias, wa, wb, wc, wo, tm=512):
    b, s, d = h3.shape
    tm = min(tm, s)
    row = lambda n: pl.BlockSpec((1, tm, n), lambda bi, i: (bi, i, 0))
    views = [pl.BlockSpec((1, tm // r, r * DIL_WIDTH), lambda bi, i: (bi, i, 0)) for _, r in DIL_GROUPS]
    n_buf = 2 * sum(1 for _, r in DIL_GROUPS if r > 1)
    params = (g_mix, w_gate, gbias, wa, wb, wc, wo)
    return pl.pallas_call(
        _merge_body,
        grid=(b, s // tm),
        in_specs=[row(d), row(SB_WIDTH), row(SB_WIDTH)] + views + views + [_resident(p.shape) for p in params],
        out_specs=row(d),
        out_shape=jax.ShapeDtypeStruct((b, s, d), F32),
        scratch_shapes=[pltpu.VMEM((DIL_WIDTH // LANES, tm, LANES), F32)] * n_buf,
        compiler_params=_cparams(("parallel", "parallel")),
        name="merge",
    )(h3, oa, ob, *ocs, *lses, *params)


def _memkv_body(m_ref, g_ref, w_ref, o_ref):
    u = _rms(m_ref[0], g_ref[...]).astype(BF)
    o_ref[0] = _dot(u, w_ref[...]).astype(o_ref.dtype)


def _mem_kv(mem, g, w):
    b, m, d = mem.shape
    n = w.shape[1]
    return pl.pallas_call(
        _memkv_body,
        grid=(b,),
        in_specs=[pl.BlockSpec((1, m, d), lambda i: (i, 0, 0)),
                  pl.BlockSpec((1, d), lambda i: (0, 0)),
                  pl.BlockSpec((d, n), lambda i: (0, 0))],
        out_specs=pl.BlockSpec((1, m, n), lambda i: (i, 0, 0)),
        out_shape=jax.ShapeDtypeStruct((b, m, n), BF),
        compiler_params=_cparams(("parallel",)),
        name="mem_kv",
    )(mem, g, w)


def _xattn_body(h_ref, g_ref, kv_ref, wq_ref, wo_ref, o_ref):
    x = h_ref[0]
    u = _rms(x, g_ref[...]).astype(BF)
    q = (_dot(u, wq_ref[...]) * (X_HEAD_DIM ** -0.5 * LOG2E)).astype(BF)
    kv = kv_ref[0]
    width = X_HEADS * X_HEAD_DIM
    heads = range(X_HEADS)
    cols = [slice(h * X_HEAD_DIM, (h + 1) * X_HEAD_DIM) for h in heads]
    ss = [_dot_t(q[:, cols[h]], kv[:, cols[h]]) for h in heads]
    es = [jnp.exp2(s - jnp.max(s, axis=-1, keepdims=True)) for s in ss]
    ps = [(e / jnp.sum(e, axis=-1, keepdims=True)).astype(BF) for e in es]
    outs = [_dot(ps[h], kv[:, width + h * X_HEAD_DIM:width + (h + 1) * X_HEAD_DIM]) for h in heads]
    o = jnp.concatenate(outs, axis=-1).astype(BF)
    o_ref[0] = x + _dot(o, wo_ref[...])


def _xattn(h3, g, kv, wq, wo, tm=512):
    b, s, d = h3.shape
    tm = min(tm, s)
    m = kv.shape[1]
    return pl.pallas_call(
        _xattn_body,
        grid=(b, s // tm),
        in_specs=[pl.BlockSpec((1, tm, d), lambda bi, i: (bi, i, 0)),
                  pl.BlockSpec((1, d), lambda bi, i: (0, 0)),
                  pl.BlockSpec((1, m, kv.shape[2]), lambda bi, i: (bi, 0, 0)),
                  pl.BlockSpec(wq.shape, lambda bi, i: (0, 0)),
                  pl.BlockSpec(wo.shape, lambda bi, i: (0, 0))],
        out_specs=pl.BlockSpec((1, tm, d), lambda bi, i: (bi, i, 0)),
        out_shape=jax.ShapeDtypeStruct((b, s, d), F32),
        compiler_params=_cparams(("parallel", "parallel")),
        name="xattn",
    )(h3, g, kv, wq, wo)


def _rel_bucket(dist):
    exact = REL_BUCKETS // 2
    d = jnp.maximum(dist, exact).astype(F32)
    large = exact + (jnp.log(d / exact) / math.log(REL_MAX_DIST / exact)
                     * (REL_BUCKETS - exact)).astype(jnp.int32)
    return jnp.where(dist < exact, dist, jnp.minimum(large, REL_BUCKETS - 1))


def _band_bias(rel_bias, group, dilation):
    heads = DIL_WIDTH // HEAD_DIM
    qi = jnp.arange(BAND)
    kj = jnp.arange(2 * BAND)
    steps = (qi[:, None] + BAND) - kj[None, :]
    table = rel_bias[:, group * heads:(group + 1) * heads].astype(F32)
    bucket = _rel_bucket(jnp.clip(steps, 0, BAND) * dilation)
    onehot = (bucket[None] == jnp.arange(REL_BUCKETS)[:, None, None]).astype(F32)
    bias = jnp.einsum('bh,bqk->hqk', table * LOG2E, onehot, precision=lax.Precision.HIGHEST)
    return jnp.where((steps >= 0) & (steps <= BAND), bias, NEG)


def _swap_halves(w, width):
    k, n = w.shape
    w = w.reshape(k, n // width, 2, width // 2)
    return w[:, :, ::-1, :].reshape(k, n)


def _pack_w_in(w_in):
    d = w_in.shape[0]
    seg_a = w_in[:, :COL_KR].astype(BF)
    kr = w_in[:, COL_KR:COL_DIL]
    zeros = jnp.zeros((d, LANES - 2 * MLA_ROPE), w_in.dtype)
    kr_sw = _swap_halves(kr, MLA_ROPE)
    w_rope = jnp.concatenate([kr, kr, zeros, kr_sw, kr_sw, zeros], axis=1).astype(BF)
    group = 3 * DIL_WIDTH
    seg_d = [w_in[:, COL_DIL + g * group:COL_DIL + (g + 1) * group].astype(BF) for g in range(len(DIL_GROUPS))]
    seg_g = w_in[:, COL_DIL + len(DIL_GROUPS) * group:].astype(BF)
    return seg_a, w_rope, seg_d, seg_g


def _pack_w_uq(w_uq):
    k = w_uq.shape[0]
    w = w_uq.reshape(k, MLA_HEADS, MLA_NOPE + MLA_ROPE)
    nope = w[:, :, :MLA_NOPE].reshape(k, MLA_HEADS * MLA_NOPE)
    rope = w[:, :, MLA_NOPE:]

    def pair_layout(rp):
        rp = rp.reshape(k, MLA_HEADS // 2, 2 * MLA_ROPE)
        pad = jnp.zeros((k, MLA_HEADS // 2, 128 - 2 * MLA_ROPE), rp.dtype)
        return jnp.concatenate([rp, pad], axis=-1).reshape(k, (MLA_HEADS // 2) * 128)

    rope_sw = rope.reshape(k, MLA_HEADS, 2, MLA_ROPE // 2)[:, :, ::-1, :].reshape(k, MLA_HEADS, MLA_ROPE)
    return jnp.concatenate([nope, pair_layout(rope), pair_layout(rope_sw)], axis=1).astype(BF)


def _pack_w_ukv(w_ukv):
    k = w_ukv.shape[0]
    w = w_ukv.reshape(k, MLA_HEADS, MLA_NOPE + MLA_V)
    return jnp.concatenate([w[:, :, :MLA_NOPE].reshape(k, -1), w[:, :, MLA_NOPE:].reshape(k, -1)],
                           axis=1).astype(BF)


def _rope_tables(seq):
    half = MLA_ROPE // 2
    freqs = ROPE_THETA ** (-jnp.arange(half, dtype=F32) / half)
    ang = jnp.arange(seq).astype(F32)[:, None] * freqs[None, :]
    cos, sin = jnp.cos(ang), jnp.sin(ang)
    pad = jnp.zeros((seq, 128 - 2 * MLA_ROPE), F32)
    cos_t = jnp.concatenate([cos, cos, cos, cos, pad], axis=1)
    sin_t = jnp.concatenate([-sin, sin, -sin, sin, pad], axis=1)
    return cos_t, sin_t


def kernel(x, mem, ffn1_norm, ffn1_w_gate, ffn1_w_up, ffn1_w_down, mix_norm, w_in, gate_bias, mla_q_norm, mla_w_uq, mla_kv_norm, mla_w_ukv, w_branch_a, w_branch_b, w_branch_c, w_mix_out, rel_bias, xattn_norm, mem_norm, xattn_w_q, xattn_w_kv, xattn_w_o, ffn2_norm, ffn2_w_gate, ffn2_w_up, ffn2_w_down, final_norm):
    b, s, d = x.shape
    t = b * s
    depth = w_in.shape[0]
    cos_t, sin_t = _rope_tables(s)
    biases = [_band_bias(rel_bias, g, dil) for g, (_, dil) in enumerate(DIL_GROUPS)]
    row = lambda v: v.reshape(1, -1)
    bf = lambda w: w.astype(BF)
    fg = row(final_norm)

    h = x.reshape(t, d)
    for l in range(depth):
        h = _ffn(h, row(ffn1_norm[l]), bf(ffn1_w_gate[l]), bf(ffn1_w_up[l]), bf(ffn1_w_down[l]), fg, False)

        w_a, w_rope, w_d, w_g = _pack_w_in(w_in[l])
        g_mix = row(mix_norm[l])
        h3 = h.reshape(b, s, d)
        tm_a = min(MLA_BLOCK, s)
        assert DIL_GROUPS[0][1] == 1
        sb, d1, qp, kp, vt = _proj_a(h, g_mix, w_a, w_rope, w_d[0], cos_t, sin_t, row(mla_q_norm[l]),
                                     row(mla_kv_norm[l]), _pack_w_uq(mla_w_uq[l]), _pack_w_ukv(mla_w_ukv[l]),
                                     s, tm_a)
        o_a = _sb_attention(sb.reshape(b, s, sb.shape[1]))
        o_b = _mla_attention(qp.reshape(b, s, qp.shape[1]), kp.reshape(b, s, kp.shape[1]),
                             vt.reshape(b, s // tm_a, vt.shape[1], tm_a))
        ocs, lses = [], []
        for g, (_, dil) in enumerate(DIL_GROUPS):
            view = d1.reshape(b, s, d1.shape[1]) if g == 0 else _dil_proj(h3, g_mix, w_d[g], dil)
            o, lse = _dil_attention(view, biases[g], dil)
            ocs.append(o)
            lses.append(lse)
        h3 = _merge(h3, o_a, o_b, ocs, lses, g_mix, w_g, gate_bias[l],
                    bf(w_branch_a[l]), bf(w_branch_b[l]), bf(w_branch_c[l]), bf(w_mix_out[l]))

        kv = _mem_kv(mem, row(mem_norm[l]), bf(xattn_w_kv[l]))
        h = _xattn(h3, row(xattn_norm[l]), kv, bf(xattn_w_q[l]), bf(xattn_w_o[l])).reshape(t, d)

        h = _ffn(h, row(ffn2_norm[l]), bf(ffn2_w_gate[l]), bf(ffn2_w_up[l]), bf(ffn2_w_down[l]), fg,
                 l == depth - 1)
    return h.reshape(b, s, d)
```

```python
import functools
import math

import jax
import jax.numpy as jnp
from jax import lax
from jax.experimental import pallas as pl
from jax.experimental.pallas import tpu as pltpu

BF = jnp.bfloat16
F32 = jnp.float32

EPS = 1e-6
NEG = -1e30
FFN_RESIDUAL = 0.5

SB_WIDTH = 512
MLA_HEADS = 8
MLA_Q_RANK = 384
MLA_KV_RANK = 128
MLA_NOPE = 64
MLA_ROPE = 32
MLA_V = 64
ROPE_THETA = 10000.0
DIL_GROUPS = ((128, 1), (512, 4), (2048, 16))
DIL_WIDTH = 512
REL_BUCKETS = 32
REL_MAX_DIST = 2048
X_HEADS = 4
X_HEAD_DIM = 128
HEAD_DIM = 64
BAND = 128
LANES = 128
MLA_PAIR = 2 * LANES

COL_CQ = 3 * SB_WIDTH
COL_CKV = COL_CQ + MLA_Q_RANK
COL_KR = COL_CKV + MLA_KV_RANK
COL_DIL = COL_KR + MLA_ROPE
SB_DEAD_LOG2 = -150.0
LOG2E = math.log2(math.e)
LN2 = math.log(2.0)

V7X_VMEM_LIMIT = 48 * 1024 * 1024
FF_CHUNK = 256
MLA_BLOCK = 512
DIL_RES_PER_STEP = 4
QSCALE = HEAD_DIM ** -0.5 * LOG2E


def _cparams(sem):
    return pltpu.CompilerParams(dimension_semantics=sem, vmem_limit_bytes=V7X_VMEM_LIMIT)


def _rms(x, g):
    ms = jnp.mean(x * x, axis=-1, keepdims=True)
    return x * lax.rsqrt(ms + EPS) * g


def _sigmoid(x):
    return 0.5 * jnp.tanh(0.5 * x) + 0.5


def _dot(a, b):
    return jnp.dot(a, b, preferred_element_type=F32)


def _dot_t(a, b):
    return lax.dot_general(a, b, (((1,), (1,)), ((), ())), preferred_element_type=F32)


def _resident(shape):
    nd = len(shape)
    return pl.BlockSpec(shape, lambda *_: (0,) * nd, pipeline_mode=pl.Buffered(1))


def _ffn_body(h_ref, g_ref, wg_ref, wu_ref, wd_ref, fg_ref, o_ref, act_ref, *, final):
    x = h_ref[...]
    u = _rms(x, g_ref[...]).astype(BF)
    d_ff = wg_ref.shape[1]
    for c in range(d_ff // FF_CHUNK):
        sl = slice(c * FF_CHUNK, (c + 1) * FF_CHUNK)
        a = _dot(u, wg_ref[:, sl])
        b = _dot(u, wu_ref[:, sl])
        act_ref[:, sl] = (a * jax.nn.sigmoid(a) * b).astype(BF)
    y = x + FFN_RESIDUAL * _dot(act_ref[...], wd_ref[...])
    if final:
        y = _rms(y, fg_ref[...])
    o_ref[...] = y


def _ffn(h, g, wg, wu, wd, fg, final, tm=512):
    t, d = h.shape
    d_ff = wg.shape[1]
    tm = min(tm, t)
    return pl.pallas_call(
        functools.partial(_ffn_body, final=final),
        grid=(t // tm,),
        in_specs=[
            pl.BlockSpec((tm, d), lambda i: (i, 0)),
            _resident((1, d)),
            _resident((d, d_ff)),
            _resident((d, d_ff)),
            _resident((d_ff, d)),
            _resident((1, d)),
        ],
        out_specs=pl.BlockSpec((tm, d), lambda i: (i, 0)),
        out_shape=jax.ShapeDtypeStruct((t, d), F32),
        scratch_shapes=[pltpu.VMEM((tm, d_ff), BF)],
        compiler_params=_cparams(("parallel",)),
        name="ffn",
    )(h, g, wg, wu, wd, fg)


def _proj_a_body(h_ref, g_ref, w_ref, wr_ref, wd_ref, cos_ref, sin_ref, qg_ref, kvg_ref, wq_ref, wkv_ref,
                 sb_ref, d1_ref, qp_ref, kp_ref, vt_ref, *, scale):
    u = _rms(h_ref[...], g_ref[...]).astype(BF)
    for wt_ref, out_ref in ((w_ref, sb_ref), (wd_ref, d1_ref)):
        for c, factor in enumerate((QSCALE, None, None)):
            sl = slice(c * SB_WIDTH, (c + 1) * SB_WIDTH)
            y = _dot(u, wt_ref[:, sl])
            out_ref[:, sl] = (y if factor is None else y * factor).astype(out_ref.dtype)
    lat = _dot(u, w_ref[:, COL_CQ:COL_KR])
    kr = _dot(u, wr_ref[...])
    c_kv0 = COL_CKV - COL_CQ
    nq = _rms(lat[:, :c_kv0], qg_ref[...]).astype(BF)
    q = _dot(nq, wq_ref[...])
    nkv = _rms(lat[:, c_kv0:], kvg_ref[...]).astype(BF)
    kv = _dot(nkv, wkv_ref[...])
    cos = cos_ref[...]
    sin = sin_ref[...]
    k_pe = (kr[:, :LANES] * cos + kr[:, LANES:] * sin).astype(BF)
    width = q.shape[1] // 3
    for p in range(width // LANES):
        lo, hi = p * LANES, (p + 1) * LANES
        qp_ref[:, 2 * lo:2 * lo + LANES] = (q[:, lo:hi] * scale).astype(BF)
        q_pe = q[:, width + lo:width + hi] * cos + q[:, 2 * width + lo:2 * width + hi] * sin
        qp_ref[:, 2 * lo + LANES:2 * hi] = (q_pe * scale).astype(BF)
        kp_ref[:, 2 * lo:2 * lo + LANES] = kv[:, lo:hi].astype(BF)
        kp_ref[:, 2 * lo + LANES:2 * hi] = k_pe
    vt_ref[0] = kv[:, width:].T.astype(BF)


def _proj_a(h, g, w, w_rope, w_dil1, cos, sin, qg, kvg, wq, wkv, seq, tm):
    t, d = h.shape
    width = MLA_HEADS * MLA_NOPE
    scale = (MLA_NOPE + MLA_ROPE) ** -0.5 * LOG2E
    per_seq = seq // tm
    params = (g, w, w_rope, w_dil1)
    tables = [pl.BlockSpec((tm, LANES), lambda i: (i % per_seq, 0))] * 2
    mla_params = (qg, kvg, wq, wkv)
    row = lambda n: pl.BlockSpec((tm, n), lambda i: (i, 0))
    return pl.pallas_call(
        functools.partial(_proj_a_body, scale=scale),
        grid=(t // tm,),
        in_specs=[row(d)] + [_resident(p.shape) for p in params] + tables + [_resident(p.shape) for p in mla_params],
        out_specs=[row(COL_CQ), row(3 * DIL_WIDTH), row(2 * width), row(2 * width),
                   pl.BlockSpec((1, width, tm), lambda i: (i, 0, 0))],
        out_shape=[jax.ShapeDtypeStruct((t, COL_CQ), BF), jax.ShapeDtypeStruct((t, 3 * DIL_WIDTH), BF),
                   jax.ShapeDtypeStruct((t, 2 * width), BF), jax.ShapeDtypeStruct((t, 2 * width), BF),
                   jax.ShapeDtypeStruct((t // tm, width, tm), BF)],
        compiler_params=_cparams(("parallel",)),
        name="proj_a",
    )(h, g, w, w_rope, w_dil1, cos, sin, qg, kvg, wq, wkv)


def _dil_proj_body(h_ref, g_ref, w_ref, o_ref, stage, *, r):
    rows = h_ref.shape[1] // r
    n = w_ref.shape[1]
    u = _rms(h_ref[0], g_ref[...])
    chunks = u.shape[1] // LANES
    for c in range(chunks):
        stage[c] = u[:, c * LANES:(c + 1) * LANES]
    for res in range(r):
        ur = jnp.concatenate([stage[c, pl.ds(res, rows, stride=r), :] for c in range(chunks)], axis=1).astype(BF)
        for c, factor in enumerate((QSCALE, None, None)):
            y = _dot(ur, w_ref[:, c * DIL_WIDTH:(c + 1) * DIL_WIDTH])
            lo = res * n + c * DIL_WIDTH
            o_ref[0, :, lo:lo + DIL_WIDTH] = (y if factor is None else y * factor).astype(o_ref.dtype)


def _dil_proj(h3, g, w, r):
    b, s, d = h3.shape
    n = w.shape[1]
    tm = min(s, max(1024, BAND * r))
    return pl.pallas_call(
        functools.partial(_dil_proj_body, r=r),
        grid=(b, s // tm),
        in_specs=[
            pl.BlockSpec((1, tm, d), lambda bi, i: (bi, i, 0)),
            pl.BlockSpec((1, d), lambda bi, i: (0, 0)),
            pl.BlockSpec((d, n), lambda bi, i: (0, 0)),
        ],
        out_specs=pl.BlockSpec((1, tm // r, r * n), lambda bi, i: (bi, i, 0)),
        out_shape=jax.ShapeDtypeStruct((b, s // r, r * n), BF),
        scratch_shapes=[pltpu.VMEM((d // LANES, tm, LANES), F32)],
        compiler_params=_cparams(("parallel", "parallel")),
        name=f"dil_proj_r{r}",
    )(h3, g, w)


def _sb_body(q_ref, k_ref, v_ref, o_ref, *, tq, pairs):
    i = pl.program_id(2)
    pw = 2 * HEAD_DIM
    lane = lax.broadcasted_iota(jnp.int32, (1, pw), 1)
    first = lane < HEAD_DIM
    q_heads = []
    for p in range(pairs):
        q = q_ref[0, :, p * pw:(p + 1) * pw]
        zero = jnp.zeros_like(q)
        q_heads += [jnp.where(first, q, zero), jnp.where(first, zero, q)]
    row = lax.broadcasted_iota(jnp.int32, (tq, tq), 0)
    col = lax.broadcasted_iota(jnp.int32, (tq, tq), 1)
    strict = col < row
    later = jnp.where(row > col, 1.0, 0.0).astype(BF)

    def block(j, accs, rs, diag):
        start = pl.multiple_of(j * tq, tq)
        heads = range(2 * pairs)
        ks = [k_ref[0, pl.ds(start, tq), p * pw:(p + 1) * pw] for p in range(pairs)]
        vs = [v_ref[0, pl.ds(start, tq), p * pw:(p + 1) * pw] for p in range(pairs)]
        zs = [_dot_t(q_heads[h], ks[h // 2]) for h in heads]
        log_betas, log_keeps = [], []
        for z in zs:
            sp = jnp.log(1.0 + jnp.exp2(-jnp.abs(z))) * LOG2E
            log_beta = jnp.minimum(z, 0.0) - sp
            log_keep = log_beta - z
            if diag:
                log_keep = jnp.where(strict, log_keep, 0.0)
            log_betas.append(log_beta)
            log_keeps.append(log_keep)
        afters = [_dot(log_keeps[h].astype(BF), later) for h in heads]
        ws = []
        for h in heads:
            w = jnp.exp2(log_betas[h] + afters[h] + rs[h])
            if diag:
                w = jnp.where(strict, w, 0.0)
            ws.append(w.astype(BF))
        outs = [_dot(ws[h], vs[h // 2]) for h in heads]
        new_rs = [rs[h] + jnp.sum(log_keeps[h], axis=-1, keepdims=True) for h in heads]
        new_accs = [accs[p] + jnp.where(first, outs[2 * p], outs[2 * p + 1]) for p in range(pairs)]
        return tuple(new_accs), tuple(new_rs)

    def remaining(rs):
        return jnp.max(functools.reduce(jnp.maximum, rs))

    def cond(c):
        return (c[0] < i) & (c[1] > SB_DEAD_LOG2)

    def body(c):
        accs, rs = block(i - 1 - c[0], c[2], c[3], False)
        return c[0] + 1, remaining(rs), accs, rs

    accs = tuple(jnp.zeros((tq, pw), F32) for _ in range(pairs))
    rs = tuple(jnp.zeros((tq, 1), F32) for _ in range(2 * pairs))
    accs, rs = block(i, accs, rs, True)
    out = lax.while_loop(cond, body, (jnp.int32(0), remaining(rs), accs, rs))
    for p in range(pairs):
        o_ref[0, :, p * pw:(p + 1) * pw] = out[2][p].astype(o_ref.dtype)


def _sb_attention(a3, tq=256, pairs=4):
    b, s, _ = a3.shape
    tq = min(tq, s)
    w = 2 * HEAD_DIM * pairs
    groups = SB_WIDTH // w
    return pl.pallas_call(
        functools.partial(_sb_body, tq=tq, pairs=pairs),
        grid=(b, groups, s // tq),
        in_specs=[
            pl.BlockSpec((1, tq, w), lambda bi, p, i: (bi, i, p)),
            pl.BlockSpec((1, s, w), lambda bi, p, i: (bi, 0, groups + p)),
            pl.BlockSpec((1, s, w), lambda bi, p, i: (bi, 0, 2 * groups + p)),
        ],
        out_specs=pl.BlockSpec((1, tq, w), lambda bi, p, i: (bi, i, p)),
        out_shape=jax.ShapeDtypeStruct((b, s, SB_WIDTH), BF),
        compiler_params=_cparams(("parallel", "parallel", "arbitrary")),
        name="sb_attn",
    )(a3, a3, a3)


def _mla_body(q_ref, k_ref, v_ref, o_ref, *, tq, pairs):
    i = pl.program_id(2)
    lane = lax.broadcasted_iota(jnp.int32, (1, MLA_PAIR), 1)
    sel0 = (lane < 64) | ((lane >= 128) & (lane < 160))
    sel1 = ((lane >= 64) & (lane < 128)) | ((lane >= 160) & (lane < 192))
    q_heads = []
    for p in range(pairs):
        q = q_ref[0, :, p * MLA_PAIR:(p + 1) * MLA_PAIR]
        zero = jnp.zeros_like(q)
        q_heads += [jnp.where(sel0, q, zero), jnp.where(sel1, q, zero)]
    key = lax.broadcasted_iota(jnp.int32, (tq, tq), 0)
    qry = lax.broadcasted_iota(jnp.int32, (tq, tq), 1)
    causal = key <= qry
    first = lax.broadcasted_iota(jnp.int32, (128, 1), 0) < 64

    def block(j, carry, diag, nblk=1):
        start = pl.multiple_of(j * tq, tq)
        heads = range(2 * pairs)
        ks = [k_ref[0, pl.ds(start, nblk * tq), p * MLA_PAIR:(p + 1) * MLA_PAIR] for p in range(pairs)]
        v_heads = []
        for p in range(pairs):
            vt = jnp.concatenate([v_ref[0, j + n, p * 128:(p + 1) * 128, :] for n in range(nblk)], axis=1)
            ones = jnp.ones_like(vt)
            v_heads += [jnp.where(first, vt, ones), jnp.where(first, ones, vt)]
        ss = [_dot_t(ks[h // 2], q_heads[h]) for h in heads]
        if diag:
            ss = [jnp.where(causal, s, NEG) for s in ss]
        m_news = [jnp.maximum(carry[h][0], jnp.max(ss[h], axis=0, keepdims=True)) for h in heads]
        prs = [jnp.exp2(ss[h] - m_news[h]).astype(BF) for h in heads]
        pvs = [_dot(v_heads[h], prs[h]) for h in heads]
        return tuple((m_news[h], jnp.exp2(carry[h][0] - m_news[h]) * carry[h][1] + pvs[h]) for h in heads)

    one = (jnp.full((1, tq), NEG, F32), jnp.zeros((128, tq), F32))
    carry = lax.fori_loop(0, i // 2, lambda t, c: block(2 * t, c, False, 2), (one,) * (2 * pairs))
    carry = lax.cond(i % 2 == 1, lambda c: block(i - 1, c, False), lambda c: c, carry)
    carry = block(i, carry, True)
    for p in range(pairs):
        a0, a1 = carry[2 * p][1], carry[2 * p + 1][1]
        lsw = jnp.where(first, a1, a0)
        l = jnp.concatenate([lsw[64:], lsw[:64]], axis=0)
        out_t = jnp.where(first, a0, a1) / l
        o_ref[0, :, p * 128:(p + 1) * 128] = out_t.T.astype(o_ref.dtype)


def _mla_attention(qp, kp, vt, pairs=2):
    b, s, _ = qp.shape
    nblk, width, tq = vt.shape[1:]
    groups = MLA_HEADS // (2 * pairs)
    return pl.pallas_call(
        functools.partial(_mla_body, tq=tq, pairs=pairs),
        grid=(b, groups, s // tq),
        in_specs=[
            pl.BlockSpec((1, tq, MLA_PAIR * pairs), lambda bi, p, i: (bi, i, p)),
            pl.BlockSpec((1, s, MLA_PAIR * pairs), lambda bi, p, i: (bi, 0, p)),
            pl.BlockSpec((1, nblk, 128 * pairs, tq), lambda bi, p, i: (bi, 0, p, 0)),
        ],
        out_specs=pl.BlockSpec((1, tq, 128 * pairs), lambda bi, p, i: (bi, i, p)),
        out_shape=jax.ShapeDtypeStruct((b, s, MLA_HEADS * MLA_V), BF),
        compiler_params=_cparams(("parallel", "parallel", "arbitrary")),
        name="mla_attn",
    )(qp, kp, vt)


def _dil_body(cur_ref, prev_ref, bias_ref, o_ref, lse_ref, *, tl, nres):
    j = pl.program_id(2)
    lane = lax.broadcasted_iota(jnp.int32, (1, 2 * HEAD_DIM), 1)
    first = lane < HEAD_DIM
    kj = lax.broadcasted_iota(jnp.int32, (BAND, 2 * BAND), 1)
    has_prev = (kj >= BAND) | (j > 0)
    pw = 2 * HEAD_DIM
    pairs = DIL_WIDTH // pw
    heads = range(2 * pairs)
    for rr, sb in [(rr, sb) for rr in range(nres) for sb in range(tl // BAND)]:
        rows = slice(sb * BAND, (sb + 1) * BAND)
        prev = slice((sb - 1) * BAND, sb * BAND)
        q_cols, k_cols, v_cols = (slice((3 * rr + c) * DIL_WIDTH, (3 * rr + c + 1) * DIL_WIDTH) for c in range(3))
        out_base = rr * DIL_WIDTH
        k_prev = prev_ref[0, :, k_cols] if sb == 0 else cur_ref[0, prev, k_cols]
        v_prev = prev_ref[0, :, v_cols] if sb == 0 else cur_ref[0, prev, v_cols]
        kcat = jnp.concatenate([k_prev, cur_ref[0, rows, k_cols]], axis=0)
        vcat = jnp.concatenate([v_prev, cur_ref[0, rows, v_cols]], axis=0)
        qs = cur_ref[0, rows, q_cols]
        q_heads, v_heads = [], []
        for p in range(pairs):
            qb = qs[:, p * pw:(p + 1) * pw]
            vb = vcat[:, p * pw:(p + 1) * pw]
            zero, ones = jnp.zeros_like(qb), jnp.ones_like(vb)
            q_heads += [jnp.where(first, qb, zero), jnp.where(first, zero, qb)]
            v_heads += [jnp.where(first, vb, ones), jnp.where(first, ones, vb)]
        ss = [_dot_t(q_heads[h], kcat[:, (h // 2) * pw:(h // 2 + 1) * pw]) + bias_ref[h] for h in heads]
        if sb == 0:
            ss = [jnp.where(has_prev, s, NEG) for s in ss]
        ms = [jnp.max(s, axis=-1, keepdims=True) for s in ss]
        prs = [jnp.exp2(ss[h] - ms[h]).astype(BF) for h in heads]
        pvs = [_dot(prs[h], v_heads[h]) for h in heads]
        for p in range(pairs):
            a0, a1 = pvs[2 * p], pvs[2 * p + 1]
            cols = slice(out_base + p * pw, out_base + (p + 1) * pw)
            l = pltpu.roll(jnp.where(first, a1, a0), HEAD_DIM, 1)
            o_ref[0, rows, cols] = (jnp.where(first, a0, a1) / l).astype(o_ref.dtype)
            lse_ref[0, rows, cols] = jnp.where(first, ms[2 * p], ms[2 * p + 1]) * LN2 + jnp.log(l)


def _dil_attention(view, bias, dilation, tl=512):
    r = dilation
    batch, length, _ = view.shape
    tl = min(tl, length)
    sub = tl // BAND
    nres = min(r, DIL_RES_PER_STEP)
    wide = nres * 3 * DIL_WIDTH
    cur = pl.BlockSpec((1, tl, wide), lambda bi, rg, j: (bi, j, rg))
    prev = pl.BlockSpec((1, BAND, wide), lambda bi, rg, j: (bi, jnp.maximum(j * sub - 1, 0), rg))
    out_spec = pl.BlockSpec((1, tl, nres * DIL_WIDTH), lambda bi, rg, j: (bi, j, rg))
    return pl.pallas_call(
        functools.partial(_dil_body, tl=tl, nres=nres),
        grid=(batch, r // nres, length // tl),
        in_specs=[cur, prev, pl.BlockSpec(bias.shape, lambda bi, rg, j: (0, 0, 0))],
        out_specs=[out_spec, out_spec],
        out_shape=[jax.ShapeDtypeStruct((batch, length, r * DIL_WIDTH), BF),
                   jax.ShapeDtypeStruct((batch, length, r * DIL_WIDTH), F32)],
        compiler_params=_cparams(("parallel", "parallel", "arbitrary")),
        name=f"dil_attn_r{r}",
    )(view, view, bias)


def _merge_body(h_ref, oa_ref, ob_ref, oc0_ref, oc1_ref, oc2_ref, l0_ref, l1_ref, l2_ref, kv_ref,
                gn_ref, wg_ref, gbias_ref, wa_ref, wb_ref, wc_ref, wo_ref, xg_ref, wxq_ref, wxo_ref,
                o_ref, *scratch):
    tm, d = h_ref.shape[1:]
    scratch = list(scratch)
    h = h_ref[0]
    u = _rms(h, gn_ref[...]).astype(BF)

    def gate(c):
        return _sigmoid(_dot(u, wg_ref[:, c * d:(c + 1) * d]) + gbias_ref[c:c + 1, :])

    def token_major(ref, r):
        if r == 1:
            return ref[0].astype(F32)
        buf = scratch.pop(0)
        chunks = DIL_WIDTH // LANES
        for res in range(r):
            for c in range(chunks):
                lo = res * DIL_WIDTH + c * LANES
                buf[c, pl.ds(res, tm // r, stride=r), :] = ref[0, :, lo:lo + LANES].astype(F32)
        return jnp.concatenate([buf[c] for c in range(chunks)], axis=1)

    dils = [dil for _, dil in DIL_GROUPS]
    merged = gate(0) * _dot(oa_ref[0], wa_ref[...]) + gate(1) * _dot(ob_ref[0], wb_ref[...])
    gate_c = gate(2)
    l0, l1, l2 = (token_major(ref, r) for ref, r in zip((l0_ref, l1_ref, l2_ref), dils))
    mx = jnp.maximum(jnp.maximum(l0, l1), l2)
    e0, e1, e2 = jnp.exp(l0 - mx), jnp.exp(l1 - mx), jnp.exp(l2 - mx)
    o0, o1, o2 = (token_major(ref, r) for ref, r in zip((oc0_ref, oc1_ref, oc2_ref), dils))
    oc = (e0 * o0 + e1 * o1 + e2 * o2) / (e0 + e1 + e2)
    merged = merged + gate_c * _dot(oc.astype(BF), wc_ref[...])
    mixed = h + _dot(merged.astype(BF), wo_ref[...])
    o_ref[0] = _cross_attend(mixed, xg_ref, kv_ref[0], wxq_ref, wxo_ref)


def _merge(h3, oa, ob, ocs, lses, kv, g_mix, w_gate, gbias, wa, wb, wc, wo, xg, wxq, wxo, tm=512):
    b, s, d = h3.shape
    tm = min(tm, s)
    row = lambda n: pl.BlockSpec((1, tm, n), lambda bi, i: (bi, i, 0))
    views = [pl.BlockSpec((1, tm // r, r * DIL_WIDTH), lambda bi, i: (bi, i, 0)) for _, r in DIL_GROUPS]
    n_buf = 2 * sum(1 for _, r in DIL_GROUPS if r > 1)
    params = (g_mix, w_gate, gbias, wa, wb, wc, wo, xg, wxq, wxo)
    kv_spec = pl.BlockSpec((1,) + kv.shape[1:], lambda bi, i: (bi, 0, 0))
    return pl.pallas_call(
        _merge_body,
        grid=(b, s // tm),
        in_specs=[row(d), row(SB_WIDTH), row(SB_WIDTH)] + views + views + [kv_spec]
        + [_resident(p.shape) for p in params],
        out_specs=row(d),
        out_shape=jax.ShapeDtypeStruct((b, s, d), F32),
        scratch_shapes=[pltpu.VMEM((DIL_WIDTH // LANES, tm, LANES), F32)] * n_buf,
        compiler_params=_cparams(("parallel", "parallel")),
        name="merge",
    )(h3, oa, ob, *ocs, *lses, kv, *params)


def _memkv_body(m_ref, g_ref, w_ref, o_ref):
    u = _rms(m_ref[0], g_ref[...]).astype(BF)
    o_ref[0] = _dot(u, w_ref[...]).astype(o_ref.dtype)


def _mem_kv(mem, g, w):
    b, m, d = mem.shape
    n = w.shape[1]
    return pl.pallas_call(
        _memkv_body,
        grid=(b,),
        in_specs=[pl.BlockSpec((1, m, d), lambda i: (i, 0, 0)),
                  pl.BlockSpec((1, d), lambda i: (0, 0)),
                  pl.BlockSpec((d, n), lambda i: (0, 0))],
        out_specs=pl.BlockSpec((1, m, n), lambda i: (i, 0, 0)),
        out_shape=jax.ShapeDtypeStruct((b, m, n), BF),
        compiler_params=_cparams(("parallel",)),
        name="mem_kv",
    )(mem, g, w)


def _cross_attend(x, g_ref, kv, wq_ref, wo_ref):
    u = _rms(x, g_ref[...]).astype(BF)
    q = (_dot(u, wq_ref[...]) * (X_HEAD_DIM ** -0.5 * LOG2E)).astype(BF)
    width = X_HEADS * X_HEAD_DIM
    heads = range(X_HEADS)
    cols = [slice(h * X_HEAD_DIM, (h + 1) * X_HEAD_DIM) for h in heads]
    ss = [_dot_t(q[:, cols[h]], kv[:, cols[h]]) for h in heads]
    es = [jnp.exp2(s - jnp.max(s, axis=-1, keepdims=True)) for s in ss]
    ps = [(e / jnp.sum(e, axis=-1, keepdims=True)).astype(BF) for e in es]
    outs = [_dot(ps[h], kv[:, width + h * X_HEAD_DIM:width + (h + 1) * X_HEAD_DIM]) for h in heads]
    o = jnp.concatenate(outs, axis=-1).astype(BF)
    return x + _dot(o, wo_ref[...])


def _rel_bucket(dist):
    exact = REL_BUCKETS // 2
    d = jnp.maximum(dist, exact).astype(F32)
    large = exact + (jnp.log(d / exact) / math.log(REL_MAX_DIST / exact)
                     * (REL_BUCKETS - exact)).astype(jnp.int32)
    return jnp.where(dist < exact, dist, jnp.minimum(large, REL_BUCKETS - 1))


def _band_bias(rel_bias, group, dilation):
    heads = DIL_WIDTH // HEAD_DIM
    qi = jnp.arange(BAND)
    kj = jnp.arange(2 * BAND)
    steps = (qi[:, None] + BAND) - kj[None, :]
    table = rel_bias[:, group * heads:(group + 1) * heads].astype(F32)
    bucket = _rel_bucket(jnp.clip(steps, 0, BAND) * dilation)
    onehot = (bucket[None] == jnp.arange(REL_BUCKETS)[:, None, None]).astype(F32)
    bias = jnp.einsum('bh,bqk->hqk', table * LOG2E, onehot, precision=lax.Precision.HIGHEST)
    return jnp.where((steps >= 0) & (steps <= BAND), bias, NEG)


def _swap_halves(w, width):
    k, n = w.shape
    w = w.reshape(k, n // width, 2, width // 2)
    return w[:, :, ::-1, :].reshape(k, n)


def _pack_w_in(w_in):
    d = w_in.shape[0]
    seg_a = w_in[:, :COL_KR].astype(BF)
    kr = w_in[:, COL_KR:COL_DIL]
    zeros = jnp.zeros((d, LANES - 2 * MLA_ROPE), w_in.dtype)
    kr_sw = _swap_halves(kr, MLA_ROPE)
    w_rope = jnp.concatenate([kr, kr, zeros, kr_sw, kr_sw, zeros], axis=1).astype(BF)
    group = 3 * DIL_WIDTH
    seg_d = [w_in[:, COL_DIL + g * group:COL_DIL + (g + 1) * group].astype(BF) for g in range(len(DIL_GROUPS))]
    seg_g = w_in[:, COL_DIL + len(DIL_GROUPS) * group:].astype(BF)
    return seg_a, w_rope, seg_d, seg_g


def _pack_w_uq(w_uq):
    k = w_uq.shape[0]
    w = w_uq.reshape(k, MLA_HEADS, MLA_NOPE + MLA_ROPE)
    nope = w[:, :, :MLA_NOPE].reshape(k, MLA_HEADS * MLA_NOPE)
    rope = w[:, :, MLA_NOPE:]

    def pair_layout(rp):
        rp = rp.reshape(k, MLA_HEADS // 2, 2 * MLA_ROPE)
        pad = jnp.zeros((k, MLA_HEADS // 2, 128 - 2 * MLA_ROPE), rp.dtype)
        return jnp.concatenate([rp, pad], axis=-1).reshape(k, (MLA_HEADS // 2) * 128)

    rope_sw = rope.reshape(k, MLA_HEADS, 2, MLA_ROPE // 2)[:, :, ::-1, :].reshape(k, MLA_HEADS, MLA_ROPE)
    return jnp.concatenate([nope, pair_layout(rope), pair_layout(rope_sw)], axis=1).astype(BF)


def _pack_w_ukv(w_ukv):
    k = w_ukv.shape[0]
    w = w_ukv.reshape(k, MLA_HEADS, MLA_NOPE + MLA_V)
    return jnp.concatenate([w[:, :, :MLA_NOPE].reshape(k, -1), w[:, :, MLA_NOPE:].reshape(k, -1)],
                           axis=1).astype(BF)


def _rope_tables(seq):
    half = MLA_ROPE // 2
    freqs = ROPE_THETA ** (-jnp.arange(half, dtype=F32) / half)
    ang = jnp.arange(seq).astype(F32)[:, None] * freqs[None, :]
    cos, sin = jnp.cos(ang), jnp.sin(ang)
    pad = jnp.zeros((seq, 128 - 2 * MLA_ROPE), F32)
    cos_t = jnp.concatenate([cos, cos, cos, cos, pad], axis=1)
    sin_t = jnp.concatenate([-sin, sin, -sin, sin, pad], axis=1)
    return cos_t, sin_t


def kernel(x, mem, ffn1_norm, ffn1_w_gate, ffn1_w_up, ffn1_w_down, mix_norm, w_in, gate_bias, mla_q_norm, mla_w_uq, mla_kv_norm, mla_w_ukv, w_branch_a, w_branch_b, w_branch_c, w_mix_out, rel_bias, xattn_norm, mem_norm, xattn_w_q, xattn_w_kv, xattn_w_o, ffn2_norm, ffn2_w_gate, ffn2_w_up, ffn2_w_down, final_norm):
    b, s, d = x.shape
    t = b * s
    depth = w_in.shape[0]
    cos_t, sin_t = _rope_tables(s)
    biases = [_band_bias(rel_bias, g, dil) for g, (_, dil) in enumerate(DIL_GROUPS)]
    row = lambda v: v.reshape(1, -1)
    bf = lambda w: w.astype(BF)
    fg = row(final_norm)

    h = x.reshape(t, d)
    for l in range(depth):
        h = _ffn(h, row(ffn1_norm[l]), bf(ffn1_w_gate[l]), bf(ffn1_w_up[l]), bf(ffn1_w_down[l]), fg, False)

        w_a, w_rope, w_d, w_g = _pack_w_in(w_in[l])
        g_mix = row(mix_norm[l])
        h3 = h.reshape(b, s, d)
        tm_a = min(MLA_BLOCK, s)
        assert DIL_GROUPS[0][1] == 1
        sb, d1, qp, kp, vt = _proj_a(h, g_mix, w_a, w_rope, w_d[0], cos_t, sin_t, row(mla_q_norm[l]),
                                     row(mla_kv_norm[l]), _pack_w_uq(mla_w_uq[l]), _pack_w_ukv(mla_w_ukv[l]),
                                     s, tm_a)
        o_a = _sb_attention(sb.reshape(b, s, sb.shape[1]))
        o_b = _mla_attention(qp.reshape(b, s, qp.shape[1]), kp.reshape(b, s, kp.shape[1]),
                             vt.reshape(b, s // tm_a, vt.shape[1], tm_a))
        ocs, lses = [], []
        for g, (_, dil) in enumerate(DIL_GROUPS):
            view = d1.reshape(b, s, d1.shape[1]) if g == 0 else _dil_proj(h3, g_mix, w_d[g], dil)
            o, lse = _dil_attention(view, biases[g], dil)
            ocs.append(o)
            lses.append(lse)
        kv = _mem_kv(mem, row(mem_norm[l]), bf(xattn_w_kv[l]))
        h = _merge(h3, o_a, o_b, ocs, lses, kv, g_mix, w_g, gate_bias[l],
                   bf(w_branch_a[l]), bf(w_branch_b[l]), bf(w_branch_c[l]), bf(w_mix_out[l]),
                   row(xattn_norm[l]), bf(xattn_w_q[l]), bf(xattn_w_o[l])).reshape(t, d)

        h = _ffn(h, row(ffn2_norm[l]), bf(ffn2_w_gate[l]), bf(ffn2_w_up[l]), bf(ffn2_w_down[l]), fg,
                 l == depth - 1)
    return h.reshape(b, s, d)
```

```python
import functools
import math

import jax
import jax.numpy as jnp
from jax import lax
from jax.experimental import pallas as pl
from jax.experimental.pallas import tpu as pltpu

BF = jnp.bfloat16
F32 = jnp.float32

EPS = 1e-6
NEG = -1e30
FFN_RESIDUAL = 0.5

SB_WIDTH = 512
MLA_HEADS = 8
MLA_Q_RANK = 384
MLA_KV_RANK = 128
MLA_NOPE = 64
MLA_ROPE = 32
MLA_V = 64
ROPE_THETA = 10000.0
DIL_GROUPS = ((128, 1), (512, 4), (2048, 16))
DIL_WIDTH = 512
REL_BUCKETS = 32
REL_MAX_DIST = 2048
X_HEADS = 4
X_HEAD_DIM = 128
HEAD_DIM = 64
BAND = 128
LANES = 128
MLA_PAIR = 2 * LANES

COL_CQ = 3 * SB_WIDTH
COL_CKV = COL_CQ + MLA_Q_RANK
COL_KR = COL_CKV + MLA_KV_RANK
COL_DIL = COL_KR + MLA_ROPE
SB_DEAD_LOG2 = -150.0
LOG2E = math.log2(math.e)
LN2 = math.log(2.0)

V7X_VMEM_LIMIT = 48 * 1024 * 1024
FF_CHUNK = 256
MLA_BLOCK = 512
DIL_RES_PER_STEP = 4
DIL_ROWS_PER_STEP = 2048
QSCALE = HEAD_DIM ** -0.5 * LOG2E


def _cparams(sem):
    return pltpu.CompilerParams(dimension_semantics=sem, vmem_limit_bytes=V7X_VMEM_LIMIT)


def _rms(x, g):
    ms = jnp.mean(x * x, axis=-1, keepdims=True)
    return x * lax.rsqrt(ms + EPS) * g


def _sigmoid(x):
    return 0.5 * jnp.tanh(0.5 * x) + 0.5


def _dot(a, b):
    return jnp.dot(a, b, preferred_element_type=F32)


def _dot_t(a, b):
    return lax.dot_general(a, b, (((1,), (1,)), ((), ())), preferred_element_type=F32)


def _resident(shape):
    nd = len(shape)
    return pl.BlockSpec(shape, lambda *_: (0,) * nd, pipeline_mode=pl.Buffered(1))


def _ffn_body(h_ref, g_ref, wg_ref, wu_ref, wd_ref, fg_ref, o_ref, act_ref, *, final):
    x = h_ref[...]
    u = _rms(x, g_ref[...]).astype(BF)
    d_ff = wg_ref.shape[1]
    for c in range(d_ff // FF_CHUNK):
        sl = slice(c * FF_CHUNK, (c + 1) * FF_CHUNK)
        a = _dot(u, wg_ref[:, sl])
        b = _dot(u, wu_ref[:, sl])
        act_ref[:, sl] = (a * jax.nn.sigmoid(a) * b).astype(BF)
    y = x + FFN_RESIDUAL * _dot(act_ref[...], wd_ref[...])
    if final:
        y = _rms(y, fg_ref[...])
    o_ref[...] = y


def _ffn(h, g, wg, wu, wd, fg, final, tm=512):
    t, d = h.shape
    d_ff = wg.shape[1]
    tm = min(tm, t)
    return pl.pallas_call(
        functools.partial(_ffn_body, final=final),
        grid=(t // tm,),
        in_specs=[
            pl.BlockSpec((tm, d), lambda i: (i, 0)),
            _resident((1, d)),
            _resident((d, d_ff)),
            _resident((d, d_ff)),
            _resident((d_ff, d)),
            _resident((1, d)),
        ],
        out_specs=pl.BlockSpec((tm, d), lambda i: (i, 0)),
        out_shape=jax.ShapeDtypeStruct((t, d), F32),
        scratch_shapes=[pltpu.VMEM((tm, d_ff), BF)],
        compiler_params=_cparams(("parallel",)),
        name="ffn",
    )(h, g, wg, wu, wd, fg)


def _proj_a_body(h_ref, g_ref, w_ref, wr_ref, wd_ref, cos_ref, sin_ref, qg_ref, kvg_ref, wq_ref, wkv_ref,
                 sb_ref, d1_ref, qp_ref, kp_ref, vt_ref, *, scale):
    u = _rms(h_ref[...], g_ref[...]).astype(BF)
    for wt_ref, out_ref in ((w_ref, sb_ref), (wd_ref, d1_ref)):
        for c, factor in enumerate((QSCALE, None, None)):
            sl = slice(c * SB_WIDTH, (c + 1) * SB_WIDTH)
            y = _dot(u, wt_ref[:, sl])
            out_ref[:, sl] = (y if factor is None else y * factor).astype(out_ref.dtype)
    lat = _dot(u, w_ref[:, COL_CQ:COL_KR])
    kr = _dot(u, wr_ref[...])
    c_kv0 = COL_CKV - COL_CQ
    nq = _rms(lat[:, :c_kv0], qg_ref[...]).astype(BF)
    q = _dot(nq, wq_ref[...])
    nkv = _rms(lat[:, c_kv0:], kvg_ref[...]).astype(BF)
    kv = _dot(nkv, wkv_ref[...])
    cos = cos_ref[...]
    sin = sin_ref[...]
    k_pe = (kr[:, :LANES] * cos + kr[:, LANES:] * sin).astype(BF)
    width = q.shape[1] // 3
    for p in range(width // LANES):
        lo, hi = p * LANES, (p + 1) * LANES
        qp_ref[:, 2 * lo:2 * lo + LANES] = (q[:, lo:hi] * scale).astype(BF)
        q_pe = q[:, width + lo:width + hi] * cos + q[:, 2 * width + lo:2 * width + hi] * sin
        qp_ref[:, 2 * lo + LANES:2 * hi] = (q_pe * scale).astype(BF)
        kp_ref[:, 2 * lo:2 * lo + LANES] = kv[:, lo:hi].astype(BF)
        kp_ref[:, 2 * lo + LANES:2 * hi] = k_pe
    vt_ref[0] = kv[:, width:].T.astype(BF)


def _proj_a(h, g, w, w_rope, w_dil1, cos, sin, qg, kvg, wq, wkv, seq, tm):
    t, d = h.shape
    width = MLA_HEADS * MLA_NOPE
    scale = (MLA_NOPE + MLA_ROPE) ** -0.5 * LOG2E
    per_seq = seq // tm
    params = (g, w, w_rope, w_dil1)
    tables = [pl.BlockSpec((tm, LANES), lambda i: (i % per_seq, 0))] * 2
    mla_params = (qg, kvg, wq, wkv)
    row = lambda n: pl.BlockSpec((tm, n), lambda i: (i, 0))
    return pl.pallas_call(
        functools.partial(_proj_a_body, scale=scale),
        grid=(t // tm,),
        in_specs=[row(d)] + [_resident(p.shape) for p in params] + tables + [_resident(p.shape) for p in mla_params],
        out_specs=[row(COL_CQ), row(3 * DIL_WIDTH), row(2 * width), row(2 * width),
                   pl.BlockSpec((1, width, tm), lambda i: (i, 0, 0))],
        out_shape=[jax.ShapeDtypeStruct((t, COL_CQ), BF), jax.ShapeDtypeStruct((t, 3 * DIL_WIDTH), BF),
                   jax.ShapeDtypeStruct((t, 2 * width), BF), jax.ShapeDtypeStruct((t, 2 * width), BF),
                   jax.ShapeDtypeStruct((t // tm, width, tm), BF)],
        compiler_params=_cparams(("parallel",)),
        name="proj_a",
    )(h, g, w, w_rope, w_dil1, cos, sin, qg, kvg, wq, wkv)


def _dil_proj_body(h_ref, g_ref, w_ref, o_ref, stage, *, r):
    rows = h_ref.shape[1] // r
    n = w_ref.shape[1]
    u = _rms(h_ref[0], g_ref[...])
    chunks = u.shape[1] // LANES
    for c in range(chunks):
        stage[c] = u[:, c * LANES:(c + 1) * LANES]
    for res in range(r):
        ur = jnp.concatenate([stage[c, pl.ds(res, rows, stride=r), :] for c in range(chunks)], axis=1).astype(BF)
        for c, factor in enumerate((QSCALE, None, None)):
            y = _dot(ur, w_ref[:, c * DIL_WIDTH:(c + 1) * DIL_WIDTH])
            lo = res * n + c * DIL_WIDTH
            o_ref[0, :, lo:lo + DIL_WIDTH] = (y if factor is None else y * factor).astype(o_ref.dtype)


def _dil_proj(h3, g, w, r):
    b, s, d = h3.shape
    n = w.shape[1]
    tm = min(s, max(1024, BAND * r))
    return pl.pallas_call(
        functools.partial(_dil_proj_body, r=r),
        grid=(b, s // tm),
        in_specs=[
            pl.BlockSpec((1, tm, d), lambda bi, i: (bi, i, 0)),
            pl.BlockSpec((1, d), lambda bi, i: (0, 0)),
            pl.BlockSpec((d, n), lambda bi, i: (0, 0)),
        ],
        out_specs=pl.BlockSpec((1, tm // r, r * n), lambda bi, i: (bi, i, 0)),
        out_shape=jax.ShapeDtypeStruct((b, s // r, r * n), BF),
        scratch_shapes=[pltpu.VMEM((d // LANES, tm, LANES), F32)],
        compiler_params=_cparams(("parallel", "parallel")),
        name=f"dil_proj_r{r}",
    )(h3, g, w)


def _sb_body(q_ref, k_ref, v_ref, o_ref, *, tq, pairs):
    i = pl.program_id(2)
    pw = 2 * HEAD_DIM
    lane = lax.broadcasted_iota(jnp.int32, (1, pw), 1)
    first = lane < HEAD_DIM
    q_heads = []
    for p in range(pairs):
        q = q_ref[0, :, p * pw:(p + 1) * pw]
        zero = jnp.zeros_like(q)
        q_heads += [jnp.where(first, q, zero), jnp.where(first, zero, q)]
    row = lax.broadcasted_iota(jnp.int32, (tq, tq), 0)
    col = lax.broadcasted_iota(jnp.int32, (tq, tq), 1)
    strict = col < row
    later = jnp.where(row > col, 1.0, 0.0).astype(BF)

    def block(j, accs, rs, diag):
        start = pl.multiple_of(j * tq, tq)
        heads = range(2 * pairs)
        ks = [k_ref[0, pl.ds(start, tq), p * pw:(p + 1) * pw] for p in range(pairs)]
        vs = [v_ref[0, pl.ds(start, tq), p * pw:(p + 1) * pw] for p in range(pairs)]
        zs = [_dot_t(q_heads[h], ks[h // 2]) for h in heads]
        log_betas, log_keeps = [], []
        for z in zs:
            sp = jnp.log(1.0 + jnp.exp2(-jnp.abs(z))) * LOG2E
            log_beta = jnp.minimum(z, 0.0) - sp
            log_keep = log_beta - z
            if diag:
                log_keep = jnp.where(strict, log_keep, 0.0)
            log_betas.append(log_beta)
            log_keeps.append(log_keep)
        afters = [_dot(log_keeps[h].astype(BF), later) for h in heads]
        ws = []
        for h in heads:
            w = jnp.exp2(log_betas[h] + afters[h] + rs[h])
            if diag:
                w = jnp.where(strict, w, 0.0)
            ws.append(w.astype(BF))
        outs = [_dot(ws[h], vs[h // 2]) for h in heads]
        new_rs = [rs[h] + jnp.sum(log_keeps[h], axis=-1, keepdims=True) for h in heads]
        new_accs = [accs[p] + jnp.where(first, outs[2 * p], outs[2 * p + 1]) for p in range(pairs)]
        return tuple(new_accs), tuple(new_rs)

    def remaining(rs):
        return jnp.max(functools.reduce(jnp.maximum, rs))

    def cond(c):
        return (c[0] < i) & (c[1] > SB_DEAD_LOG2)

    def body(c):
        accs, rs = block(i - 1 - c[0], c[2], c[3], False)
        return c[0] + 1, remaining(rs), accs, rs

    accs = tuple(jnp.zeros((tq, pw), F32) for _ in range(pairs))
    rs = tuple(jnp.zeros((tq, 1), F32) for _ in range(2 * pairs))
    accs, rs = block(i, accs, rs, True)
    out = lax.while_loop(cond, body, (jnp.int32(0), remaining(rs), accs, rs))
    for p in range(pairs):
        o_ref[0, :, p * pw:(p + 1) * pw] = out[2][p].astype(o_ref.dtype)


def _sb_attention(a3, tq=256, pairs=4):
    b, s, _ = a3.shape
    tq = min(tq, s)
    w = 2 * HEAD_DIM * pairs
    groups = SB_WIDTH // w
    return pl.pallas_call(
        functools.partial(_sb_body, tq=tq, pairs=pairs),
        grid=(b, groups, s // tq),
        in_specs=[
            pl.BlockSpec((1, tq, w), lambda bi, p, i: (bi, i, p)),
            pl.BlockSpec((1, s, w), lambda bi, p, i: (bi, 0, groups + p)),
            pl.BlockSpec((1, s, w), lambda bi, p, i: (bi, 0, 2 * groups + p)),
        ],
        out_specs=pl.BlockSpec((1, tq, w), lambda bi, p, i: (bi, i, p)),
        out_shape=jax.ShapeDtypeStruct((b, s, SB_WIDTH), BF),
        compiler_params=_cparams(("parallel", "parallel", "arbitrary")),
        name="sb_attn",
    )(a3, a3, a3)


def _mla_body(q_ref, k_ref, v_ref, o_ref, *, tq, pairs):
    i = pl.program_id(2)
    lane = lax.broadcasted_iota(jnp.int32, (1, MLA_PAIR), 1)
    sel0 = (lane < 64) | ((lane >= 128) & (lane < 160))
    sel1 = ((lane >= 64) & (lane < 128)) | ((lane >= 160) & (lane < 192))
    q_heads = []
    for p in range(pairs):
        q = q_ref[0, :, p * MLA_PAIR:(p + 1) * MLA_PAIR]
        zero = jnp.zeros_like(q)
        q_heads += [jnp.where(sel0, q, zero), jnp.where(sel1, q, zero)]
    key = lax.broadcasted_iota(jnp.int32, (tq, tq), 0)
    qry = lax.broadcasted_iota(jnp.int32, (tq, tq), 1)
    causal = key <= qry
    first = lax.broadcasted_iota(jnp.int32, (128, 1), 0) < 64

    def block(j, carry, diag, nblk=1):
        start = pl.multiple_of(j * tq, tq)
        heads = range(2 * pairs)
        ks = [k_ref[0, pl.ds(start, nblk * tq), p * MLA_PAIR:(p + 1) * MLA_PAIR] for p in range(pairs)]
        v_heads = []
        for p in range(pairs):
            vt = jnp.concatenate([v_ref[0, j + n, p * 128:(p + 1) * 128, :] for n in range(nblk)], axis=1)
            ones = jnp.ones_like(vt)
            v_heads += [jnp.where(first, vt, ones), jnp.where(first, ones, vt)]
        ss = [_dot_t(ks[h // 2], q_heads[h]) for h in heads]
        if diag:
            ss = [jnp.where(causal, s, NEG) for s in ss]
        m_news = [jnp.maximum(carry[h][0], jnp.max(ss[h], axis=0, keepdims=True)) for h in heads]
        prs = [jnp.exp2(ss[h] - m_news[h]).astype(BF) for h in heads]
        pvs = [_dot(v_heads[h], prs[h]) for h in heads]
        return tuple((m_news[h], jnp.exp2(carry[h][0] - m_news[h]) * carry[h][1] + pvs[h]) for h in heads)

    one = (jnp.full((1, tq), NEG, F32), jnp.zeros((128, tq), F32))
    carry = lax.fori_loop(0, i // 2, lambda t, c: block(2 * t, c, False, 2), (one,) * (2 * pairs))
    carry = lax.cond(i % 2 == 1, lambda c: block(i - 1, c, False), lambda c: c, carry)
    carry = block(i, carry, True)
    for p in range(pairs):
        a0, a1 = carry[2 * p][1], carry[2 * p + 1][1]
        lsw = jnp.where(first, a1, a0)
        l = jnp.concatenate([lsw[64:], lsw[:64]], axis=0)
        out_t = jnp.where(first, a0, a1) / l
        o_ref[0, :, p * 128:(p + 1) * 128] = out_t.T.astype(o_ref.dtype)


def _mla_attention(qp, kp, vt, pairs=2):
    b, s, _ = qp.shape
    nblk, width, tq = vt.shape[1:]
    groups = MLA_HEADS // (2 * pairs)
    return pl.pallas_call(
        functools.partial(_mla_body, tq=tq, pairs=pairs),
        grid=(b, groups, s // tq),
        in_specs=[
            pl.BlockSpec((1, tq, MLA_PAIR * pairs), lambda bi, p, i: (bi, i, p)),
            pl.BlockSpec((1, s, MLA_PAIR * pairs), lambda bi, p, i: (bi, 0, p)),
            pl.BlockSpec((1, nblk, 128 * pairs, tq), lambda bi, p, i: (bi, 0, p, 0)),
        ],
        out_specs=pl.BlockSpec((1, tq, 128 * pairs), lambda bi, p, i: (bi, i, p)),
        out_shape=jax.ShapeDtypeStruct((b, s, MLA_HEADS * MLA_V), BF),
        compiler_params=_cparams(("parallel", "parallel", "arbitrary")),
        name="mla_attn",
    )(qp, kp, vt)


def _dil_body(cur_ref, prev_ref, bias_ref, o_ref, lse_ref, *, tl, nres):
    j = pl.program_id(2)
    lane = lax.broadcasted_iota(jnp.int32, (1, 2 * HEAD_DIM), 1)
    first = lane < HEAD_DIM
    kj = lax.broadcasted_iota(jnp.int32, (BAND, 2 * BAND), 1)
    has_prev = (kj >= BAND) | (j > 0)
    pw = 2 * HEAD_DIM
    pairs = DIL_WIDTH // pw
    heads = range(2 * pairs)
    for rr, sb in [(rr, sb) for rr in range(nres) for sb in range(tl // BAND)]:
        rows = slice(sb * BAND, (sb + 1) * BAND)
        prev = slice((sb - 1) * BAND, sb * BAND)
        q_cols, k_cols, v_cols = (slice((3 * rr + c) * DIL_WIDTH, (3 * rr + c + 1) * DIL_WIDTH) for c in range(3))
        out_base = rr * DIL_WIDTH
        k_prev = prev_ref[0, :, k_cols] if sb == 0 else cur_ref[0, prev, k_cols]
        v_prev = prev_ref[0, :, v_cols] if sb == 0 else cur_ref[0, prev, v_cols]
        kcat = jnp.concatenate([k_prev, cur_ref[0, rows, k_cols]], axis=0)
        vcat = jnp.concatenate([v_prev, cur_ref[0, rows, v_cols]], axis=0)
        qs = cur_ref[0, rows, q_cols]
        q_heads, v_heads = [], []
        for p in range(pairs):
            qb = qs[:, p * pw:(p + 1) * pw]
            vb = vcat[:, p * pw:(p + 1) * pw]
            zero, ones = jnp.zeros_like(qb), jnp.ones_like(vb)
            q_heads += [jnp.where(first, qb, zero), jnp.where(first, zero, qb)]
            v_heads += [jnp.where(first, vb, ones), jnp.where(first, ones, vb)]
        ss = [_dot_t(q_heads[h], kcat[:, (h // 2) * pw:(h // 2 + 1) * pw]) + bias_ref[h] for h in heads]
        if sb == 0:
            ss = [jnp.where(has_prev, s, NEG) for s in ss]
        ms = [jnp.max(s, axis=-1, keepdims=True) for s in ss]
        prs = [jnp.exp2(ss[h] - ms[h]).astype(BF) for h in heads]
        pvs = [_dot(prs[h], v_heads[h]) for h in heads]
        for p in range(pairs):
            a0, a1 = pvs[2 * p], pvs[2 * p + 1]
            cols = slice(out_base + p * pw, out_base + (p + 1) * pw)
            l = pltpu.roll(jnp.where(first, a1, a0), HEAD_DIM, 1)
            o_ref[0, rows, cols] = (jnp.where(first, a0, a1) / l).astype(o_ref.dtype)
            lse_ref[0, rows, cols] = jnp.where(first, ms[2 * p], ms[2 * p + 1]) * LN2 + jnp.log(l)


def _dil_attention(view, bias, dilation):
    r = dilation
    batch, length, _ = view.shape
    nres = min(r, DIL_RES_PER_STEP)
    tl = min(length, DIL_ROWS_PER_STEP // nres)
    sub = tl // BAND
    wide = nres * 3 * DIL_WIDTH
    cur = pl.BlockSpec((1, tl, wide), lambda bi, rg, j: (bi, j, rg))
    prev = pl.BlockSpec((1, BAND, wide), lambda bi, rg, j: (bi, jnp.maximum(j * sub - 1, 0), rg))
    out_spec = pl.BlockSpec((1, tl, nres * DIL_WIDTH), lambda bi, rg, j: (bi, j, rg))
    return pl.pallas_call(
        functools.partial(_dil_body, tl=tl, nres=nres),
        grid=(batch, r // nres, length // tl),
        in_specs=[cur, prev, pl.BlockSpec(bias.shape, lambda bi, rg, j: (0, 0, 0))],
        out_specs=[out_spec, out_spec],
        out_shape=[jax.ShapeDtypeStruct((batch, length, r * DIL_WIDTH), BF),
                   jax.ShapeDtypeStruct((batch, length, r * DIL_WIDTH), F32)],
        compiler_params=_cparams(("parallel", "parallel", "arbitrary")),
        name=f"dil_attn_r{r}",
    )(view, view, bias)


def _merge_body(h_ref, oa_ref, ob_ref, oc0_ref, oc1_ref, oc2_ref, l0_ref, l1_ref, l2_ref, kv_ref,
                gn_ref, wg_ref, gbias_ref, wa_ref, wb_ref, wc_ref, wo_ref, xg_ref, wxq_ref, wxo_ref,
                o_ref, *scratch):
    tm, d = h_ref.shape[1:]
    scratch = list(scratch)
    h = h_ref[0]
    u = _rms(h, gn_ref[...]).astype(BF)

    def gate(c):
        return _sigmoid(_dot(u, wg_ref[:, c * d:(c + 1) * d]) + gbias_ref[c:c + 1, :])

    def token_major(ref, r):
        if r == 1:
            return ref[0].astype(F32)
        buf = scratch.pop(0)
        chunks = DIL_WIDTH // LANES
        for res in range(r):
            for c in range(chunks):
                lo = res * DIL_WIDTH + c * LANES
                buf[c, pl.ds(res, tm // r, stride=r), :] = ref[0, :, lo:lo + LANES].astype(F32)
        return jnp.concatenate([buf[c] for c in range(chunks)], axis=1)

    dils = [dil for _, dil in DIL_GROUPS]
    merged = gate(0) * _dot(oa_ref[0], wa_ref[...]) + gate(1) * _dot(ob_ref[0], wb_ref[...])
    gate_c = gate(2)
    l0, l1, l2 = (token_major(ref, r) for ref, r in zip((l0_ref, l1_ref, l2_ref), dils))
    mx = jnp.maximum(jnp.maximum(l0, l1), l2)
    e0, e1, e2 = jnp.exp(l0 - mx), jnp.exp(l1 - mx), jnp.exp(l2 - mx)
    o0, o1, o2 = (token_major(ref, r) for ref, r in zip((oc0_ref, oc1_ref, oc2_ref), dils))
    oc = (e0 * o0 + e1 * o1 + e2 * o2) / (e0 + e1 + e2)
    merged = merged + gate_c * _dot(oc.astype(BF), wc_ref[...])
    mixed = h + _dot(merged.astype(BF), wo_ref[...])
    o_ref[0] = _cross_attend(mixed, xg_ref, kv_ref[0], wxq_ref, wxo_ref)


def _merge(h3, oa, ob, ocs, lses, kv, g_mix, w_gate, gbias, wa, wb, wc, wo, xg, wxq, wxo, tm=512):
    b, s, d = h3.shape
    tm = min(tm, s)
    row = lambda n: pl.BlockSpec((1, tm, n), lambda bi, i: (bi, i, 0))
    views = [pl.BlockSpec((1, tm // r, r * DIL_WIDTH), lambda bi, i: (bi, i, 0)) for _, r in DIL_GROUPS]
    n_buf = 2 * sum(1 for _, r in DIL_GROUPS if r > 1)
    params = (g_mix, w_gate, gbias, wa, wb, wc, wo, xg, wxq, wxo)
    kv_spec = pl.BlockSpec((1,) + kv.shape[1:], lambda bi, i: (bi, 0, 0))
    return pl.pallas_call(
        _merge_body,
        grid=(b, s // tm),
        in_specs=[row(d), row(SB_WIDTH), row(SB_WIDTH)] + views + views + [kv_spec]
        + [_resident(p.shape) for p in params],
        out_specs=row(d),
        out_shape=jax.ShapeDtypeStruct((b, s, d), F32),
        scratch_shapes=[pltpu.VMEM((DIL_WIDTH // LANES, tm, LANES), F32)] * n_buf,
        compiler_params=_cparams(("parallel", "parallel")),
        name="merge",
    )(h3, oa, ob, *ocs, *lses, kv, *params)


def _memkv_body(m_ref, g_ref, w_ref, o_ref):
    u = _rms(m_ref[0], g_ref[...]).astype(BF)
    o_ref[0] = _dot(u, w_ref[...]).astype(o_ref.dtype)


def _mem_kv(mem, g, w):
    b, m, d = mem.shape
    n = w.shape[1]
    return pl.pallas_call(
        _memkv_body,
        grid=(b,),
        in_specs=[pl.BlockSpec((1, m, d), lambda i: (i, 0, 0)),
                  pl.BlockSpec((1, d), lambda i: (0, 0)),
                  pl.BlockSpec((d, n), lambda i: (0, 0))],
        out_specs=pl.BlockSpec((1, m, n), lambda i: (i, 0, 0)),
        out_shape=jax.ShapeDtypeStruct((b, m, n), BF),
        compiler_params=_cparams(("parallel",)),
        name="mem_kv",
    )(mem, g, w)


def _cross_attend(x, g_ref, kv, wq_ref, wo_ref):
    u = _rms(x, g_ref[...]).astype(BF)
    q = (_dot(u, wq_ref[...]) * (X_HEAD_DIM ** -0.5 * LOG2E)).astype(BF)
    width = X_HEADS * X_HEAD_DIM
    heads = range(X_HEADS)
    cols = [slice(h * X_HEAD_DIM, (h + 1) * X_HEAD_DIM) for h in heads]
    ss = [_dot_t(q[:, cols[h]], kv[:, cols[h]]) for h in heads]
    es = [jnp.exp2(s - jnp.max(s, axis=-1, keepdims=True)) for s in ss]
    ps = [(e / jnp.sum(e, axis=-1, keepdims=True)).astype(BF) for e in es]
    outs = [_dot(ps[h], kv[:, width + h * X_HEAD_DIM:width + (h + 1) * X_HEAD_DIM]) for h in heads]
    o = jnp.concatenate(outs, axis=-1).astype(BF)
    return x + _dot(o, wo_ref[...])


def _rel_bucket(dist):
    exact = REL_BUCKETS // 2
    d = jnp.maximum(dist, exact).astype(F32)
    large = exact + (jnp.log(d / exact) / math.log(REL_MAX_DIST / exact)
                     * (REL_BUCKETS - exact)).astype(jnp.int32)
    return jnp.where(dist < exact, dist, jnp.minimum(large, REL_BUCKETS - 1))


def _band_bias(rel_bias, group, dilation):
    heads = DIL_WIDTH // HEAD_DIM
    qi = jnp.arange(BAND)
    kj = jnp.arange(2 * BAND)
    steps = (qi[:, None] + BAND) - kj[None, :]
    table = rel_bias[:, group * heads:(group + 1) * heads].astype(F32)
    bucket = _rel_bucket(jnp.clip(steps, 0, BAND) * dilation)
    onehot = (bucket[None] == jnp.arange(REL_BUCKETS)[:, None, None]).astype(F32)
    bias = jnp.einsum('bh,bqk->hqk', table * LOG2E, onehot, precision=lax.Precision.HIGHEST)
    return jnp.where((steps >= 0) & (steps <= BAND), bias, NEG)


def _swap_halves(w, width):
    k, n = w.shape
    w = w.reshape(k, n // width, 2, width // 2)
    return w[:, :, ::-1, :].reshape(k, n)


def _pack_w_in(w_in):
    d = w_in.shape[0]
    seg_a = w_in[:, :COL_KR].astype(BF)
    kr = w_in[:, COL_KR:COL_DIL]
    zeros = jnp.zeros((d, LANES - 2 * MLA_ROPE), w_in.dtype)
    kr_sw = _swap_halves(kr, MLA_ROPE)
    w_rope = jnp.concatenate([kr, kr, zeros, kr_sw, kr_sw, zeros], axis=1).astype(BF)
    group = 3 * DIL_WIDTH
    seg_d = [w_in[:, COL_DIL + g * group:COL_DIL + (g + 1) * group].astype(BF) for g in range(len(DIL_GROUPS))]
    seg_g = w_in[:, COL_DIL + len(DIL_GROUPS) * group:].astype(BF)
    return seg_a, w_rope, seg_d, seg_g


def _pack_w_uq(w_uq):
    k = w_uq.shape[0]
    w = w_uq.reshape(k, MLA_HEADS, MLA_NOPE + MLA_ROPE)
    nope = w[:, :, :MLA_NOPE].reshape(k, MLA_HEADS * MLA_NOPE)
    rope = w[:, :, MLA_NOPE:]

    def pair_layout(rp):
        rp = rp.reshape(k, MLA_HEADS // 2, 2 * MLA_ROPE)
        pad = jnp.zeros((k, MLA_HEADS // 2, 128 - 2 * MLA_ROPE), rp.dtype)
        return jnp.concatenate([rp, pad], axis=-1).reshape(k, (MLA_HEADS // 2) * 128)

    rope_sw = rope.reshape(k, MLA_HEADS, 2, MLA_ROPE // 2)[:, :, ::-1, :].reshape(k, MLA_HEADS, MLA_ROPE)
    return jnp.concatenate([nope, pair_layout(rope), pair_layout(rope_sw)], axis=1).astype(BF)


def _pack_w_ukv(w_ukv):
    k = w_ukv.shape[0]
    w = w_ukv.reshape(k, MLA_HEADS, MLA_NOPE + MLA_V)
    return jnp.concatenate([w[:, :, :MLA_NOPE].reshape(k, -1), w[:, :, MLA_NOPE:].reshape(k, -1)],
                           axis=1).astype(BF)


def _rope_tables(seq):
    half = MLA_ROPE // 2
    freqs = ROPE_THETA ** (-jnp.arange(half, dtype=F32) / half)
    ang = jnp.arange(seq).astype(F32)[:, None] * freqs[None, :]
    cos, sin = jnp.cos(ang), jnp.sin(ang)
    pad = jnp.zeros((seq, 128 - 2 * MLA_ROPE), F32)
    cos_t = jnp.concatenate([cos, cos, cos, cos, pad], axis=1)
    sin_t = jnp.concatenate([-sin, sin, -sin, sin, pad], axis=1)
    return cos_t, sin_t


def kernel(x, mem, ffn1_norm, ffn1_w_gate, ffn1_w_up, ffn1_w_down, mix_norm, w_in, gate_bias, mla_q_norm, mla_w_uq, mla_kv_norm, mla_w_ukv, w_branch_a, w_branch_b, w_branch_c, w_mix_out, rel_bias, xattn_norm, mem_norm, xattn_w_q, xattn_w_kv, xattn_w_o, ffn2_norm, ffn2_w_gate, ffn2_w_up, ffn2_w_down, final_norm):
    b, s, d = x.shape
    t = b * s
    depth = w_in.shape[0]
    cos_t, sin_t = _rope_tables(s)
    biases = [_band_bias(rel_bias, g, dil) for g, (_, dil) in enumerate(DIL_GROUPS)]
    row = lambda v: v.reshape(1, -1)
    bf = lambda w: w.astype(BF)
    fg = row(final_norm)

    h = x.reshape(t, d)
    for l in range(depth):
        h = _ffn(h, row(ffn1_norm[l]), bf(ffn1_w_gate[l]), bf(ffn1_w_up[l]), bf(ffn1_w_down[l]), fg, False)

        w_a, w_rope, w_d, w_g = _pack_w_in(w_in[l])
        g_mix = row(mix_norm[l])
        h3 = h.reshape(b, s, d)
        tm_a = min(MLA_BLOCK, s)
        assert DIL_GROUPS[0][1] == 1
        sb, d1, qp, kp, vt = _proj_a(h, g_mix, w_a, w_rope, w_d[0], cos_t, sin_t, row(mla_q_norm[l]),
                                     row(mla_kv_norm[l]), _pack_w_uq(mla_w_uq[l]), _pack_w_ukv(mla_w_ukv[l]),
                                     s, tm_a)
        o_a = _sb_attention(sb.reshape(b, s, sb.shape[1]))
        o_b = _mla_attention(qp.reshape(b, s, qp.shape[1]), kp.reshape(b, s, kp.shape[1]),
                             vt.reshape(b, s // tm_a, vt.shape[1], tm_a))
        ocs, lses = [], []
        for g, (_, dil) in enumerate(DIL_GROUPS):
            view = d1.reshape(b, s, d1.shape[1]) if g == 0 else _dil_proj(h3, g_mix, w_d[g], dil)
            o, lse = _dil_attention(view, biases[g], dil)
            ocs.append(o)
            lses.append(lse)
        kv = _mem_kv(mem, row(mem_norm[l]), bf(xattn_w_kv[l]))
        h = _merge(h3, o_a, o_b, ocs, lses, kv, g_mix, w_g, gate_bias[l],
                   bf(w_branch_a[l]), bf(w_branch_b[l]), bf(w_branch_c[l]), bf(w_mix_out[l]),
                   row(xattn_norm[l]), bf(xattn_w_q[l]), bf(xattn_w_o[l])).reshape(t, d)

        h = _ffn(h, row(ffn2_norm[l]), bf(ffn2_w_gate[l]), bf(ffn2_w_up[l]), bf(ffn2_w_down[l]), fg,
                 l == depth - 1)
    return h.reshape(b, s, d)
```
